```python
import jax
import jax.numpy as jnp
from jax import lax

D_MODEL = 1024
BATCH = 8
SEQ = 4096
DEPTH = 2

CTX_LEN = 256
GRID_W = 64
D_MIX = D_MODEL

CONV_W = D_MIX // 4
ATT_HD = 64
ATT_Q_HEADS = (D_MIX // 2) // ATT_HD
ATT_KV_HEADS = ATT_Q_HEADS // 4
ATT_GROUP = ATT_Q_HEADS // ATT_KV_HEADS
ROPE_AXIS_DIM = ATT_HD // 2
ROPE_THETA = 10000.0
Q_BLOCK = 128
RW_HD = 64
RW = D_MIX // 4
RW_HEADS = RW // RW_HD
DECAY_LORA = 64
AAA_LORA = 64
GATE_LORA = 160
N_DIR = 2
GN_EPS = 64e-5
CONV_COLS = 3 * CONV_W
ATT_COLS = (ATT_Q_HEADS + 2 * ATT_KV_HEADS) * ATT_HD
RW_COLS = 3 * RW + N_DIR * (DECAY_LORA + AAA_LORA) + GATE_LORA
PROJ_COLS = CONV_COLS + ATT_COLS + RW_COLS
N_EXPERTS = 16
CAPACITY_FACTOR = 2
D_EXPERT = D_MODEL
N_MOD = 6
NORM_EPS = 1e-6

kernel_name = 'hybrid_conv_gqa_rwkv7_ec_moe_dit'


def rms_norm(x, g):
    xf = x.astype(jnp.float32)
    y = xf * lax.rsqrt(jnp.mean(xf * xf, axis=-1, keepdims=True) + NORM_EPS)
    return (y * g.astype(jnp.float32)).astype(x.dtype)


def adaln_params(cvec, w, b):
    m = (jax.nn.silu(cvec) @ w + b).reshape(cvec.shape[0], 1, N_MOD, D_MODEL)
    return [m[:, :, i] for i in range(N_MOD)]


def modulate(x, g, shift, scale):
    return rms_norm(x, g) * (1 + scale) + shift


def shift_prev(z):
    return jnp.pad(z, ((0, 0), (1, 0), (0, 0)))[:, :-1]


def shift_next(z):
    return jnp.pad(z, ((0, 0), (0, 1), (0, 0)))[:, 1:]


def short_conv_mixer(p_cv, conv_w):
    b_gate, c_gate, u = jnp.split(p_cv, 3, axis=-1)
    z = c_gate * u
    z = conv_w[0] * shift_prev(z) + conv_w[1] * z + conv_w[2] * shift_next(z)
    return b_gate * z


def attn_qkv(p_at, q_g, k_g):
    bsz, n, _ = p_at.shape
    q, k, v = jnp.split(p_at, [ATT_Q_HEADS * ATT_HD, (ATT_Q_HEADS + ATT_KV_HEADS) * ATT_HD], axis=-1)
    q = rms_norm(q.reshape(bsz, n, ATT_Q_HEADS, ATT_HD), q_g)
    k = rms_norm(k.reshape(bsz, n, ATT_KV_HEADS, ATT_HD), k_g)
    v = v.reshape(bsz, n, ATT_KV_HEADS, ATT_HD)
    return q, k, v


def axial_rope(n):
    rows = n // GRID_W
    row = jnp.repeat(jnp.arange(rows, dtype=jnp.float32), GRID_W)
    col = jnp.tile(jnp.arange(GRID_W, dtype=jnp.float32), rows)
    inv = ROPE_THETA ** (-jnp.arange(0, ROPE_AXIS_DIM, 2, dtype=jnp.float32) / ROPE_AXIS_DIM)
    ang = jnp.concatenate([row[:, None] * inv, col[:, None] * inv], axis=-1)
    return jnp.cos(ang)[:, None, :], jnp.sin(ang)[:, None, :]


def apply_rope(x, cos, sin):
    xf = x.astype(jnp.float32)
    x1, x2 = xf[..., :ATT_HD // 2], xf[..., ATT_HD // 2:]
    return jnp.concatenate([x1 * cos - x2 * sin, x1 * sin + x2 * cos], axis=-1).astype(x.dtype)


def gqa_attend(q, k, v):
    bsz, nq = q.shape[0], q.shape[1]
    qg = q.reshape(bsz, nq, ATT_KV_HEADS, ATT_GROUP, ATT_HD)
    s = jnp.einsum('bqkgd,bskd->bkgqs', qg, k).astype(jnp.float32) * (ATT_HD ** -0.5)
    p = jax.nn.softmax(s, axis=-1).astype(v.dtype)
    o = jnp.einsum('bkgqs,bskd->bqkgd', p, v)
    return o.reshape(bsz, nq, ATT_Q_HEADS * ATT_HD)


def latent_attention(q, k_lat, v_lat, k_ctx, v_ctx):
    k = jnp.concatenate([k_lat, k_ctx], axis=1)
    v = jnp.concatenate([v_lat, v_ctx], axis=1)
    bsz, n = q.shape[0], q.shape[1]
    nb = n // Q_BLOCK
    qb = jnp.moveaxis(q.reshape(bsz, nb, Q_BLOCK, ATT_Q_HEADS, ATT_HD), 1, 0)
    o = lax.map(lambda q_blk: gqa_attend(q_blk, k, v), qb)
    return jnp.moveaxis(o, 0, 1).reshape(bsz, n, ATT_Q_HEADS * ATT_HD)


def rwkv_features(p, mu, w0, w_b, a0, a_b, g_b, k_k, k_a):
    bsz, n, _ = p.shape
    p = p + mu * (0.5 * (shift_prev(p) + shift_next(p)) - p)
    splits = [RW, 2 * RW, 3 * RW, 3 * RW + N_DIR * DECAY_LORA, 3 * RW + N_DIR * (DECAY_LORA + AAA_LORA)]
    r, k, v, wl, al, gl = jnp.split(p, splits, axis=-1)
    wl = wl.reshape(bsz, n, N_DIR, DECAY_LORA)
    al = al.reshape(bsz, n, N_DIR, AAA_LORA)
    w_log = -jax.nn.softplus(-(w0 + jnp.einsum('bndr,drc->bndc', jnp.tanh(wl), w_b))) - 0.5
    decay = jnp.exp(-jnp.exp(w_log.astype(jnp.float32)))
    a = jax.nn.sigmoid(a0 + jnp.einsum('bndr,drc->bndc', al, a_b))
    kk = (k * k_k).reshape(bsz, n, RW_HEADS, RW_HD).astype(jnp.float32)
    kk = kk * lax.rsqrt(jnp.maximum(jnp.sum(kk * kk, axis=-1, keepdims=True), 1e-24))
    k_dir = k[:, :, None] * (1 + (a - 1) * k_a)
    b_dir = kk[:, :, None] * a.reshape(bsz, n, N_DIR, RW_HEADS, RW_HD)
    g = jax.nn.sigmoid(gl) @ g_b
    heads = lambda t: t.reshape(t.shape[:-1] + (RW_HEADS, RW_HD))
    return heads(r), heads(decay), heads(k_dir), heads(v), kk, b_dir, g


def wkv_scan(r, decay, k, v, a_vec, b_vec, s0, reverse):
    def step(s, inp):
        r_t, w_t, k_t, v_t, a_t, b_t = inp
        sa = jnp.einsum('bhvk,bhk->bhv', s, a_t)
        s = s * w_t[:, :, None, :] + sa[..., None] * b_t[:, :, None, :] + v_t[..., None] * k_t[:, :, None, :]
        return s, jnp.einsum('bhvk,bhk->bhv', s, r_t)
    xs = tuple(jnp.moveaxis(t.astype(jnp.float32), 1, 0) for t in (r, decay, k, v, a_vec, b_vec))
    s_final, ys = lax.scan(step, s0, xs, reverse=reverse)
    return s_final, jnp.moveaxis(ys, 0, 1)


def rwkv_out(y, r, k_dir, v, g, r_k, ln_w, ln_b):
    bsz, n = y.shape[0], y.shape[1]
    mean = jnp.mean(y, axis=-1, keepdims=True)
    var = jnp.mean(jnp.square(y - mean), axis=-1, keepdims=True)
    yn = (y - mean) * lax.rsqrt(var + GN_EPS)
    k_mean = 0.5 * (k_dir[:, :, 0] + k_dir[:, :, 1])
    bonus = jnp.sum(r * k_mean * r_k, axis=-1, keepdims=True) * v
    out = yn.reshape(bsz, n, RW) * ln_w + ln_b + bonus.reshape(bsz, n, RW)
    return (out * g).astype(g.dtype)


def rwkv_mixer(p_lat, p_ctx, mu, w0, w_b, a0, a_b, g_b, k_k, k_a, r_k, ln_w, ln_b, with_ctx_out):
    r_l, dec_l, k_l, v_l, kk_l, b_l, g_l = rwkv_features(p_lat, mu, w0, w_b, a0, a_b, g_b, k_k, k_a)
    r_c, dec_c, k_c, v_c, kk_c, b_c, g_c = rwkv_features(p_ctx, mu, w0, w_b, a0, a_b, g_b, k_k, k_a)
    s0 = jnp.zeros((p_lat.shape[0], RW_HEADS, RW_HD, RW_HD), jnp.float32)
    ys_l, ys_c = [], []
    for d in range(N_DIR):
        rev = d == 1
        s_c, y_c = wkv_scan(r_c, dec_c[:, :, d], k_c[:, :, d], v_c, -kk_c, b_c[:, :, d], s0, rev)
        _, y_l = wkv_scan(r_l, dec_l[:, :, d], k_l[:, :, d], v_l, -kk_l, b_l[:, :, d], s_c, rev)
        ys_l.append(y_l)
        ys_c.append(y_c)
    out_l = rwkv_out(ys_l[0] + ys_l[1], r_l, k_l, v_l, g_l, r_k, ln_w, ln_b)
    if not with_ctx_out:
        return out_l, None
    out_c = rwkv_out(ys_c[0] + ys_c[1], r_c, k_c, v_c, g_c, r_k, ln_w, ln_b)
    return out_l, out_c


def token_mixers(p_lat, p_ctx, conv_w, q_g, k_g, rw_mu, rw_w0, rw_w_b, rw_a0, rw_a_b, rw_g_b,
                 rw_k_k, rw_k_a, rw_r_k, rw_ln_w, rw_ln_b, with_ctx_out):
    split = [CONV_COLS, CONV_COLS + ATT_COLS]
    cv_l, at_l, rw_l = jnp.split(p_lat, split, axis=-1)
    cv_c, at_c, rw_c = jnp.split(p_ctx, split, axis=-1)
    conv_l = short_conv_mixer(cv_l, conv_w)
    q_l, k_l, v_l = attn_qkv(at_l, q_g, k_g)
    q_c, k_c, v_c = attn_qkv(at_c, q_g, k_g)
    cos, sin = axial_rope(p_lat.shape[1])
    att_l = latent_attention(apply_rope(q_l, cos, sin), apply_rope(k_l, cos, sin), v_l, k_c, v_c)
    rwkv_l, rwkv_c = rwkv_mixer(rw_l, rw_c, rw_mu, rw_w0, rw_w_b, rw_a0, rw_a_b, rw_g_b, rw_k_k, rw_k_a,
                                rw_r_k, rw_ln_w, rw_ln_b, with_ctx_out)
    mix_l = jnp.concatenate([conv_l, att_l, rwkv_l], axis=-1)
    if not with_ctx_out:
        return mix_l, None
    mix_c = jnp.concatenate([short_conv_mixer(cv_c, conv_w), gqa_attend(q_c, k_c, v_c), rwkv_c], axis=-1)
    return mix_l, mix_c


def expert_choice_ffn(h, router_w, w_gate, w_up, w_down):
    bsz, n, _ = h.shape
    cap = CAPACITY_FACTOR * n // N_EXPERTS
    aff = jax.nn.softmax((h @ router_w).astype(jnp.float32), axis=-1)
    gate, idx = lax.top_k(jnp.swapaxes(aff, 1, 2), cap)
    bidx = jnp.arange(bsz)[:, None, None]
    xs = h[bidx, idx]
    hid = jax.nn.silu(jnp.einsum('becd,edf->becf', xs, w_gate)) * jnp.einsum('becd,edf->becf', xs, w_up)
    y = jnp.einsum('becf,efd->becd', hid, w_down) * gate[..., None].astype(h.dtype)
    return jnp.zeros_like(h).at[bidx, idx].add(y)


def setup_inputs(seed: int = 0) -> dict:
    key = jax.random.key(seed)
    ks = jax.random.split(key, 28)
    f32 = jnp.float32
    nrm = lambda k, shape, s: jax.random.normal(k, shape, f32) * s
    uni = lambda k, shape: jax.random.uniform(k, shape, f32)
    L = DEPTH
    return {
        'x': nrm(ks[0], (BATCH, SEQ, D_MODEL), 1.0),
        'c': nrm(ks[1], (BATCH, D_MODEL), 1.0),
        'ctx': nrm(ks[2], (BATCH, CTX_LEN, D_MODEL), 1.0),
        'c_ctx': nrm(ks[3], (D_MODEL,), 1.0),
        'ada_w': nrm(ks[4], (L, D_MODEL, N_MOD * D_MODEL), 0.5 * D_MODEL ** -0.5),
        'ada_b': nrm(ks[5], (L, N_MOD * D_MODEL), 0.02),
        'norm1_g': 1.0 + nrm(ks[6], (L, D_MODEL), 0.02),
        'norm2_g': 1.0 + nrm(ks[7], (L, D_MODEL), 0.02),
        'w_in': nrm(ks[8], (L, D_MODEL, PROJ_COLS), D_MODEL ** -0.5),
        'w_out': nrm(ks[9], (L, D_MIX, D_MODEL), D_MIX ** -0.5),
        'conv_w': nrm(ks[10], (L, 3, CONV_W), 3 ** -0.5),
        'q_norm_g': 1.0 + nrm(ks[11], (L, ATT_HD), 0.02),
        'k_norm_g': 1.0 + nrm(ks[12], (L, ATT_HD), 0.02),
        'rw_mu': uni(ks[13], (L, RW_COLS)),
        'rw_w0': -6.0 + 5.0 * uni(ks[14], (L, N_DIR, RW)),
        'rw_w_b': nrm(ks[15], (L, N_DIR, DECAY_LORA, RW), 0.1),
        'rw_a0': nrm(ks[16], (L, N_DIR, RW), 0.1),
        'rw_a_b': nrm(ks[17], (L, N_DIR, AAA_LORA, RW), 0.1),
        'rw_g_b': nrm(ks[18], (L, GATE_LORA, RW), GATE_LORA ** -0.5),
        'rw_k_k': 0.85 + nrm(ks[19], (L, RW), 0.05),
        'rw_k_a': 1.0 + nrm(ks[20], (L, RW), 0.05),
        'rw_r_k': nrm(ks[21], (L, RW_HEADS, RW_HD), 0.1),
        'rw_ln_w': 1.0 + nrm(ks[22], (L, RW), 0.02),
        'rw_ln_b': nrm(ks[23], (L, RW), 0.02),
        'router_w': nrm(ks[24], (L, D_MODEL, N_EXPERTS), D_MODEL ** -0.5),
        'exp_w_gate': nrm(ks[25], (L, N_EXPERTS, D_MODEL, D_EXPERT), D_MODEL ** -0.5),
        'exp_w_up': nrm(ks[26], (L, N_EXPERTS, D_MODEL, D_EXPERT), D_MODEL ** -0.5),
        'exp_w_down': nrm(ks[27], (L, N_EXPERTS, D_EXPERT, D_MODEL), D_EXPERT ** -0.5),
    }


def reference(x, c, ctx, c_ctx, ada_w, ada_b, norm1_g, norm2_g, w_in, w_out, conv_w, q_norm_g, k_norm_g,
              rw_mu, rw_w0, rw_w_b, rw_a0, rw_a_b, rw_g_b, rw_k_k, rw_k_a, rw_r_k, rw_ln_w, rw_ln_b,
              router_w, exp_w_gate, exp_w_up, exp_w_down):
    xc = ctx
    for l in range(DEPTH):
        update_ctx = l < DEPTH - 1
        sh1, sc1, gt1, sh2, sc2, gt2 = adaln_params(c, ada_w[l], ada_b[l])
        csh1, csc1, cgt1, csh2, csc2, cgt2 = adaln_params(c_ctx[None], ada_w[l], ada_b[l])
        p_l = modulate(x, norm1_g[l], sh1, sc1) @ w_in[l]
        p_c = modulate(xc, norm1_g[l], csh1, csc1) @ w_in[l]
        mix_l, mix_c = token_mixers(p_l, p_c, conv_w[l], q_norm_g[l], k_norm_g[l], rw_mu[l], rw_w0[l],
                                    rw_w_b[l], rw_a0[l], rw_a_b[l], rw_g_b[l], rw_k_k[l], rw_k_a[l],
                                    rw_r_k[l], rw_ln_w[l], rw_ln_b[l], update_ctx)
        x = x + gt1 * (mix_l @ w_out[l])
        x = x + gt2 * expert_choice_ffn(modulate(x, norm2_g[l], sh2, sc2), router_w[l],
                                        exp_w_gate[l], exp_w_up[l], exp_w_down[l])
        if update_ctx:
            xc = xc + cgt1 * (mix_c @ w_out[l])
            xc = xc + cgt2 * expert_choice_ffn(modulate(xc, norm2_g[l], csh2, csc2), router_w[l],
                                               exp_w_gate[l], exp_w_up[l], exp_w_down[l])
    return x
```

```python
import functools

import jax
import jax.numpy as jnp
from jax import lax
from jax.experimental import pallas as pl
from jax.experimental.pallas import tpu as pltpu

F32 = jnp.float32
BF16 = jnp.bfloat16
I32 = jnp.int32
HIGHEST = lax.Precision.HIGHEST

HEAD = 64
N_Q_HEADS = 8
N_KV_HEADS = 2
CONV_W = 256
RW = 256
RW_HEADS = RW // HEAD
DECAY_LORA = 64
AAA_LORA = 64
GATE_LORA = 160
N_MOD = 6
N_EXPERTS = 16
CAPACITY_FACTOR = 2
GRID_W = 64
ROPE_THETA = 10000.0
NORM_EPS = 1e-6
GN_EPS = 64e-5

CONV_COLS = 3 * CONV_W
Q_COLS = N_Q_HEADS * HEAD
KV_COLS = N_KV_HEADS * HEAD
ATT_COLS = Q_COLS + 2 * KV_COLS
RW_COLS = 3 * RW + 2 * (DECAY_LORA + AAA_LORA) + GATE_LORA
LANES = 128
SUBLANES = 8
RW_COLS_PAD = -(-RW_COLS // LANES) * LANES
GATE_LORA_PAD = RW_COLS_PAD - (3 * RW + 2 * (DECAY_LORA + AAA_LORA))
PROJ_PAD = CONV_COLS + ATT_COLS + RW_COLS_PAD

WKV_CHUNK = 64
VMEM_LIMIT = 56 * 1024 * 1024

Q_HEAD_ORDER = (0, 4, 1, 5, 2, 6, 3, 7)


def _cparams(*sem):
    return pltpu.CompilerParams(dimension_semantics=sem, vmem_limit_bytes=VMEM_LIMIT)


def _dot(a, b, precision=None):
    return jnp.dot(a, b, preferred_element_type=F32, precision=precision)


def _dot_nt(a, b, precision=None):
    return lax.dot_general(a, b, (((1,), (1,)), ((), ())), preferred_element_type=F32, precision=precision)


def _dot_tn(a, b, precision=None):
    return lax.dot_general(a, b, (((0,), (0,)), ((), ())), preferred_element_type=F32, precision=precision)


def _sigmoid(x):
    return 1.0 / (1.0 + jnp.exp(-x))


def _group_sum(t, ones_bd):
    hi = t.astype(BF16)
    lo = (t - hi.astype(F32)).astype(BF16)
    return _dot(hi, ones_bd) + _dot(lo, ones_bd)


def _block_ones(width):
    g = jnp.arange(width) // HEAD
    return (g[:, None] == g[None, :]).astype(BF16)


def _adaln_kernel(c_ref, w_ref, b_ref, o_ref):
    c = c_ref[...]
    s = c * _sigmoid(c)
    o_ref[0] = _dot(s, w_ref[0], HIGHEST) + b_ref[0]


def _adaln(cvecs, ada_w, ada_b):
    depth, d, n = ada_w.shape
    r = cvecs.shape[0]
    tn = 1024
    return pl.pallas_call(
        _adaln_kernel,
        grid=(depth, n // tn),
        in_specs=[
            pl.BlockSpec((r, d), lambda l, j: (0, 0)),
            pl.BlockSpec((1, d, tn), lambda l, j: (l, 0, j)),
            pl.BlockSpec((1, 1, tn), lambda l, j: (l, 0, j)),
        ],
        out_specs=pl.BlockSpec((1, r, tn), lambda l, j: (l, 0, j)),
        out_shape=jax.ShapeDtypeStruct((depth, r, n), F32),
        compiler_params=_cparams("parallel", "parallel"),
        name="adaln",
    )(cvecs, ada_w, ada_b.reshape(depth, 1, n))


def _modulated_norm(x, g, shift, scale):
    ms = jnp.mean(x * x, axis=-1, keepdims=True)
    return (x * lax.rsqrt(ms + NORM_EPS) * g) * (1.0 + scale) + shift


def _head_norm(t, ones_bd, gain):
    ss = _group_sum(t * t, ones_bd)
    return t * lax.rsqrt(ss * (1.0 / HEAD) + NORM_EPS) * gain


def _rope(t, cos, sin_signed):
    width = t.shape[1]
    lane = lax.broadcasted_iota(I32, t.shape, 1)
    first_half = (lane & (HEAD - 1)) < HEAD // 2
    swapped = jnp.where(first_half, pltpu.roll(t, width - HEAD // 2, 1), pltpu.roll(t, HEAD // 2, 1))
    reps = width // LANES
    if reps > 1:
        cos = jnp.concatenate([cos] * reps, axis=1)
        sin_signed = jnp.concatenate([sin_signed] * reps, axis=1)
    return t * cos + swapped * sin_signed


def _inproj_kernel(x_ref, sh_ref, sc_ref, g_ref, w_ref, bd_ref, qg_ref, kg_ref, cos_ref, sin_ref,
                   pc_ref, q_ref, k_ref, v_ref, rw_ref, *, rope):
    h = _modulated_norm(x_ref[0], g_ref[...], sh_ref[0], sc_ref[0])
    p = _dot(h.astype(BF16), w_ref[...])
    pc_ref[0] = p[:, :CONV_COLS]
    q = p[:, CONV_COLS:CONV_COLS + Q_COLS]
    k = p[:, CONV_COLS + Q_COLS:CONV_COLS + Q_COLS + KV_COLS]
    v = p[:, CONV_COLS + Q_COLS + KV_COLS:CONV_COLS + ATT_COLS]
    q = _head_norm(q, bd_ref[...], qg_ref[...])
    k = _head_norm(k, bd_ref[:KV_COLS, :KV_COLS], kg_ref[...])
    if rope:
        q = _rope(q, cos_ref[...], sin_ref[...])
        k = _rope(k, cos_ref[...], sin_ref[...])
    q_ref[0] = (q * (HEAD ** -0.5)).astype(BF16)
    k_ref[0] = k.astype(BF16)
    v_ref[0] = v.astype(BF16)
    rw_ref[0] = p[:, CONV_COLS + ATT_COLS:]


def _inproj(x, shift, scale, g, w_in, ones_q, qg, kg, cos, sin, *, rope, tm):
    bsz, n, d = x.shape
    mod_map = (lambda b, i: (b, 0, 0)) if shift.shape[0] == bsz else (lambda b, i: (0, 0, 0))
    const2 = lambda b, i: (0, 0)
    tile = lambda w: pl.BlockSpec((1, tm, w), lambda b, i: (b, i, 0))
    return pl.pallas_call(
        functools.partial(_inproj_kernel, rope=rope),
        grid=(bsz, n // tm),
        in_specs=[
            tile(d),
            pl.BlockSpec((1, 1, d), mod_map),
            pl.BlockSpec((1, 1, d), mod_map),
            pl.BlockSpec((1, d), const2),
            pl.BlockSpec((d, PROJ_PAD), const2),
            pl.BlockSpec((Q_COLS, Q_COLS), const2),
            pl.BlockSpec((1, Q_COLS), const2),
            pl.BlockSpec((1, KV_COLS), const2),
            pl.BlockSpec((tm, LANES), lambda b, i: (i, 0)),
            pl.BlockSpec((tm, LANES), lambda b, i: (i, 0)),
        ],
        out_specs=[tile(CONV_COLS), tile(Q_COLS), tile(KV_COLS), tile(KV_COLS), tile(RW_COLS_PAD)],
        out_shape=[
            jax.ShapeDtypeStruct((bsz, n, CONV_COLS), F32),
            jax.ShapeDtypeStruct((bsz, n, Q_COLS), BF16),
            jax.ShapeDtypeStruct((bsz, n, KV_COLS), BF16),
            jax.ShapeDtypeStruct((bsz, n, KV_COLS), BF16),
            jax.ShapeDtypeStruct((bsz, n, RW_COLS_PAD), F32),
        ],
        compiler_params=_cparams("parallel", "parallel"),
        name="inproj",
    )(x, shift, scale, g, w_in, ones_q, qg, kg, cos, sin)


def _shifted(x, prev_row, next_row):
    rows = lax.broadcasted_iota(I32, x.shape, 0)
    tm = x.shape[0]
    before = jnp.where(rows == 0, prev_row, pltpu.roll(x, 1, 0))
    after = jnp.where(rows == tm - 1, next_row, pltpu.roll(x, tm - 1, 0))
    return before, after


def _features_kernel(pc_ref, pc_prev_ref, pc_next_ref, rw_ref, rw_prev_ref, rw_next_ref,
                     conv_w_ref, mu_ref, w0_ref, wb_ref, a0_ref, ab_ref, gb_ref, kk_ref, ka_ref, bd_ref,
                     conv_ref, r_ref, v_ref, kkn_ref, g_ref, lw_ref, kd_ref, bdir_ref):
    i = pl.program_id(1)
    has_prev = (i > 0).astype(F32)
    has_next = (i < pl.num_programs(1) - 1).astype(F32)

    pc = pc_ref[0]
    pcp = pc_prev_ref[0][SUBLANES - 1:SUBLANES, :] * has_prev
    pcn = pc_next_ref[0][0:1, :] * has_next
    z = pc[:, CONV_W:2 * CONV_W] * pc[:, 2 * CONV_W:]
    zp = pcp[:, CONV_W:2 * CONV_W] * pcp[:, 2 * CONV_W:]
    zn = pcn[:, CONV_W:2 * CONV_W] * pcn[:, 2 * CONV_W:]
    zb, za = _shifted(z, zp, zn)
    cw = conv_w_ref[...]
    conv_ref[0] = pc[:, :CONV_W] * (cw[0:1] * zb + cw[1:2] * z + cw[2:3] * za)

    p = rw_ref[0]
    pb, pa = _shifted(p, rw_prev_ref[0][SUBLANES - 1:SUBLANES, :] * has_prev, rw_next_ref[0][0:1, :] * has_next)
    p = p + mu_ref[...] * (0.5 * (pb + pa) - p)
    r = p[:, :RW]
    k = p[:, RW:2 * RW]
    v = p[:, 2 * RW:3 * RW]
    o = 3 * RW
    wl = p[:, o:o + 2 * DECAY_LORA]
    al = p[:, o + 2 * DECAY_LORA:o + 2 * (DECAY_LORA + AAA_LORA)]
    gl = p[:, o + 2 * (DECAY_LORA + AAA_LORA):]
    wz = -(w0_ref[...] + _dot(jnp.tanh(wl), wb_ref[...], HIGHEST))
    softplus = jnp.maximum(wz, 0.0) + jnp.log(1.0 + jnp.exp(-jnp.abs(wz)))
    lw_ref[0] = -jnp.exp(-softplus - 0.5)
    a = _sigmoid(a0_ref[...] + _dot(al, ab_ref[...], HIGHEST))
    kk = k * kk_ref[...]
    kk = kk * lax.rsqrt(jnp.maximum(_group_sum(kk * kk, bd_ref[...]), 1e-24))
    k2 = jnp.concatenate([k, k], axis=1)
    kk2 = jnp.concatenate([kk, kk], axis=1)
    ka2 = jnp.concatenate([ka_ref[...], ka_ref[...]], axis=1)
    kd_ref[0] = k2 * (1.0 + (a - 1.0) * ka2)
    bdir_ref[0] = kk2 * a
    r_ref[0] = r
    v_ref[0] = v
    kkn_ref[0] = kk
    g_ref[0] = _dot(_sigmoid(gl), gb_ref[...], HIGHEST)


def _features(pc, prw, conv_w, mu, w0, wb, a0, ab, gb, kk, ka, ones_rw, *, tm):
    bsz, n, _ = pc.shape
    nt = n // tm
    r8 = tm // SUBLANES
    tile = lambda w: pl.BlockSpec((1, tm, w), lambda b, i: (b, i, 0))
    prev = lambda w: pl.BlockSpec((1, SUBLANES, w), lambda b, i: (b, jnp.maximum(i * r8 - 1, 0), 0))
    nxt = lambda w: pl.BlockSpec((1, SUBLANES, w), lambda b, i: (b, jnp.minimum((i + 1) * r8, nt * r8 - 1), 0))
    full = lambda a: pl.BlockSpec(a.shape, lambda b, i: (0,) * a.ndim)
    params = (conv_w, mu, w0, wb, a0, ab, gb, kk, ka, ones_rw)
    widths = (CONV_W, RW, RW, RW, RW, 2 * RW, 2 * RW, 2 * RW)
    return pl.pallas_call(
        _features_kernel,
        grid=(bsz, nt),
        in_specs=[tile(CONV_COLS), prev(CONV_COLS), nxt(CONV_COLS),
                  tile(RW_COLS_PAD), prev(RW_COLS_PAD), nxt(RW_COLS_PAD)] + [full(a) for a in params],
        out_specs=[tile(w) for w in widths],
        out_shape=[jax.ShapeDtypeStruct((bsz, n, w), F32) for w in widths],
        compiler_params=_cparams("parallel", "parallel"),
        name="features",
    )(pc, pc, pc, prw, prw, prw, *params)


def _attend_kernel(q_ref, k_ref, v_ref, o_ref):
    k = k_ref[0]
    v = v_ref[0]
    lane = lax.broadcasted_iota(I32, (q_ref.shape[1], LANES), 1)
    low = lane < HEAD
    for c in range(Q_COLS // LANES):
        qc = q_ref[0, :, c * LANES:(c + 1) * LANES]
        outs = []
        for use_low in (True, False):
            qm = jnp.where(low == use_low, qc, jnp.zeros_like(qc))
            s = _dot_nt(qm, k)
            p = jnp.exp(s - jnp.max(s, axis=-1, keepdims=True))
            denom = jnp.sum(p, axis=-1, keepdims=True)
            outs.append(_dot(p.astype(BF16), v) / denom)
        o_ref[0, :, c * LANES:(c + 1) * LANES] = jnp.where(low, outs[0], outs[1]).astype(BF16)


def _attend(q, k, v, *, tq):
    bsz, n, _ = q.shape
    nk = k.shape[1]
    return pl.pallas_call(
        _attend_kernel,
        grid=(bsz, n // tq),
        in_specs=[
            pl.BlockSpec((1, tq, Q_COLS), lambda b, i: (b, i, 0)),
            pl.BlockSpec((1, nk, KV_COLS), lambda b, i: (b, 0, 0)),
            pl.BlockSpec((1, nk, KV_COLS), lambda b, i: (b, 0, 0)),
        ],
        out_specs=pl.BlockSpec((1, tq, Q_COLS), lambda b, i: (b, i, 0)),
        out_shape=jax.ShapeDtypeStruct((bsz, n, Q_COLS), BF16),
        compiler_params=_cparams("parallel", "parallel"),
        name="attend",
    )(q, k, v)


def _head_blocks(x):
    head = lax.broadcasted_iota(I32, x.shape, 1) >> 6
    return jnp.concatenate([jnp.where(head == h, x, 0.0) for h in range(RW_HEADS)], axis=0)


def _wkv_chunk(r, v, kk, lw, kd, bd, s, *, reverse, prec):
    c = WKV_CHUNK
    ri = lax.broadcasted_iota(I32, (c, c), 0)
    ci = lax.broadcasted_iota(I32, (c, c), 1)
    tri = ((ci >= ri) if reverse else (ci <= ri)).astype(F32)
    cum = _dot(tri, lw, HIGHEST)
    total = jnp.sum(lw, axis=0, keepdims=True)
    a_t = -kk * jnp.exp(cum - lw)
    r_t = r * jnp.exp(cum)
    inv = jnp.exp(-cum)
    b_t = bd * inv
    k_t = kd * inv
    rem = jnp.exp(total - cum)
    b_end = bd * rem
    k_end = kd * rem

    row = lax.broadcasted_iota(I32, (c, RW_HEADS * c), 0)
    col = lax.broadcasted_iota(I32, (c, RW_HEADS * c), 1) & (c - 1)
    strict = (col > row) if reverse else (col < row)
    incl = (col >= row) if reverse else (col <= row)
    eye = (col == row).astype(F32)

    ar = jnp.concatenate([a_t, r_t], axis=0)
    sb = _dot_nt(ar, _head_blocks(b_t), prec)
    sk = _dot_nt(ar, _head_blocks(k_t), prec)
    a_ab = jnp.where(strict, sb[:c], 0.0)
    a_rb = jnp.where(incl, sb[c:], 0.0)
    a_ak = jnp.where(strict, sk[:c], 0.0)
    a_rk = jnp.where(incl, sk[c:], 0.0)

    t_inv = eye + a_ab
    m = a_ab
    steps = (c - 1).bit_length() - 1
    for _ in range(steps):
        m = _dot(m, _head_blocks(m), prec)
        t_inv = t_inv + _dot(t_inv, _head_blocks(m), prec)

    av = _dot(jnp.concatenate([a_ak, a_rk], axis=0), _head_blocks(v), prec)
    w1 = _dot(t_inv, _head_blocks(a_t), prec)
    w2 = _dot(t_inv, _head_blocks(av[:c]), prec)
    g = _dot_nt(jnp.concatenate([w1, r_t], axis=0), s, prec)
    u = g[:c] + w2
    y = g[c:] + _dot(a_rb, _head_blocks(u), prec) + av[c:]
    upd = _dot_tn(jnp.concatenate([u, v], axis=0), jnp.concatenate([b_end, k_end], axis=0), prec)
    hr = lax.broadcasted_iota(I32, s.shape, 0) >> 6
    hc = lax.broadcasted_iota(I32, s.shape, 1) >> 6
    s_new = s * jnp.exp(total) + jnp.where(hr == hc, upd, 0.0)
    return y, s_new


def _wkv_kernel(r_ref, v_ref, kk_ref, lw_ref, kd_ref, bd_ref, s0_ref, y_ref, s_out_ref, s_scr,
                *, reverse, chunks, prec):
    @pl.when(pl.program_id(1) == 0)
    def _():
        s_scr[...] = s0_ref[0]

    c = WKV_CHUNK
    order = range(chunks - 1, -1, -1) if reverse else range(chunks)
    s = s_scr[...]
    for j in order:
        sl = slice(j * c, (j + 1) * c)
        y, s = _wkv_chunk(r_ref[0, sl], v_ref[0, sl], kk_ref[0, sl], lw_ref[0, sl], kd_ref[0, sl],
                          bd_ref[0, sl], s, reverse=reverse, prec=prec)
        y_ref[0, sl] = y
    s_scr[...] = s
    s_out_ref[0] = s


def _wkv(r, v, kk, lw, kd, bd, s0, *, direction, prec=HIGHEST):
    bsz, n, _ = r.shape
    chunks = min(4, n // WKV_CHUNK)
    ts = chunks * WKV_CHUNK
    steps = n // ts
    reverse = direction == 1
    pos = (lambda i: steps - 1 - i) if reverse else (lambda i: i)
    shared = pl.BlockSpec((1, ts, RW), lambda b, i: (b, pos(i), 0))
    per_dir = pl.BlockSpec((1, ts, RW), lambda b, i: (b, pos(i), direction))
    state = pl.BlockSpec((1, RW, RW), lambda b, i: (b, 0, 0))
    return pl.pallas_call(
        functools.partial(_wkv_kernel, reverse=reverse, chunks=chunks, prec=prec),
        grid=(bsz, steps),
        in_specs=[shared, shared, shared, per_dir, per_dir, per_dir, state],
        out_specs=[shared, state],
        out_shape=[jax.ShapeDtypeStruct((bsz, n, RW), F32), jax.ShapeDtypeStruct((bsz, RW, RW), F32)],
        scratch_shapes=[pltpu.VMEM((RW, RW), F32)],
        compiler_params=_cparams("parallel", "arbitrary"),
        name="wkv_bwd" if reverse else "wkv_fwd",
    )(r, v, kk, lw, kd, bd, s0)


def _outproj_kernel(x_ref, conv_ref, att_ref, yf_ref, yb_ref, r_ref, v_ref, g_ref, kd_ref,
                    gt1_ref, sh2_ref, sc2_ref, g2_ref, w_ref, rk_ref, lnw_ref, lnb_ref, bd_ref, rwt_ref,
                    x1_ref, h2_ref, aff_ref):
    ones_bd = bd_ref[...]
    y = yf_ref[0] + yb_ref[0]
    mean = _group_sum(y, ones_bd) * (1.0 / HEAD)
    yc = y - mean
    var = _group_sum(yc * yc, ones_bd) * (1.0 / HEAD)
    yn = yc * lax.rsqrt(var + GN_EPS)
    kd = kd_ref[0]
    k_mean = 0.5 * (kd[:, :RW] + kd[:, RW:])
    bonus = _group_sum(r_ref[0] * k_mean * rk_ref[...], ones_bd) * v_ref[0]
    rwkv = (yn * lnw_ref[...] + lnb_ref[...] + bonus) * g_ref[0]

    mixed = (_dot(conv_ref[0].astype(BF16), w_ref[:CONV_W])
             + _dot(att_ref[0], w_ref[CONV_W:CONV_W + Q_COLS])
             + _dot(rwkv.astype(BF16), w_ref[CONV_W + Q_COLS:]))
    x1 = x_ref[0] + gt1_ref[0] * mixed
    x1_ref[0] = x1

    h2 = _modulated_norm(x1, g2_ref[...], sh2_ref[0], sc2_ref[0])
    h2_ref[0] = h2.astype(BF16)
    logits = _dot_nt(rwt_ref[...], h2, HIGHEST)
    e = jnp.exp(logits - jnp.max(logits, axis=0, keepdims=True))
    aff_ref[0] = e / jnp.sum(e, axis=0, keepdims=True)


def _outproj(x, conv, att, yf, yb, r, v, g, kd, gt1, sh2, sc2, g2, w_out, rk, lnw, lnb, ones_rw, router_t, *, tm):
    bsz, n, d = x.shape
    mod_map = (lambda b, i: (b, 0, 0)) if gt1.shape[0] == bsz else (lambda b, i: (0, 0, 0))
    tile = lambda w: pl.BlockSpec((1, tm, w), lambda b, i: (b, i, 0))
    mod = pl.BlockSpec((1, 1, d), mod_map)
    full = lambda a: pl.BlockSpec(a.shape, lambda b, i: (0,) * a.ndim)
    params = (g2, w_out, rk, lnw, lnb, ones_rw, router_t)
    return pl.pallas_call(
        _outproj_kernel,
        grid=(bsz, n // tm),
        in_specs=[tile(d), tile(CONV_W), tile(Q_COLS), tile(RW), tile(RW), tile(RW), tile(RW), tile(RW),
                  tile(2 * RW), mod, mod, mod] + [full(a) for a in params],
        out_specs=[tile(d), tile(d), pl.BlockSpec((1, N_EXPERTS, tm), lambda b, i: (b, 0, i))],
        out_shape=[
            jax.ShapeDtypeStruct((bsz, n, d), F32),
            jax.ShapeDtypeStruct((bsz, n, d), BF16),
            jax.ShapeDtypeStruct((bsz, N_EXPERTS, n), F32),
        ],
        compiler_params=_cparams("parallel", "parallel"),
        name="outproj",
    )(x, conv, att, yf, yb, r, v, g, kd, gt1, sh2, sc2, *params)


def _cumsum_lanes(x):
    n = x.shape[1]
    w = min(n, 256)
    ri = lax.broadcasted_iota(I32, (w, w), 0)
    ci = lax.broadcasted_iota(I32, (w, w), 1)
    tri = (ri <= ci).astype(BF16)
    carry = jnp.zeros((x.shape[0], 1), F32)
    parts = []
    for j in range(n // w):
        cs = _dot(x[:, j * w:(j + 1) * w].astype(BF16), tri) + carry
        carry = cs[:, w - 1:w]
        parts.append(cs)
    return parts[0] if len(parts) == 1 else jnp.concatenate(parts, axis=1)


def _to_token_major(x):
    n = x.shape[1]
    pad = jnp.zeros((LANES - x.shape[0], LANES), F32)
    blocks = [jnp.concatenate([x[:, j * LANES:(j + 1) * LANES], pad], axis=0).T for j in range(n // LANES)]
    return blocks[0] if len(blocks) == 1 else jnp.concatenate(blocks, axis=0)


def _route_kernel(aff_ref, pos_ref, pos_t_ref, aff_t_ref, *, cap):
    aff = aff_ref[0]
    bits = pltpu.bitcast(aff, I32)

    def search(i, thr):
        cand = thr | (1 << (30 - i))
        cnt = jnp.sum((bits >= cand).astype(F32), axis=1, keepdims=True)
        return jnp.where(cnt >= cap, cand, thr)

    thr = lax.fori_loop(0, 31, search, jnp.zeros((aff.shape[0], 1), I32))
    above = bits > thr
    tied = bits == thr
    need = cap - jnp.sum(above.astype(F32), axis=1, keepdims=True)
    tied_f = tied.astype(F32)
    chosen = jnp.where(above | (tied & (_cumsum_lanes(tied_f) <= need)), 1.0, 0.0)
    pos = jnp.where(chosen > 0.0, _cumsum_lanes(chosen) - 1.0, -1.0)
    pos_ref[0] = pos
    pos_t_ref[0] = _to_token_major(pos)
    aff_t_ref[0] = _to_token_major(aff)


def _route(aff, *, cap):
    bsz, e, n = aff.shape
    return pl.pallas_call(
        functools.partial(_route_kernel, cap=cap),
        grid=(bsz,),
        in_specs=[pl.BlockSpec((1, e, n), lambda b: (b, 0, 0))],
        out_specs=[pl.BlockSpec((1, e, n), lambda b: (b, 0, 0)),
                   pl.BlockSpec((1, n, LANES), lambda b: (b, 0, 0)),
                   pl.BlockSpec((1, n, LANES), lambda b: (b, 0, 0))],
        out_shape=[jax.ShapeDtypeStruct((bsz, e, n), F32),
                   jax.ShapeDtypeStruct((bsz, n, LANES), F32),
                   jax.ShapeDtypeStruct((bsz, n, LANES), F32)],
        compiler_params=_cparams("parallel"),
        name="route",
    )(aff)


def _gather_kernel(pos_ref, h_ref, xs_ref, *, slots):
    e = pl.program_id(1)
    pos = pos_ref[0, pl.ds(e, 1), :]
    slot = lax.broadcasted_iota(I32, (slots, pos.shape[1]), 0).astype(F32)
    onehot = jnp.where(slot == pos, 1.0, 0.0).astype(BF16)
    xs_ref[0, 0] = _dot(onehot, h_ref[0]).astype(BF16)


def _gather(pos, h2, *, slots):
    bsz, n, d = h2.shape
    e = pos.shape[1]
    return pl.pallas_call(
        functools.partial(_gather_kernel, slots=slots),
        grid=(bsz, e),
        in_specs=[pl.BlockSpec((1, e, n), lambda b, j: (b, 0, 0)),
                  pl.BlockSpec((1, n, d), lambda b, j: (b, 0, 0))],
        out_specs=pl.BlockSpec((1, 1, slots, d), lambda b, j: (b, j, 0, 0)),
        out_shape=jax.ShapeDtypeStruct((bsz, e, slots, d), BF16),
        compiler_params=_cparams("parallel", "arbitrary"),
        name="gather",
    )(pos, h2)


def _experts_kernel(xs_ref, wg_ref, wu_ref, wd_ref, y_ref, wg_s, wu_s, wd_s):
    @pl.when(pl.program_id(1) == 0)
    def _():
        wg_s[...] = wg_ref[0].astype(BF16)
        wu_s[...] = wu_ref[0].astype(BF16)
        wd_s[...] = wd_ref[0].astype(BF16)

    xs = xs_ref[0, 0]
    hg = _dot(xs, wg_s[...])
    hu = _dot(xs, wu_s[...])
    hid = (hg * _sigmoid(hg)) * hu
    y_ref[0, 0] = _dot(hid.astype(BF16), wd_s[...]).astype(BF16)


def _experts(xs, w_gate, w_up, w_down):
    bsz, e, slots, d = xs.shape
    f = w_gate.shape[-1]
    return pl.pallas_call(
        _experts_kernel,
        grid=(e, bsz),
        in_specs=[pl.BlockSpec((1, 1, slots, d), lambda j, b: (b, j, 0, 0)),
                  pl.BlockSpec((1, d, f), lambda j, b: (j, 0, 0)),
                  pl.BlockSpec((1, d, f), lambda j, b: (j, 0, 0)),
                  pl.BlockSpec((1, f, d), lambda j, b: (j, 0, 0))],
        out_specs=pl.BlockSpec((1, 1, slots, d), lambda j, b: (b, j, 0, 0)),
        out_shape=jax.ShapeDtypeStruct((bsz, e, slots, d), BF16),
        scratch_shapes=[pltpu.VMEM((d, f), BF16), pltpu.VMEM((d, f), BF16), pltpu.VMEM((f, d), BF16)],
        compiler_params=_cparams("parallel", "arbitrary"),
        name="experts",
    )(xs, w_gate, w_up, w_down)


def _combine_kernel(x_ref, gt_ref, pos_ref, gate_ref, y_ref, o_ref, *, slots):
    pos = pos_ref[0]
    gate = gate_ref[0]
    slot = lax.broadcasted_iota(I32, (pos.shape[0], slots), 1).astype(F32)
    acc = jnp.zeros(x_ref.shape[1:], F32)
    for e in range(N_EXPERTS):
        onehot = jnp.where(slot == pos[:, e:e + 1], 1.0, 0.0).astype(BF16)
        acc = acc + _dot(onehot, y_ref[0, e]) * gate[:, e:e + 1]
    o_ref[0] = x_ref[0] + gt_ref[0] * acc


def _combine(x1, gt2, pos_t, aff_t, y, *, tt):
    bsz, n, d = x1.shape
    e, slots = y.shape[1], y.shape[2]
    mod_map = (lambda b, i: (b, 0, 0)) if gt2.shape[0] == bsz else (lambda b, i: (0, 0, 0))
    tile = lambda w: pl.BlockSpec((1, tt, w), lambda b, i: (b, i, 0))
    return pl.pallas_call(
        functools.partial(_combine_kernel, slots=slots),
        grid=(bsz, n // tt),
        in_specs=[tile(d), pl.BlockSpec((1, 1, d), mod_map), tile(LANES), tile(LANES),
                  pl.BlockSpec((1, e, slots, d), lambda b, i: (b, 0, 0, 0))],
        out_specs=tile(d),
        out_shape=jax.ShapeDtypeStruct((bsz, n, d), F32),
        compiler_params=_cparams("parallel", "arbitrary"),
        name="combine",
    )(x1, gt2, pos_t, aff_t, y)


def _rope_tables(n):
    rows = n // GRID_W
    row = jnp.repeat(jnp.arange(rows, dtype=F32), GRID_W)
    col = jnp.tile(jnp.arange(GRID_W, dtype=F32), rows)
    half = HEAD // 2
    inv = ROPE_THETA ** (-jnp.arange(0, half, 2, dtype=F32) / half)
    ang = jnp.concatenate([row[:, None] * inv, col[:, None] * inv], axis=-1)
    cos, sin = jnp.cos(ang), jnp.sin(ang)
    cos = jnp.concatenate([cos, cos], axis=-1)
    sin = jnp.concatenate([-sin, sin], axis=-1)
    return jnp.tile(cos, (1, LANES // HEAD)), jnp.tile(sin, (1, LANES // HEAD))


def _block_diag2(m):
    z = jnp.zeros_like(m[0])
    return jnp.concatenate([jnp.concatenate([m[0], z], axis=1), jnp.concatenate([z, m[1]], axis=1)], axis=0)


def _permute_q_heads(w, axis):
    idx = jnp.concatenate([jnp.arange(h * HEAD, (h + 1) * HEAD) for h in Q_HEAD_ORDER])
    return jnp.take(w, idx, axis=axis)


def _layer_params(l, w_in, w_out, conv_w, q_norm_g, k_norm_g, rw_mu, rw_w0, rw_w_b, rw_a0, rw_a_b, rw_g_b,
                  rw_k_k, rw_k_a, rw_r_k, rw_ln_w, rw_ln_b, router_w, norm1_g, norm2_g):
    wi = w_in[l]
    q_cols = _permute_q_heads(wi[:, CONV_COLS:CONV_COLS + Q_COLS], 1)
    wi = jnp.concatenate([wi[:, :CONV_COLS], q_cols, wi[:, CONV_COLS + Q_COLS:],
                          jnp.zeros((wi.shape[0], RW_COLS_PAD - RW_COLS), F32)], axis=1).astype(BF16)
    wo = w_out[l]
    wo = jnp.concatenate([wo[:CONV_W], _permute_q_heads(wo[CONV_W:CONV_W + Q_COLS], 0),
                          wo[CONV_W + Q_COLS:]], axis=0).astype(BF16)
    row = lambda a: a.reshape(1, -1)
    return dict(
        w_in=wi, w_out=wo, conv_w=conv_w[l],
        qg=row(jnp.tile(q_norm_g[l], N_Q_HEADS)), kg=row(jnp.tile(k_norm_g[l], N_KV_HEADS)),
        mu=row(jnp.pad(rw_mu[l], (0, RW_COLS_PAD - RW_COLS))),
        w0=row(rw_w0[l]), wb=_block_diag2(rw_w_b[l]), a0=row(rw_a0[l]), ab=_block_diag2(rw_a_b[l]),
        gb=jnp.pad(rw_g_b[l], ((0, GATE_LORA_PAD - GATE_LORA), (0, 0))),
        kk=row(rw_k_k[l]), ka=row(rw_k_a[l]), rk=row(rw_r_k[l]), lnw=row(rw_ln_w[l]), lnb=row(rw_ln_b[l]),
        router_t=router_w[l].T, g1=row(norm1_g[l]), g2=row(norm2_g[l]),
    )


def _moe(x1, h2, aff, gt2, w_gate, w_up, w_down, *, tt):
    n = x1.shape[1]
    cap = CAPACITY_FACTOR * n // N_EXPERTS
    slots = -(-cap // LANES) * LANES
    pos, pos_t, aff_t = _route(aff, cap=cap)
    xs = _gather(pos, h2, slots=slots)
    y = _experts(xs, w_gate, w_up, w_down)
    return _combine(x1, gt2, pos_t, aff_t, y, tt=tt)


def kernel(x, c, ctx, c_ctx, ada_w, ada_b, norm1_g, norm2_g, w_in, w_out, conv_w, q_norm_g, k_norm_g, rw_mu,
           rw_w0, rw_w_b, rw_a0, rw_a_b, rw_g_b, rw_k_k, rw_k_a, rw_r_k, rw_ln_w, rw_ln_b, router_w,
           exp_w_gate, exp_w_up, exp_w_down):
    bsz, n, d = x.shape
    n_ctx = ctx.shape[1]
    depth = ada_w.shape[0]
    tm = min(512, n)
    tm_c = min(512, n_ctx)
    tq = min(256, n)
    tq_c = min(256, n_ctx)

    rows = -(-(bsz + 1) // SUBLANES) * SUBLANES
    cvecs = jnp.concatenate([c, c_ctx[None], jnp.zeros((rows - bsz - 1, d), F32)], axis=0)
    mod = _adaln(cvecs, ada_w, ada_b).reshape(depth, rows, N_MOD, d)

    cos, sin = _rope_tables(n)
    cos_c = jnp.ones((n_ctx, LANES), F32)
    sin_c = jnp.zeros((n_ctx, LANES), F32)
    ones_q = _block_ones(Q_COLS)
    ones_rw = _block_ones(RW)
    zero_state = jnp.zeros((bsz, RW, RW), F32)

    xc = ctx
    for l in range(depth):
        update_ctx = l < depth - 1
        p = _layer_params(l, w_in, w_out, conv_w, q_norm_g, k_norm_g, rw_mu, rw_w0, rw_w_b, rw_a0, rw_a_b, rw_g_b,
                          rw_k_k, rw_k_a, rw_r_k, rw_ln_w, rw_ln_b, router_w, norm1_g, norm2_g)
        lat = [mod[l, :bsz, i][:, None, :] for i in range(N_MOD)]
        cmod = [mod[l, bsz:bsz + 1, i][:, None, :] for i in range(N_MOD)]

        pc_l, q_l, k_l, v_l, rw_l = _inproj(x, lat[0], lat[1], p["g1"], p["w_in"], ones_q, p["qg"], p["kg"],
                                            cos, sin, rope=True, tm=tm)
        pc_c, q_c, k_c, v_c, rw_c = _inproj(xc, cmod[0], cmod[1], p["g1"], p["w_in"], ones_q, p["qg"], p["kg"],
                                            cos_c, sin_c, rope=False, tm=tm_c)
        feat_args = (p["conv_w"], p["mu"], p["w0"], p["wb"], p["a0"], p["ab"], p["gb"], p["kk"], p["ka"], ones_rw)
        conv_l, r_l, vv_l, kk_l, g_l, lw_l, kd_l, bd_l = _features(pc_l, rw_l, *feat_args, tm=min(256, n))
        conv_c, r_c, vv_c, kk_c, g_c, lw_c, kd_c, bd_c = _features(pc_c, rw_c, *feat_args, tm=min(256, n_ctx))

        att_l = _attend(q_l, jnp.concatenate([k_l, k_c], axis=1), jnp.concatenate([v_l, v_c], axis=1), tq=tq)

        ys_l, ys_c = [], []
        for direction in range(2):
            y_c, s_c = _wkv(r_c, vv_c, kk_c, lw_c, kd_c, bd_c, zero_state, direction=direction)
            y_l, _ = _wkv(r_l, vv_l, kk_l, lw_l, kd_l, bd_l, s_c, direction=direction)
            ys_l.append(y_l)
            ys_c.append(y_c)

        out_args = (p["g2"], p["w_out"], p["rk"], p["lnw"], p["lnb"], ones_rw, p["router_t"])
        x1, h2, aff = _outproj(x, conv_l, att_l, ys_l[0], ys_l[1], r_l, vv_l, g_l, kd_l,
                               lat[2], lat[3], lat[4], *out_args, tm=tm)
        x = _moe(x1, h2, aff, lat[5], exp_w_gate[l], exp_w_up[l], exp_w_down[l], tt=min(256, n))
        if update_ctx:
            att_c = _attend(q_c, k_c, v_c, tq=tq_c)
            xc1, hc2, aff_c = _outproj(xc, conv_c, att_c, ys_c[0], ys_c[1], r_c, vv_c, g_c, kd_c,
                                       cmod[2], cmod[3], cmod[4], *out_args, tm=tm_c)
            xc = _moe(xc1, hc2, aff_c, cmod[5], exp_w_gate[l], exp_w_up[l], exp_w_down[l], tt=min(256, n_ctx))
    return x
```

```python
import functools

import jax
import jax.numpy as jnp
from jax import lax
from jax.experimental import pallas as pl
from jax.experimental.pallas import tpu as pltpu

F32 = jnp.float32
BF16 = jnp.bfloat16
I32 = jnp.int32
HIGHEST = lax.Precision.HIGHEST

HEAD = 64
N_Q_HEADS = 8
N_KV_HEADS = 2
CONV_W = 256
RW = 256
RW_HEADS = RW // HEAD
DECAY_LORA = 64
AAA_LORA = 64
GATE_LORA = 160
N_MOD = 6
N_EXPERTS = 16
CAPACITY_FACTOR = 2
GRID_W = 64
ROPE_THETA = 10000.0
NORM_EPS = 1e-6
GN_EPS = 64e-5

CONV_COLS = 3 * CONV_W
Q_COLS = N_Q_HEADS * HEAD
KV_COLS = N_KV_HEADS * HEAD
ATT_COLS = Q_COLS + 2 * KV_COLS
RW_COLS = 3 * RW + 2 * (DECAY_LORA + AAA_LORA) + GATE_LORA
LANES = 128
SUBLANES = 8
RW_COLS_PAD = -(-RW_COLS // LANES) * LANES
GATE_LORA_PAD = RW_COLS_PAD - (3 * RW + 2 * (DECAY_LORA + AAA_LORA))
PROJ_PAD = CONV_COLS + ATT_COLS + RW_COLS_PAD

WKV_CHUNK = 64
VMEM_LIMIT = 56 * 1024 * 1024
WKV_PRECISION = dict(score="bf16", inverse="bf16", apply="bf16", state="bf16")

Q_HEAD_ORDER = (0, 4, 1, 5, 2, 6, 3, 7)


def _cparams(*sem):
    return pltpu.CompilerParams(dimension_semantics=sem, vmem_limit_bytes=VMEM_LIMIT)


def _dot(a, b, precision=None):
    return jnp.dot(a, b, preferred_element_type=F32, precision=precision)


def _dot_nt(a, b, precision=None):
    return lax.dot_general(a, b, (((1,), (1,)), ((), ())), preferred_element_type=F32, precision=precision)


def _dot_tn(a, b, precision=None):
    return lax.dot_general(a, b, (((0,), (0,)), ((), ())), preferred_element_type=F32, precision=precision)


def _split_bf16(x):
    hi = x.astype(BF16)
    return hi, (x - hi.astype(F32)).astype(BF16)


_NN = ((1,), (0,))
_NT = ((1,), (1,))
_TN = ((0,), (0,))


def _contract(a, b, dims, mode):
    dn = (dims, ((), ()))
    if mode == "f32":
        return lax.dot_general(a, b, dn, preferred_element_type=F32, precision=HIGHEST)
    if mode == "bf16":
        return lax.dot_general(a.astype(BF16), b.astype(BF16), dn, preferred_element_type=F32)
    ah, al = _split_bf16(a)
    bh, bl = _split_bf16(b)
    dot = lambda x, y: lax.dot_general(x, y, dn, preferred_element_type=F32)
    return dot(ah, bh) + (dot(ah, bl) + dot(al, bh))


def _sigmoid(x):
    return 1.0 / (1.0 + jnp.exp(-x))


def _group_sum(t, ones_bd):
    hi = t.astype(BF16)
    lo = (t - hi.astype(F32)).astype(BF16)
    return _dot(hi, ones_bd) + _dot(lo, ones_bd)


def _block_ones(width):
    g = jnp.arange(width) // HEAD
    return (g[:, None] == g[None, :]).astype(BF16)


def _adaln_kernel(c_ref, w_ref, b_ref, o_ref):
    c = c_ref[...]
    s = c * _sigmoid(c)
    o_ref[0] = _dot(s, w_ref[0], HIGHEST) + b_ref[0]


def _adaln(cvecs, ada_w, ada_b):
    depth, d, n = ada_w.shape
    r = cvecs.shape[0]
    tn = 1024
    return pl.pallas_call(
        _adaln_kernel,
        grid=(depth, n // tn),
        in_specs=[
            pl.BlockSpec((r, d), lambda l, j: (0, 0)),
            pl.BlockSpec((1, d, tn), lambda l, j: (l, 0, j)),
            pl.BlockSpec((1, 1, tn), lambda l, j: (l, 0, j)),
        ],
        out_specs=pl.BlockSpec((1, r, tn), lambda l, j: (l, 0, j)),
        out_shape=jax.ShapeDtypeStruct((depth, r, n), F32),
        compiler_params=_cparams("parallel", "parallel"),
        name="adaln",
    )(cvecs, ada_w, ada_b.reshape(depth, 1, n))


def _modulated_norm(x, g, shift, scale):
    ms = jnp.mean(x * x, axis=-1, keepdims=True)
    return (x * lax.rsqrt(ms + NORM_EPS) * g) * (1.0 + scale) + shift


def _head_norm(t, ones_bd, gain):
    ss = _group_sum(t * t, ones_bd)
    return t * lax.rsqrt(ss * (1.0 / HEAD) + NORM_EPS) * gain


def _rope(t, cos, sin_signed):
    width = t.shape[1]
    lane = lax.broadcasted_iota(I32, t.shape, 1)
    first_half = (lane & (HEAD - 1)) < HEAD // 2
    swapped = jnp.where(first_half, pltpu.roll(t, width - HEAD // 2, 1), pltpu.roll(t, HEAD // 2, 1))
    reps = width // LANES
    if reps > 1:
        cos = jnp.concatenate([cos] * reps, axis=1)
        sin_signed = jnp.concatenate([sin_signed] * reps, axis=1)
    return t * cos + swapped * sin_signed


def _inproj_kernel(x_ref, sh_ref, sc_ref, g_ref, w_ref, bd_ref, qg_ref, kg_ref, cos_ref, sin_ref,
                   pc_ref, q_ref, k_ref, v_ref, rw_ref, *, rope):
    h = _modulated_norm(x_ref[0], g_ref[...], sh_ref[0], sc_ref[0])
    p = _dot(h.astype(BF16), w_ref[...])
    pc_ref[0] = p[:, :CONV_COLS]
    q = p[:, CONV_COLS:CONV_COLS + Q_COLS]
    k = p[:, CONV_COLS + Q_COLS:CONV_COLS + Q_COLS + KV_COLS]
    v = p[:, CONV_COLS + Q_COLS + KV_COLS:CONV_COLS + ATT_COLS]
    q = _head_norm(q, bd_ref[...], qg_ref[...])
    k = _head_norm(k, bd_ref[:KV_COLS, :KV_COLS], kg_ref[...])
    if rope:
        q = _rope(q, cos_ref[...], sin_ref[...])
        k = _rope(k, cos_ref[...], sin_ref[...])
    q_ref[0] = (q * (HEAD ** -0.5)).astype(BF16)
    k_ref[0] = k.astype(BF16)
    v_ref[0] = v.astype(BF16)
    rw_ref[0] = p[:, CONV_COLS + ATT_COLS:]


def _inproj(x, shift, scale, g, w_in, ones_q, qg, kg, cos, sin, *, rope, tm):
    bsz, n, d = x.shape
    mod_map = (lambda b, i: (b, 0, 0)) if shift.shape[0] == bsz else (lambda b, i: (0, 0, 0))
    const2 = lambda b, i: (0, 0)
    tile = lambda w: pl.BlockSpec((1, tm, w), lambda b, i: (b, i, 0))
    return pl.pallas_call(
        functools.partial(_inproj_kernel, rope=rope),
        grid=(bsz, n // tm),
        in_specs=[
            tile(d),
            pl.BlockSpec((1, 1, d), mod_map),
            pl.BlockSpec((1, 1, d), mod_map),
            pl.BlockSpec((1, d), const2),
            pl.BlockSpec((d, PROJ_PAD), const2),
            pl.BlockSpec((Q_COLS, Q_COLS), const2),
            pl.BlockSpec((1, Q_COLS), const2),
            pl.BlockSpec((1, KV_COLS), const2),
            pl.BlockSpec((tm, LANES), lambda b, i: (i, 0)),
            pl.BlockSpec((tm, LANES), lambda b, i: (i, 0)),
        ],
        out_specs=[tile(CONV_COLS), tile(Q_COLS), tile(KV_COLS), tile(KV_COLS), tile(RW_COLS_PAD)],
        out_shape=[
            jax.ShapeDtypeStruct((bsz, n, CONV_COLS), F32),
            jax.ShapeDtypeStruct((bsz, n, Q_COLS), BF16),
            jax.ShapeDtypeStruct((bsz, n, KV_COLS), BF16),
            jax.ShapeDtypeStruct((bsz, n, KV_COLS), BF16),
            jax.ShapeDtypeStruct((bsz, n, RW_COLS_PAD), F32),
        ],
        compiler_params=_cparams("parallel", "parallel"),
        name="inproj",
    )(x, shift, scale, g, w_in, ones_q, qg, kg, cos, sin)


def _shifted(x, prev_row, next_row):
    rows = lax.broadcasted_iota(I32, x.shape, 0)
    tm = x.shape[0]
    before = jnp.where(rows == 0, prev_row, pltpu.roll(x, 1, 0))
    after = jnp.where(rows == tm - 1, next_row, pltpu.roll(x, tm - 1, 0))
    return before, after


def _features_kernel(pc_ref, pc_prev_ref, pc_next_ref, rw_ref, rw_prev_ref, rw_next_ref,
                     conv_w_ref, mu_ref, w0_ref, wb_ref, a0_ref, ab_ref, gb_ref, kk_ref, ka_ref, bd_ref,
                     conv_ref, r_ref, v_ref, kkn_ref, g_ref, lw_ref, kd_ref, bdir_ref):
    i = pl.program_id(1)
    has_prev = (i > 0).astype(F32)
    has_next = (i < pl.num_programs(1) - 1).astype(F32)

    pc = pc_ref[0]
    pcp = pc_prev_ref[0][SUBLANES - 1:SUBLANES, :] * has_prev
    pcn = pc_next_ref[0][0:1, :] * has_next
    z = pc[:, CONV_W:2 * CONV_W] * pc[:, 2 * CONV_W:]
    zp = pcp[:, CONV_W:2 * CONV_W] * pcp[:, 2 * CONV_W:]
    zn = pcn[:, CONV_W:2 * CONV_W] * pcn[:, 2 * CONV_W:]
    zb, za = _shifted(z, zp, zn)
    cw = conv_w_ref[...]
    conv_ref[0] = pc[:, :CONV_W] * (cw[0:1] * zb + cw[1:2] * z + cw[2:3] * za)

    p = rw_ref[0]
    pb, pa = _shifted(p, rw_prev_ref[0][SUBLANES - 1:SUBLANES, :] * has_prev, rw_next_ref[0][0:1, :] * has_next)
    p = p + mu_ref[...] * (0.5 * (pb + pa) - p)
    r = p[:, :RW]
    k = p[:, RW:2 * RW]
    v = p[:, 2 * RW:3 * RW]
    o = 3 * RW
    wl = p[:, o:o + 2 * DECAY_LORA]
    al = p[:, o + 2 * DECAY_LORA:o + 2 * (DECAY_LORA + AAA_LORA)]
    gl = p[:, o + 2 * (DECAY_LORA + AAA_LORA):]
    wz = -(w0_ref[...] + _dot(jnp.tanh(wl), wb_ref[...], HIGHEST))
    softplus = jnp.maximum(wz, 0.0) + jnp.log(1.0 + jnp.exp(-jnp.abs(wz)))
    lw_ref[0] = -jnp.exp(-softplus - 0.5)
    a = _sigmoid(a0_ref[...] + _dot(al, ab_ref[...], HIGHEST))
    kk = k * kk_ref[...]
    kk = kk * lax.rsqrt(jnp.maximum(_group_sum(kk * kk, bd_ref[...]), 1e-24))
    k2 = jnp.concatenate([k, k], axis=1)
    kk2 = jnp.concatenate([kk, kk], axis=1)
    ka2 = jnp.concatenate([ka_ref[...], ka_ref[...]], axis=1)
    kd_ref[0] = k2 * (1.0 + (a - 1.0) * ka2)
    bdir_ref[0] = kk2 * a
    r_ref[0] = r
    v_ref[0] = v
    kkn_ref[0] = kk
    g_ref[0] = _dot(_sigmoid(gl), gb_ref[...], HIGHEST)


def _features(pc, prw, conv_w, mu, w0, wb, a0, ab, gb, kk, ka, ones_rw, *, tm):
    bsz, n, _ = pc.shape
    nt = n // tm
    r8 = tm // SUBLANES
    tile = lambda w: pl.BlockSpec((1, tm, w), lambda b, i: (b, i, 0))
    prev = lambda w: pl.BlockSpec((1, SUBLANES, w), lambda b, i: (b, jnp.maximum(i * r8 - 1, 0), 0))
    nxt = lambda w: pl.BlockSpec((1, SUBLANES, w), lambda b, i: (b, jnp.minimum((i + 1) * r8, nt * r8 - 1), 0))
    full = lambda a: pl.BlockSpec(a.shape, lambda b, i: (0,) * a.ndim)
    params = (conv_w, mu, w0, wb, a0, ab, gb, kk, ka, ones_rw)
    widths = (CONV_W, RW, RW, RW, RW, 2 * RW, 2 * RW, 2 * RW)
    return pl.pallas_call(
        _features_kernel,
        grid=(bsz, nt),
        in_specs=[tile(CONV_COLS), prev(CONV_COLS), nxt(CONV_COLS),
                  tile(RW_COLS_PAD), prev(RW_COLS_PAD), nxt(RW_COLS_PAD)] + [full(a) for a in params],
        out_specs=[tile(w) for w in widths],
        out_shape=[jax.ShapeDtypeStruct((bsz, n, w), F32) for w in widths],
        compiler_params=_cparams("parallel", "parallel"),
        name="features",
    )(pc, pc, pc, prw, prw, prw, *params)


def _attend_kernel(q_ref, k_ref, v_ref, o_ref):
    k = k_ref[0]
    v = v_ref[0]
    lane = lax.broadcasted_iota(I32, (q_ref.shape[1], LANES), 1)
    low = lane < HEAD
    for c in range(Q_COLS // LANES):
        qc = q_ref[0, :, c * LANES:(c + 1) * LANES]
        outs = []
        for use_low in (True, False):
            qm = jnp.where(low == use_low, qc, jnp.zeros_like(qc))
            s = _dot_nt(qm, k)
            p = jnp.exp(s - jnp.max(s, axis=-1, keepdims=True))
            denom = jnp.sum(p, axis=-1, keepdims=True)
            outs.append(_dot(p.astype(BF16), v) / denom)
        o_ref[0, :, c * LANES:(c + 1) * LANES] = jnp.where(low, outs[0], outs[1]).astype(BF16)


def _attend(q, k, v, *, tq):
    bsz, n, _ = q.shape
    nk = k.shape[1]
    return pl.pallas_call(
        _attend_kernel,
        grid=(bsz, n // tq),
        in_specs=[
            pl.BlockSpec((1, tq, Q_COLS), lambda b, i: (b, i, 0)),
            pl.BlockSpec((1, nk, KV_COLS), lambda b, i: (b, 0, 0)),
            pl.BlockSpec((1, nk, KV_COLS), lambda b, i: (b, 0, 0)),
        ],
        out_specs=pl.BlockSpec((1, tq, Q_COLS), lambda b, i: (b, i, 0)),
        out_shape=jax.ShapeDtypeStruct((bsz, n, Q_COLS), BF16),
        compiler_params=_cparams("parallel", "parallel"),
        name="attend",
    )(q, k, v)


def _head_blocks(x, mode):
    if mode == "bf16":
        x = x.astype(BF16)
    head = lax.broadcasted_iota(I32, x.shape, 1) >> 6
    return jnp.concatenate([jnp.where(head == h, x, jnp.zeros_like(x)) for h in range(RW_HEADS)], axis=0)


def _wkv_chunks(chains, prec):
    c = WKV_CHUNK
    ps, pi, pa, pst = prec["score"], prec["inverse"], prec["apply"], prec["state"]
    ri = lax.broadcasted_iota(I32, (c, c), 0)
    ci = lax.broadcasted_iota(I32, (c, c), 1)
    row = lax.broadcasted_iota(I32, (c, RW_HEADS * c), 0)
    col = lax.broadcasted_iota(I32, (c, RW_HEADS * c), 1) & (c - 1)
    eye = (col == row).astype(F32)
    masks = {}
    for reverse in sorted({ch[7] for ch in chains}):
        masks[reverse] = (((ci >= ri) if reverse else (ci <= ri)).astype(BF16),
                          (col > row) if reverse else (col < row),
                          (col >= row) if reverse else (col <= row))

    pre = []
    for r, v, kk, lw, kd, bd, s, reverse in chains:
        tri = masks[reverse][0]
        lw_hi, lw_lo = _split_bf16(lw)
        lw_mid, lw_lo = _split_bf16(lw - lw_hi.astype(F32))
        cum = _dot(tri, lw_hi) + (_dot(tri, lw_mid) + _dot(tri, lw_lo))
        total = jnp.sum(lw, axis=0, keepdims=True)
        inv = jnp.exp(-cum)
        rem = jnp.exp(total - cum)
        pre.append(dict(a_t=-kk * jnp.exp(cum - lw), r_t=r * jnp.exp(cum), b_t=bd * inv, k_t=kd * inv,
                        b_end=bd * rem, k_end=kd * rem, decay=jnp.exp(total)))

    scores = []
    for ch, q in zip(chains, pre):
        _, strict, incl = masks[ch[7]]
        ar = jnp.concatenate([q["a_t"], q["r_t"]], axis=0)
        sb = _contract(ar, _head_blocks(q["b_t"], ps), _NT, ps)
        sk = _contract(ar, _head_blocks(q["k_t"], ps), _NT, ps)
        scores.append(dict(a_ab=jnp.where(strict, sb[:c], 0.0), a_rb=jnp.where(incl, sb[c:], 0.0),
                           a_ak=jnp.where(strict, sk[:c], 0.0), a_rk=jnp.where(incl, sk[c:], 0.0)))

    t_inv = [eye + sc["a_ab"] for sc in scores]
    m = [sc["a_ab"] for sc in scores]
    m_blocks = [_head_blocks(x, pi) for x in m]
    for _ in range((c - 1).bit_length() - 1):
        m = [_contract(x, xb, _NN, pi) for x, xb in zip(m, m_blocks)]
        m_blocks = [_head_blocks(x, pi) for x in m]
        t_inv = [t + _contract(t, xb, _NN, pi) for t, xb in zip(t_inv, m_blocks)]

    av = [_contract(jnp.concatenate([sc["a_ak"], sc["a_rk"]], axis=0), _head_blocks(ch[1], pa), _NN, pa)
          for ch, sc in zip(chains, scores)]
    w1 = [_contract(t, _head_blocks(q["a_t"], pa), _NN, pa) for t, q in zip(t_inv, pre)]
    w2 = [_contract(t, _head_blocks(x[:c], pa), _NN, pa) for t, x in zip(t_inv, av)]
    g = [_contract(jnp.concatenate([x, q["r_t"]], axis=0), ch[6], _NT, pst) for x, q, ch in zip(w1, pre, chains)]
    u = [x[:c] + y for x, y in zip(g, w2)]
    ys = [x[c:] + _contract(sc["a_rb"], _head_blocks(uu, pa), _NN, pa) + y[c:]
          for x, sc, uu, y in zip(g, scores, u, av)]
    upd = [_contract(jnp.concatenate([uu, ch[1]], axis=0), jnp.concatenate([q["b_end"], q["k_end"]], axis=0), _TN, pst)
           for uu, ch, q in zip(u, chains, pre)]
    hr = lax.broadcasted_iota(I32, (RW, RW), 0) >> 6
    hc = lax.broadcasted_iota(I32, (RW, RW), 1) >> 6
    s_new = [ch[6] * q["decay"] + jnp.where(hr == hc, x, 0.0) for ch, q, x in zip(chains, pre, upd)]
    return ys, s_new


def _wkv_kernel(rf_ref, vf_ref, kkf_ref, lwf_ref, kdf_ref, bdf_ref, rb_ref, vb_ref, kkb_ref, lwb_ref, kdb_ref,
                bdb_ref, s0_ref, yf_ref, yb_ref, s_out_ref, s_scr, *, prec):
    step = pl.program_id(0)

    @pl.when(step == 0)
    def _():
        s_scr[...] = s0_ref[...]

    bsz = rf_ref.shape[0]
    chains = []
    for b in range(bsz):
        chains.append((rf_ref[b], vf_ref[b], kkf_ref[b], lwf_ref[b], kdf_ref[b], bdf_ref[b], s_scr[0, b], False))
        chains.append((rb_ref[b], vb_ref[b], kkb_ref[b], lwb_ref[b], kdb_ref[b], bdb_ref[b], s_scr[1, b], True))
    ys, s_new = _wkv_chunks(chains, prec)
    for b in range(bsz):
        yf_ref[b] = ys[2 * b]
        yb_ref[b] = ys[2 * b + 1]
        s_scr[0, b] = s_new[2 * b]
        s_scr[1, b] = s_new[2 * b + 1]

    @pl.when(step == pl.num_programs(0) - 1)
    def _():
        s_out_ref[...] = s_scr[...]


def _wkv(r, v, kk, lw, kd, bd, s0, prec=None):
    prec = prec or WKV_PRECISION
    bsz, n, _ = r.shape
    steps = n // WKV_CHUNK
    blk = (bsz, WKV_CHUNK, RW)
    fwd = pl.BlockSpec(blk, lambda i: (0, i, 0))
    bwd = pl.BlockSpec(blk, lambda i: (0, steps - 1 - i, 0))
    bwd_dir = pl.BlockSpec(blk, lambda i: (0, steps - 1 - i, 1))
    state = pl.BlockSpec((2, bsz, RW, RW), lambda i: (0, 0, 0, 0))
    y_shape = jax.ShapeDtypeStruct((bsz, n, RW), F32)
    return pl.pallas_call(
        functools.partial(_wkv_kernel, prec=prec),
        grid=(steps,),
        in_specs=[fwd, fwd, fwd, fwd, fwd, fwd, bwd, bwd, bwd, bwd_dir, bwd_dir, bwd_dir, state],
        out_specs=[fwd, bwd, state],
        out_shape=[y_shape, y_shape, jax.ShapeDtypeStruct((2, bsz, RW, RW), F32)],
        scratch_shapes=[pltpu.VMEM((2, bsz, RW, RW), F32)],
        compiler_params=_cparams("arbitrary"),
        name="wkv",
    )(r, v, kk, lw, kd, bd, r, v, kk, lw, kd, bd, s0)


def _outproj_kernel(x_ref, conv_ref, att_ref, yf_ref, yb_ref, r_ref, v_ref, g_ref, kd_ref,
                    gt1_ref, sh2_ref, sc2_ref, g2_ref, w_ref, rk_ref, lnw_ref, lnb_ref, bd_ref, rwt_ref,
                    x1_ref, h2_ref, aff_ref):
    ones_bd = bd_ref[...]
    y = yf_ref[0] + yb_ref[0]
    mean = _group_sum(y, ones_bd) * (1.0 / HEAD)
    yc = y - mean
    var = _group_sum(yc * yc, ones_bd) * (1.0 / HEAD)
    yn = yc * lax.rsqrt(var + GN_EPS)
    kd = kd_ref[0]
    k_mean = 0.5 * (kd[:, :RW] + kd[:, RW:])
    bonus = _group_sum(r_ref[0] * k_mean * rk_ref[...], ones_bd) * v_ref[0]
    rwkv = (yn * lnw_ref[...] + lnb_ref[...] + bonus) * g_ref[0]

    mixed = (_dot(conv_ref[0].astype(BF16), w_ref[:CONV_W])
             + _dot(att_ref[0], w_ref[CONV_W:CONV_W + Q_COLS])
             + _dot(rwkv.astype(BF16), w_ref[CONV_W + Q_COLS:]))
    x1 = x_ref[0] + gt1_ref[0] * mixed
    x1_ref[0] = x1

    h2 = _modulated_norm(x1, g2_ref[...], sh2_ref[0], sc2_ref[0])
    h2_ref[0] = h2.astype(BF16)
    logits = _dot_nt(rwt_ref[...], h2, HIGHEST)
    e = jnp.exp(logits - jnp.max(logits, axis=0, keepdims=True))
    aff_ref[0] = e / jnp.sum(e, axis=0, keepdims=True)


def _outproj(x, conv, att, yf, yb, r, v, g, kd, gt1, sh2, sc2, g2, w_out, rk, lnw, lnb, ones_rw, router_t, *, tm):
    bsz, n, d = x.shape
    mod_map = (lambda b, i: (b, 0, 0)) if gt1.shape[0] == bsz else (lambda b, i: (0, 0, 0))
    tile = lambda w: pl.BlockSpec((1, tm, w), lambda b, i: (b, i, 0))
    mod = pl.BlockSpec((1, 1, d), mod_map)
    full = lambda a: pl.BlockSpec(a.shape, lambda b, i: (0,) * a.ndim)
    params = (g2, w_out, rk, lnw, lnb, ones_rw, router_t)
    return pl.pallas_call(
        _outproj_kernel,
        grid=(bsz, n // tm),
        in_specs=[tile(d), tile(CONV_W), tile(Q_COLS), tile(RW), tile(RW), tile(RW), tile(RW), tile(RW),
                  tile(2 * RW), mod, mod, mod] + [full(a) for a in params],
        out_specs=[tile(d), tile(d), pl.BlockSpec((1, N_EXPERTS, tm), lambda b, i: (b, 0, i))],
        out_shape=[
            jax.ShapeDtypeStruct((bsz, n, d), F32),
            jax.ShapeDtypeStruct((bsz, n, d), BF16),
            jax.ShapeDtypeStruct((bsz, N_EXPERTS, n), F32),
        ],
        compiler_params=_cparams("parallel", "parallel"),
        name="outproj",
    )(x, conv, att, yf, yb, r, v, g, kd, gt1, sh2, sc2, *params)


def _cumsum_lanes(x):
    n = x.shape[1]
    w = min(n, 256)
    ri = lax.broadcasted_iota(I32, (w, w), 0)
    ci = lax.broadcasted_iota(I32, (w, w), 1)
    tri = (ri <= ci).astype(BF16)
    carry = jnp.zeros((x.shape[0], 1), F32)
    parts = []
    for j in range(n // w):
        cs = _dot(x[:, j * w:(j + 1) * w].astype(BF16), tri) + carry
        carry = cs[:, w - 1:w]
        parts.append(cs)
    return parts[0] if len(parts) == 1 else jnp.concatenate(parts, axis=1)


def _to_token_major(x):
    n = x.shape[1]
    pad = jnp.zeros((LANES - x.shape[0], LANES), F32)
    blocks = [jnp.concatenate([x[:, j * LANES:(j + 1) * LANES], pad], axis=0).T for j in range(n // LANES)]
    return blocks[0] if len(blocks) == 1 else jnp.concatenate(blocks, axis=0)


def _route_kernel(aff_ref, pos_ref, pos_t_ref, aff_t_ref, *, cap):
    aff = aff_ref[0]
    bits = pltpu.bitcast(aff, I32)

    def search(i, thr):
        cand = thr | (1 << (30 - i))
        cnt = jnp.sum((bits >= cand).astype(F32), axis=1, keepdims=True)
        return jnp.where(cnt >= cap, cand, thr)

    thr = lax.fori_loop(0, 31, search, jnp.zeros((aff.shape[0], 1), I32))
    above = bits > thr
    tied = bits == thr
    need = cap - jnp.sum(above.astype(F32), axis=1, keepdims=True)
    tied_f = tied.astype(F32)
    chosen = jnp.where(above | (tied & (_cumsum_lanes(tied_f) <= need)), 1.0, 0.0)
    pos = jnp.where(chosen > 0.0, _cumsum_lanes(chosen) - 1.0, -1.0)
    pos_ref[0] = pos
    pos_t_ref[0] = _to_token_major(pos)
    aff_t_ref[0] = _to_token_major(aff)


def _route(aff, *, cap):
    bsz, e, n = aff.shape
    return pl.pallas_call(
        functools.partial(_route_kernel, cap=cap),
        grid=(bsz,),
        in_specs=[pl.BlockSpec((1, e, n), lambda b: (b, 0, 0))],
        out_specs=[pl.BlockSpec((1, e, n), lambda b: (b, 0, 0)),
                   pl.BlockSpec((1, n, LANES), lambda b: (b, 0, 0)),
                   pl.BlockSpec((1, n, LANES), lambda b: (b, 0, 0))],
        out_shape=[jax.ShapeDtypeStruct((bsz, e, n), F32),
                   jax.ShapeDtypeStruct((bsz, n, LANES), F32),
                   jax.ShapeDtypeStruct((bsz, n, LANES), F32)],
        compiler_params=_cparams("parallel"),
        name="route",
    )(aff)


def _gather_kernel(pos_ref, h_ref, xs_ref, *, slots):
    e = pl.program_id(1)
    pos = pos_ref[0, pl.ds(e, 1), :]
    slot = lax.broadcasted_iota(I32, (slots, pos.shape[1]), 0).astype(F32)
    onehot = jnp.where(slot == pos, 1.0, 0.0).astype(BF16)
    xs_ref[0, 0] = _dot(onehot, h_ref[0]).astype(BF16)


def _gather(pos, h2, *, slots):
    bsz, n, d = h2.shape
    e = pos.shape[1]
    return pl.pallas_call(
        functools.partial(_gather_kernel, slots=slots),
        grid=(bsz, e),
        in_specs=[pl.BlockSpec((1, e, n), lambda b, j: (b, 0, 0)),
                  pl.BlockSpec((1, n, d), lambda b, j: (b, 0, 0))],
        out_specs=pl.BlockSpec((1, 1, slots, d), lambda b, j: (b, j, 0, 0)),
        out_shape=jax.ShapeDtypeStruct((bsz, e, slots, d), BF16),
        compiler_params=_cparams("parallel", "arbitrary"),
        name="gather",
    )(pos, h2)


def _experts_kernel(xs_ref, wg_ref, wu_ref, wd_ref, y_ref, wg_s, wu_s, wd_s):
    @pl.when(pl.program_id(1) == 0)
    def _():
        wg_s[...] = wg_ref[0, 0].astype(BF16)
        wu_s[...] = wu_ref[0, 0].astype(BF16)
        wd_s[...] = wd_ref[0, 0].astype(BF16)

    xs = xs_ref[0, 0]
    hg = _dot(xs, wg_s[...])
    hu = _dot(xs, wu_s[...])
    hid = (hg * _sigmoid(hg)) * hu
    y_ref[0, 0] = _dot(hid.astype(BF16), wd_s[...]).astype(BF16)


def _experts(xs, w_gate, w_up, w_down, layer):
    bsz, e, slots, d = xs.shape
    f = w_gate.shape[-1]
    return pl.pallas_call(
        _experts_kernel,
        grid=(e, bsz),
        in_specs=[pl.BlockSpec((1, 1, slots, d), lambda j, b: (b, j, 0, 0)),
                  pl.BlockSpec((1, 1, d, f), lambda j, b: (layer, j, 0, 0)),
                  pl.BlockSpec((1, 1, d, f), lambda j, b: (layer, j, 0, 0)),
                  pl.BlockSpec((1, 1, f, d), lambda j, b: (layer, j, 0, 0))],
        out_specs=pl.BlockSpec((1, 1, slots, d), lambda j, b: (b, j, 0, 0)),
        out_shape=jax.ShapeDtypeStruct((bsz, e, slots, d), BF16),
        scratch_shapes=[pltpu.VMEM((d, f), BF16), pltpu.VMEM((d, f), BF16), pltpu.VMEM((f, d), BF16)],
        compiler_params=_cparams("parallel", "arbitrary"),
        name="experts",
    )(xs, w_gate, w_up, w_down)


def _combine_kernel(x_ref, gt_ref, pos_ref, gate_ref, y_ref, o_ref, *, slots):
    pos = pos_ref[0]
    gate = gate_ref[0]
    slot = lax.broadcasted_iota(I32, (pos.shape[0], slots), 1).astype(F32)
    acc = jnp.zeros(x_ref.shape[1:], F32)
    for e in range(N_EXPERTS):
        onehot = jnp.where(slot == pos[:, e:e + 1], 1.0, 0.0).astype(BF16)
        acc = acc + _dot(onehot, y_ref[0, e]) * gate[:, e:e + 1]
    o_ref[0] = x_ref[0] + gt_ref[0] * acc


def _combine(x1, gt2, pos_t, aff_t, y, *, tt):
    bsz, n, d = x1.shape
    e, slots = y.shape[1], y.shape[2]
    mod_map = (lambda b, i: (b, 0, 0)) if gt2.shape[0] == bsz else (lambda b, i: (0, 0, 0))
    tile = lambda w: pl.BlockSpec((1, tt, w), lambda b, i: (b, i, 0))
    return pl.pallas_call(
        functools.partial(_combine_kernel, slots=slots),
        grid=(bsz, n // tt),
        in_specs=[tile(d), pl.BlockSpec((1, 1, d), mod_map), tile(LANES), tile(LANES),
                  pl.BlockSpec((1, e, slots, d), lambda b, i: (b, 0, 0, 0))],
        out_specs=tile(d),
        out_shape=jax.ShapeDtypeStruct((bsz, n, d), F32),
        compiler_params=_cparams("parallel", "arbitrary"),
        name="combine",
    )(x1, gt2, pos_t, aff_t, y)


def _rope_tables(n):
    rows = n // GRID_W
    row = jnp.repeat(jnp.arange(rows, dtype=F32), GRID_W)
    col = jnp.tile(jnp.arange(GRID_W, dtype=F32), rows)
    half = HEAD // 2
    inv = ROPE_THETA ** (-jnp.arange(0, half, 2, dtype=F32) / half)
    ang = jnp.concatenate([row[:, None] * inv, col[:, None] * inv], axis=-1)
    cos, sin = jnp.cos(ang), jnp.sin(ang)
    cos = jnp.concatenate([cos, cos], axis=-1)
    sin = jnp.concatenate([-sin, sin], axis=-1)
    return jnp.tile(cos, (1, LANES // HEAD)), jnp.tile(sin, (1, LANES // HEAD))


def _block_diag2(m):
    z = jnp.zeros_like(m[0])
    return jnp.concatenate([jnp.concatenate([m[0], z], axis=1), jnp.concatenate([z, m[1]], axis=1)], axis=0)


def _permute_q_heads(w, axis):
    idx = jnp.concatenate([jnp.arange(h * HEAD, (h + 1) * HEAD) for h in Q_HEAD_ORDER])
    return jnp.take(w, idx, axis=axis)


def _layer_params(l, w_in, w_out, conv_w, q_norm_g, k_norm_g, rw_mu, rw_w0, rw_w_b, rw_a0, rw_a_b, rw_g_b,
                  rw_k_k, rw_k_a, rw_r_k, rw_ln_w, rw_ln_b, router_w, norm1_g, norm2_g):
    wi = w_in[l]
    q_cols = _permute_q_heads(wi[:, CONV_COLS:CONV_COLS + Q_COLS], 1)
    wi = jnp.concatenate([wi[:, :CONV_COLS], q_cols, wi[:, CONV_COLS + Q_COLS:],
                          jnp.zeros((wi.shape[0], RW_COLS_PAD - RW_COLS), F32)], axis=1).astype(BF16)
    wo = w_out[l]
    wo = jnp.concatenate([wo[:CONV_W], _permute_q_heads(wo[CONV_W:CONV_W + Q_COLS], 0),
                          wo[CONV_W + Q_COLS:]], axis=0).astype(BF16)
    row = lambda a: a.reshape(1, -1)
    return dict(
        w_in=wi, w_out=wo, conv_w=conv_w[l],
        qg=row(jnp.tile(q_norm_g[l], N_Q_HEADS)), kg=row(jnp.tile(k_norm_g[l], N_KV_HEADS)),
        mu=row(jnp.pad(rw_mu[l], (0, RW_COLS_PAD - RW_COLS))),
        w0=row(rw_w0[l]), wb=_block_diag2(rw_w_b[l]), a0=row(rw_a0[l]), ab=_block_diag2(rw_a_b[l]),
        gb=jnp.pad(rw_g_b[l], ((0, GATE_LORA_PAD - GATE_LORA), (0, 0))),
        kk=row(rw_k_k[l]), ka=row(rw_k_a[l]), rk=row(rw_r_k[l]), lnw=row(rw_ln_w[l]), lnb=row(rw_ln_b[l]),
        router_t=router_w[l].T, g1=row(norm1_g[l]), g2=row(norm2_g[l]),
    )


def _moe(x1, h2, aff, gt2, w_gate, w_up, w_down, layer, *, tt):
    n = x1.shape[1]
    cap = CAPACITY_FACTOR * n // N_EXPERTS
    slots = -(-cap // LANES) * LANES
    pos, pos_t, aff_t = _route(aff, cap=cap)
    xs = _gather(pos, h2, slots=slots)
    y = _experts(xs, w_gate, w_up, w_down, layer)
    return _combine(x1, gt2, pos_t, aff_t, y, tt=tt)


def kernel(x, c, ctx, c_ctx, ada_w, ada_b, norm1_g, norm2_g, w_in, w_out, conv_w, q_norm_g, k_norm_g, rw_mu,
           rw_w0, rw_w_b, rw_a0, rw_a_b, rw_g_b, rw_k_k, rw_k_a, rw_r_k, rw_ln_w, rw_ln_b, router_w,
           exp_w_gate, exp_w_up, exp_w_down):
    bsz, n, d = x.shape
    n_ctx = ctx.shape[1]
    depth = ada_w.shape[0]
    tm = min(512, n)
    tm_c = min(512, n_ctx)
    tq = min(256, n)
    tq_c = min(256, n_ctx)

    rows = -(-(bsz + 1) // SUBLANES) * SUBLANES
    cvecs = jnp.concatenate([c, c_ctx[None], jnp.zeros((rows - bsz - 1, d), F32)], axis=0)
    mod = _adaln(cvecs, ada_w, ada_b).reshape(depth, rows, N_MOD, d)

    cos, sin = _rope_tables(n)
    cos_c = jnp.ones((n_ctx, LANES), F32)
    sin_c = jnp.zeros((n_ctx, LANES), F32)
    ones_q = _block_ones(Q_COLS)
    ones_rw = _block_ones(RW)
    zero_state = jnp.zeros((2, bsz, RW, RW), F32)

    xc = ctx
    for l in range(depth):
        update_ctx = l < depth - 1
        p = _layer_params(l, w_in, w_out, conv_w, q_norm_g, k_norm_g, rw_mu, rw_w0, rw_w_b, rw_a0, rw_a_b, rw_g_b,
                          rw_k_k, rw_k_a, rw_r_k, rw_ln_w, rw_ln_b, router_w, norm1_g, norm2_g)
        lat = [mod[l, :bsz, i][:, None, :] for i in range(N_MOD)]
        cmod = [mod[l, bsz:bsz + 1, i][:, None, :] for i in range(N_MOD)]

        pc_l, q_l, k_l, v_l, rw_l = _inproj(x, lat[0], lat[1], p["g1"], p["w_in"], ones_q, p["qg"], p["kg"],
                                            cos, sin, rope=True, tm=tm)
        pc_c, q_c, k_c, v_c, rw_c = _inproj(xc, cmod[0], cmod[1], p["g1"], p["w_in"], ones_q, p["qg"], p["kg"],
                                            cos_c, sin_c, rope=False, tm=tm_c)
        feat_args = (p["conv_w"], p["mu"], p["w0"], p["wb"], p["a0"], p["ab"], p["gb"], p["kk"], p["ka"], ones_rw)
        conv_l, r_l, vv_l, kk_l, g_l, lw_l, kd_l, bd_l = _features(pc_l, rw_l, *feat_args, tm=min(256, n))
        conv_c, r_c, vv_c, kk_c, g_c, lw_c, kd_c, bd_c = _features(pc_c, rw_c, *feat_args, tm=min(256, n_ctx))

        att_l = _attend(q_l, jnp.concatenate([k_l, k_c], axis=1), jnp.concatenate([v_l, v_c], axis=1), tq=tq)

        yf_c, yb_c, s_c = _wkv(r_c, vv_c, kk_c, lw_c, kd_c, bd_c, zero_state)
        yf_l, yb_l, _ = _wkv(r_l, vv_l, kk_l, lw_l, kd_l, bd_l, s_c)
        ys_l, ys_c = (yf_l, yb_l), (yf_c, yb_c)

        out_args = (p["g2"], p["w_out"], p["rk"], p["lnw"], p["lnb"], ones_rw, p["router_t"])
        x1, h2, aff = _outproj(x, conv_l, att_l, ys_l[0], ys_l[1], r_l, vv_l, g_l, kd_l,
                               lat[2], lat[3], lat[4], *out_args, tm=tm)
        x = _moe(x1, h2, aff, lat[5], exp_w_gate, exp_w_up, exp_w_down, l, tt=min(256, n))
        if update_ctx:
            att_c = _attend(q_c, k_c, v_c, tq=tq_c)
            xc1, hc2, aff_c = _outproj(xc, conv_c, att_c, ys_c[0], ys_c[1], r_c, vv_c, g_c, kd_c,
                                       cmod[2], cmod[3], cmod[4], *out_args, tm=tm_c)
            xc = _moe(xc1, hc2, aff_c, cmod[5], exp_w_gate, exp_w_up, exp_w_down, l, tt=min(256, n_ctx))
    return x
```

```python
import functools

import jax
import jax.numpy as jnp
from jax import lax
from jax.experimental import pallas as pl
from jax.experimental.pallas import tpu as pltpu

F32 = jnp.float32
BF16 = jnp.bfloat16
I32 = jnp.int32
HIGHEST = lax.Precision.HIGHEST

HEAD = 64
N_Q_HEADS = 8
N_KV_HEADS = 2
CONV_W = 256
RW = 256
RW_HEADS = RW // HEAD
DECAY_LORA = 64
AAA_LORA = 64
GATE_LORA = 160
N_MOD = 6
N_EXPERTS = 16
CAPACITY_FACTOR = 2
GRID_W = 64
ROPE_THETA = 10000.0
NORM_EPS = 1e-6
GN_EPS = 64e-5
Q_SCALE = 1.4426950408889634 * HEAD ** -0.5

CONV_COLS = 3 * CONV_W
Q_COLS = N_Q_HEADS * HEAD
KV_COLS = N_KV_HEADS * HEAD
ATT_COLS = Q_COLS + 2 * KV_COLS
RW_COLS = 3 * RW + 2 * (DECAY_LORA + AAA_LORA) + GATE_LORA
LANES = 128
SUBLANES = 8
RW_COLS_PAD = -(-RW_COLS // LANES) * LANES
GATE_LORA_PAD = RW_COLS_PAD - (3 * RW + 2 * (DECAY_LORA + AAA_LORA))
PROJ_PAD = CONV_COLS + ATT_COLS + RW_COLS_PAD

WKV_CHUNK = 64
VMEM_LIMIT = 56 * 1024 * 1024
WKV_PRECISION = dict(score="bf16", inverse="bf16", apply="bf16", state="bf16")

Q_HEAD_ORDER = (0, 4, 1, 5, 2, 6, 3, 7)


def _cparams(*sem):
    return pltpu.CompilerParams(dimension_semantics=sem, vmem_limit_bytes=VMEM_LIMIT)


def _dot(a, b, precision=None):
    return jnp.dot(a, b, preferred_element_type=F32, precision=precision)


def _dot_nt(a, b, precision=None):
    return lax.dot_general(a, b, (((1,), (1,)), ((), ())), preferred_element_type=F32, precision=precision)


def _dot_tn(a, b, precision=None):
    return lax.dot_general(a, b, (((0,), (0,)), ((), ())), preferred_element_type=F32, precision=precision)


def _split_bf16(x):
    hi = x.astype(BF16)
    return hi, (x - hi.astype(F32)).astype(BF16)


_NN = ((1,), (0,))
_NT = ((1,), (1,))
_TN = ((0,), (0,))


def _contract(a, b, dims, mode):
    dn = (dims, ((), ()))
    if mode == "f32":
        return lax.dot_general(a, b, dn, preferred_element_type=F32, precision=HIGHEST)
    if mode == "bf16":
        return lax.dot_general(a.astype(BF16), b.astype(BF16), dn, preferred_element_type=F32)
    ah, al = _split_bf16(a)
    bh, bl = _split_bf16(b)
    dot = lambda x, y: lax.dot_general(x, y, dn, preferred_element_type=F32)
    return dot(ah, bh) + (dot(ah, bl) + dot(al, bh))


def _sigmoid(x):
    return 1.0 / (1.0 + jnp.exp(-x))


def _group_sum(t, ones_bd):
    hi = t.astype(BF16)
    lo = (t - hi.astype(F32)).astype(BF16)
    return _dot(hi, ones_bd) + _dot(lo, ones_bd)


def _block_ones(width):
    g = jnp.arange(width) // HEAD
    return (g[:, None] == g[None, :]).astype(BF16)


def _adaln_kernel(c_ref, w_ref, b_ref, o_ref):
    c = c_ref[...]
    s = c * _sigmoid(c)
    o_ref[0] = _dot(s, w_ref[0], HIGHEST) + b_ref[0]


def _adaln(cvecs, ada_w, ada_b):
    depth, d, n = ada_w.shape
    r = cvecs.shape[0]
    tn = 1024
    return pl.pallas_call(
        _adaln_kernel,
        grid=(depth, n // tn),
        in_specs=[
            pl.BlockSpec((r, d), lambda l, j: (0, 0)),
            pl.BlockSpec((1, d, tn), lambda l, j: (l, 0, j)),
            pl.BlockSpec((1, 1, tn), lambda l, j: (l, 0, j)),
        ],
        out_specs=pl.BlockSpec((1, r, tn), lambda l, j: (l, 0, j)),
        out_shape=jax.ShapeDtypeStruct((depth, r, n), F32),
        compiler_params=_cparams("parallel", "parallel"),
        name="adaln",
    )(cvecs, ada_w, ada_b.reshape(depth, 1, n))


def _modulated_norm(x, g, shift, scale):
    ms = jnp.mean(x * x, axis=-1, keepdims=True)
    return (x * lax.rsqrt(ms + NORM_EPS) * g) * (1.0 + scale) + shift


def _head_norm(t, ones_bd, gain):
    ss = _group_sum(t * t, ones_bd)
    return t * lax.rsqrt(ss * (1.0 / HEAD) + NORM_EPS) * gain


def _rope(t, cos, sin_signed):
    width = t.shape[1]
    lane = lax.broadcasted_iota(I32, t.shape, 1)
    first_half = (lane & (HEAD - 1)) < HEAD // 2
    swapped = jnp.where(first_half, pltpu.roll(t, width - HEAD // 2, 1), pltpu.roll(t, HEAD // 2, 1))
    reps = width // LANES
    if reps > 1:
        cos = jnp.concatenate([cos] * reps, axis=1)
        sin_signed = jnp.concatenate([sin_signed] * reps, axis=1)
    return t * cos + swapped * sin_signed


def _inproj_kernel(x_ref, sh_ref, sc_ref, g_ref, w_ref, bd_ref, qg_ref, kg_ref, cos_ref, sin_ref,
                   pc_ref, q_ref, k_ref, v_ref, rw_ref, *, rope):
    h = _modulated_norm(x_ref[0], g_ref[...], sh_ref[0], sc_ref[0])
    p = _dot(h.astype(BF16), w_ref[...])
    pc_ref[0] = p[:, :CONV_COLS]
    q = p[:, CONV_COLS:CONV_COLS + Q_COLS]
    k = p[:, CONV_COLS + Q_COLS:CONV_COLS + Q_COLS + KV_COLS]
    v = p[:, CONV_COLS + Q_COLS + KV_COLS:CONV_COLS + ATT_COLS]
    q = _head_norm(q, bd_ref[...], qg_ref[...])
    k = _head_norm(k, bd_ref[:KV_COLS, :KV_COLS], kg_ref[...])
    if rope:
        q = _rope(q, cos_ref[...], sin_ref[...])
        k = _rope(k, cos_ref[...], sin_ref[...])
    q_ref[0] = (q * Q_SCALE).astype(BF16)
    k_ref[0] = k.astype(BF16)
    v_ref[0] = v.astype(BF16)
    rw_ref[0] = p[:, CONV_COLS + ATT_COLS:]


def _inproj(x, shift, scale, g, w_in, ones_q, qg, kg, cos, sin, *, rope, tm):
    bsz, n, d = x.shape
    mod_map = (lambda b, i: (b, 0, 0)) if shift.shape[0] == bsz else (lambda b, i: (0, 0, 0))
    const2 = lambda b, i: (0, 0)
    tile = lambda w: pl.BlockSpec((1, tm, w), lambda b, i: (b, i, 0))
    return pl.pallas_call(
        functools.partial(_inproj_kernel, rope=rope),
        grid=(bsz, n // tm),
        in_specs=[
            tile(d),
            pl.BlockSpec((1, 1, d), mod_map),
            pl.BlockSpec((1, 1, d), mod_map),
            pl.BlockSpec((1, d), const2),
            pl.BlockSpec((d, PROJ_PAD), const2),
            pl.BlockSpec((Q_COLS, Q_COLS), const2),
            pl.BlockSpec((1, Q_COLS), const2),
            pl.BlockSpec((1, KV_COLS), const2),
            pl.BlockSpec((tm, LANES), lambda b, i: (i, 0)),
            pl.BlockSpec((tm, LANES), lambda b, i: (i, 0)),
        ],
        out_specs=[tile(CONV_COLS), tile(Q_COLS), tile(KV_COLS), tile(KV_COLS), tile(RW_COLS_PAD)],
        out_shape=[
            jax.ShapeDtypeStruct((bsz, n, CONV_COLS), F32),
            jax.ShapeDtypeStruct((bsz, n, Q_COLS), BF16),
            jax.ShapeDtypeStruct((bsz, n, KV_COLS), BF16),
            jax.ShapeDtypeStruct((bsz, n, KV_COLS), BF16),
            jax.ShapeDtypeStruct((bsz, n, RW_COLS_PAD), F32),
        ],
        compiler_params=_cparams("parallel", "parallel"),
        name="inproj",
    )(x, shift, scale, g, w_in, ones_q, qg, kg, cos, sin)


def _shifted(x, prev_row, next_row):
    rows = lax.broadcasted_iota(I32, x.shape, 0)
    tm = x.shape[0]
    before = jnp.where(rows == 0, prev_row, pltpu.roll(x, 1, 0))
    after = jnp.where(rows == tm - 1, next_row, pltpu.roll(x, tm - 1, 0))
    return before, after


def _features_kernel(pc_ref, pc_prev_ref, pc_next_ref, rw_ref, rw_prev_ref, rw_next_ref,
                     conv_w_ref, mu_ref, w0_ref, wb_ref, a0_ref, ab_ref, gb_ref, kk_ref, ka_ref, bd_ref,
                     conv_ref, r_ref, v_ref, kkn_ref, g_ref, lw_ref, kd_ref, bdir_ref):
    i = pl.program_id(1)
    has_prev = (i > 0).astype(F32)
    has_next = (i < pl.num_programs(1) - 1).astype(F32)

    pc = pc_ref[0]
    pcp = pc_prev_ref[0][SUBLANES - 1:SUBLANES, :] * has_prev
    pcn = pc_next_ref[0][0:1, :] * has_next
    z = pc[:, CONV_W:2 * CONV_W] * pc[:, 2 * CONV_W:]
    zp = pcp[:, CONV_W:2 * CONV_W] * pcp[:, 2 * CONV_W:]
    zn = pcn[:, CONV_W:2 * CONV_W] * pcn[:, 2 * CONV_W:]
    zb, za = _shifted(z, zp, zn)
    cw = conv_w_ref[...]
    conv_ref[0] = pc[:, :CONV_W] * (cw[0:1] * zb + cw[1:2] * z + cw[2:3] * za)

    p = rw_ref[0]
    pb, pa = _shifted(p, rw_prev_ref[0][SUBLANES - 1:SUBLANES, :] * has_prev, rw_next_ref[0][0:1, :] * has_next)
    p = p + mu_ref[...] * (0.5 * (pb + pa) - p)
    r = p[:, :RW]
    k = p[:, RW:2 * RW]
    v = p[:, 2 * RW:3 * RW]
    o = 3 * RW
    wl = p[:, o:o + 2 * DECAY_LORA]
    al = p[:, o + 2 * DECAY_LORA:o + 2 * (DECAY_LORA + AAA_LORA)]
    gl = p[:, o + 2 * (DECAY_LORA + AAA_LORA):]
    wz = -(w0_ref[...] + _dot(jnp.tanh(wl), wb_ref[...], HIGHEST))
    softplus = jnp.maximum(wz, 0.0) + jnp.log(1.0 + jnp.exp(-jnp.abs(wz)))
    lw_ref[0] = -jnp.exp(-softplus - 0.5)
    a = _sigmoid(a0_ref[...] + _dot(al, ab_ref[...], HIGHEST))
    kk = k * kk_ref[...]
    kk = kk * lax.rsqrt(jnp.maximum(_group_sum(kk * kk, bd_ref[...]), 1e-24))
    k2 = jnp.concatenate([k, k], axis=1)
    kk2 = jnp.concatenate([kk, kk], axis=1)
    ka2 = jnp.concatenate([ka_ref[...], ka_ref[...]], axis=1)
    kd_ref[0] = k2 * (1.0 + (a - 1.0) * ka2)
    bdir_ref[0] = kk2 * a
    r_ref[0] = r
    v_ref[0] = v
    kkn_ref[0] = kk
    g_ref[0] = _dot(_sigmoid(gl), gb_ref[...], HIGHEST)


def _features(pc, prw, conv_w, mu, w0, wb, a0, ab, gb, kk, ka, ones_rw, *, tm):
    bsz, n, _ = pc.shape
    nt = n // tm
    r8 = tm // SUBLANES
    tile = lambda w: pl.BlockSpec((1, tm, w), lambda b, i: (b, i, 0))
    prev = lambda w: pl.BlockSpec((1, SUBLANES, w), lambda b, i: (b, jnp.maximum(i * r8 - 1, 0), 0))
    nxt = lambda w: pl.BlockSpec((1, SUBLANES, w), lambda b, i: (b, jnp.minimum((i + 1) * r8, nt * r8 - 1), 0))
    full = lambda a: pl.BlockSpec(a.shape, lambda b, i: (0,) * a.ndim)
    params = (conv_w, mu, w0, wb, a0, ab, gb, kk, ka, ones_rw)
    widths = (CONV_W, RW, RW, RW, RW, 2 * RW, 2 * RW, 2 * RW)
    return pl.pallas_call(
        _features_kernel,
        grid=(bsz, nt),
        in_specs=[tile(CONV_COLS), prev(CONV_COLS), nxt(CONV_COLS),
                  tile(RW_COLS_PAD), prev(RW_COLS_PAD), nxt(RW_COLS_PAD)] + [full(a) for a in params],
        out_specs=[tile(w) for w in widths],
        out_shape=[jax.ShapeDtypeStruct((bsz, n, w), F32) for w in widths],
        compiler_params=_cparams("parallel", "parallel"),
        name="features",
    )(pc, pc, pc, prw, prw, prw, *params)


def _attend_kernel(q_ref, k_ref, v_ref, o_ref):
    k = k_ref[0]
    v = v_ref[0]
    lane = lax.broadcasted_iota(I32, (q_ref.shape[1], LANES), 1)
    low = lane < HEAD
    key_low = lax.broadcasted_iota(I32, v.shape, 1) < HEAD
    ones = jnp.ones_like(v)
    v_sides = {True: jnp.where(key_low, v, ones), False: jnp.where(key_low, ones, v)}
    for c in range(Q_COLS // LANES):
        qc = q_ref[0, :, c * LANES:(c + 1) * LANES]
        outs = []
        for use_low in (True, False):
            qm = jnp.where(low == use_low, qc, jnp.zeros_like(qc))
            s = _dot_nt(qm, k)
            p = jnp.exp2(s - jnp.max(s, axis=-1, keepdims=True))
            o = _dot(p.astype(BF16), v_sides[use_low])
            outs.append(o / pltpu.roll(o, HEAD, 1))
        o_ref[0, :, c * LANES:(c + 1) * LANES] = jnp.where(low, outs[0], outs[1]).astype(BF16)


def _attend(q, k, v, *, tq):
    bsz, n, _ = q.shape
    nk = k.shape[1]
    return pl.pallas_call(
        _attend_kernel,
        grid=(bsz, n // tq),
        in_specs=[
            pl.BlockSpec((1, tq, Q_COLS), lambda b, i: (b, i, 0)),
            pl.BlockSpec((1, nk, KV_COLS), lambda b, i: (b, 0, 0)),
            pl.BlockSpec((1, nk, KV_COLS), lambda b, i: (b, 0, 0)),
        ],
        out_specs=pl.BlockSpec((1, tq, Q_COLS), lambda b, i: (b, i, 0)),
        out_shape=jax.ShapeDtypeStruct((bsz, n, Q_COLS), BF16),
        compiler_params=_cparams("parallel", "parallel"),
        name="attend",
    )(q, k, v)


def _head_blocks(x, mode):
    if mode == "bf16":
        x = x.astype(BF16)
    head = lax.broadcasted_iota(I32, x.shape, 1) >> 6
    return jnp.concatenate([jnp.where(head == h, x, jnp.zeros_like(x)) for h in range(RW_HEADS)], axis=0)


def _wkv_chunks(chains, prec):
    c = WKV_CHUNK
    ps, pi, pa, pst = prec["score"], prec["inverse"], prec["apply"], prec["state"]
    ri = lax.broadcasted_iota(I32, (c, c), 0)
    ci = lax.broadcasted_iota(I32, (c, c), 1)
    row = lax.broadcasted_iota(I32, (c, RW_HEADS * c), 0)
    col = lax.broadcasted_iota(I32, (c, RW_HEADS * c), 1) & (c - 1)
    eye = (col == row).astype(F32)
    masks = {}
    for reverse in sorted({ch[7] for ch in chains}):
        masks[reverse] = (((ci >= ri) if reverse else (ci <= ri)).astype(BF16),
                          (col > row) if reverse else (col < row),
                          (col >= row) if reverse else (col <= row))

    pre = []
    for r, v, kk, lw, kd, bd, s, reverse in chains:
        tri = masks[reverse][0]
        lw_hi, lw_lo = _split_bf16(lw)
        lw_mid, lw_lo = _split_bf16(lw - lw_hi.astype(F32))
        cum = _dot(tri, lw_hi) + (_dot(tri, lw_mid) + _dot(tri, lw_lo))
        total = jnp.sum(lw, axis=0, keepdims=True)
        inv = jnp.exp(-cum)
        rem = jnp.exp(total - cum)
        pre.append(dict(a_t=-kk * jnp.exp(cum - lw), r_t=r * jnp.exp(cum), b_t=bd * inv, k_t=kd * inv,
                        b_end=bd * rem, k_end=kd * rem, decay=jnp.exp(total)))

    scores = []
    for ch, q in zip(chains, pre):
        _, strict, incl = masks[ch[7]]
        ar = jnp.concatenate([q["a_t"], q["r_t"]], axis=0)
        sb = _contract(ar, _head_blocks(q["b_t"], ps), _NT, ps)
        sk = _contract(ar, _head_blocks(q["k_t"], ps), _NT, ps)
        scores.append(dict(a_ab=jnp.where(strict, sb[:c], 0.0), a_rb=jnp.where(incl, sb[c:], 0.0),
                           a_ak=jnp.where(strict, sk[:c], 0.0), a_rk=jnp.where(incl, sk[c:], 0.0)))

    t_inv = [eye + sc["a_ab"] for sc in scores]
    m = [sc["a_ab"] for sc in scores]
    m_blocks = [_head_blocks(x, pi) for x in m]
    for _ in range((c - 1).bit_length() - 1):
        m = [_contract(x, xb, _NN, pi) for x, xb in zip(m, m_blocks)]
        m_blocks = [_head_blocks(x, pi) for x in m]
        t_inv = [t + _contract(t, xb, _NN, pi) for t, xb in zip(t_inv, m_blocks)]

    av = [_contract(jnp.concatenate([sc["a_ak"], sc["a_rk"]], axis=0), _head_blocks(ch[1], pa), _NN, pa)
          for ch, sc in zip(chains, scores)]
    w1 = [_contract(t, _head_blocks(q["a_t"], pa), _NN, pa) for t, q in zip(t_inv, pre)]
    w2 = [_contract(t, _head_blocks(x[:c], pa), _NN, pa) for t, x in zip(t_inv, av)]
    g = [_contract(jnp.concatenate([x, q["r_t"]], axis=0), ch[6], _NT, pst) for x, q, ch in zip(w1, pre, chains)]
    u = [x[:c] + y for x, y in zip(g, w2)]
    ys = [x[c:] + _contract(sc["a_rb"], _head_blocks(uu, pa), _NN, pa) + y[c:]
          for x, sc, uu, y in zip(g, scores, u, av)]
    upd = [_contract(jnp.concatenate([uu, ch[1]], axis=0), jnp.concatenate([q["b_end"], q["k_end"]], axis=0), _TN, pst)
           for uu, ch, q in zip(u, chains, pre)]
    hr = lax.broadcasted_iota(I32, (RW, RW), 0) >> 6
    hc = lax.broadcasted_iota(I32, (RW, RW), 1) >> 6
    s_new = [ch[6] * q["decay"] + jnp.where(hr == hc, x, 0.0) for ch, q, x in zip(chains, pre, upd)]
    return ys, s_new


def _wkv_kernel(rf_ref, vf_ref, kkf_ref, lwf_ref, kdf_ref, bdf_ref, rb_ref, vb_ref, kkb_ref, lwb_ref, kdb_ref,
                bdb_ref, s0_ref, yf_ref, yb_ref, s_out_ref, s_scr, *, prec):
    step = pl.program_id(0)

    @pl.when(step == 0)
    def _():
        s_scr[...] = s0_ref[...]

    bsz = rf_ref.shape[0]
    chains = []
    for b in range(bsz):
        chains.append((rf_ref[b], vf_ref[b], kkf_ref[b], lwf_ref[b], kdf_ref[b], bdf_ref[b], s_scr[0, b], False))
        chains.append((rb_ref[b], vb_ref[b], kkb_ref[b], lwb_ref[b], kdb_ref[b], bdb_ref[b], s_scr[1, b], True))
    ys, s_new = _wkv_chunks(chains, prec)
    for b in range(bsz):
        yf_ref[b] = ys[2 * b]
        yb_ref[b] = ys[2 * b + 1]
        s_scr[0, b] = s_new[2 * b]
        s_scr[1, b] = s_new[2 * b + 1]

    @pl.when(step == pl.num_programs(0) - 1)
    def _():
        s_out_ref[...] = s_scr[...]


def _wkv(r, v, kk, lw, kd, bd, s0, prec=None):
    prec = prec or WKV_PRECISION
    bsz, n, _ = r.shape
    steps = n // WKV_CHUNK
    blk = (bsz, WKV_CHUNK, RW)
    fwd = pl.BlockSpec(blk, lambda i: (0, i, 0))
    bwd = pl.BlockSpec(blk, lambda i: (0, steps - 1 - i, 0))
    bwd_dir = pl.BlockSpec(blk, lambda i: (0, steps - 1 - i, 1))
    state = pl.BlockSpec((2, bsz, RW, RW), lambda i: (0, 0, 0, 0))
    y_shape = jax.ShapeDtypeStruct((bsz, n, RW), F32)
    return pl.pallas_call(
        functools.partial(_wkv_kernel, prec=prec),
        grid=(steps,),
        in_specs=[fwd, fwd, fwd, fwd, fwd, fwd, bwd, bwd, bwd, bwd_dir, bwd_dir, bwd_dir, state],
        out_specs=[fwd, bwd, state],
        out_shape=[y_shape, y_shape, jax.ShapeDtypeStruct((2, bsz, RW, RW), F32)],
        scratch_shapes=[pltpu.VMEM((2, bsz, RW, RW), F32)],
        compiler_params=_cparams("arbitrary"),
        name="wkv",
    )(r, v, kk, lw, kd, bd, r, v, kk, lw, kd, bd, s0)


def _outproj_kernel(x_ref, conv_ref, att_ref, yf_ref, yb_ref, r_ref, v_ref, g_ref, kd_ref,
                    gt1_ref, sh2_ref, sc2_ref, g2_ref, w_ref, rk_ref, lnw_ref, lnb_ref, bd_ref, rwt_ref,
                    x1_ref, h2_ref, aff_ref):
    ones_bd = bd_ref[...]
    y = yf_ref[0] + yb_ref[0]
    mean = _group_sum(y, ones_bd) * (1.0 / HEAD)
    yc = y - mean
    var = _group_sum(yc * yc, ones_bd) * (1.0 / HEAD)
    yn = yc * lax.rsqrt(var + GN_EPS)
    kd = kd_ref[0]
    k_mean = 0.5 * (kd[:, :RW] + kd[:, RW:])
    bonus = _group_sum(r_ref[0] * k_mean * rk_ref[...], ones_bd) * v_ref[0]
    rwkv = (yn * lnw_ref[...] + lnb_ref[...] + bonus) * g_ref[0]

    mixed = (_dot(conv_ref[0].astype(BF16), w_ref[:CONV_W])
             + _dot(att_ref[0], w_ref[CONV_W:CONV_W + Q_COLS])
             + _dot(rwkv.astype(BF16), w_ref[CONV_W + Q_COLS:]))
    x1 = x_ref[0] + gt1_ref[0] * mixed
    x1_ref[0] = x1

    h2 = _modulated_norm(x1, g2_ref[...], sh2_ref[0], sc2_ref[0])
    h2_ref[0] = h2.astype(BF16)
    logits = _dot_nt(rwt_ref[...], h2, HIGHEST)
    e = jnp.exp(logits - jnp.max(logits, axis=0, keepdims=True))
    aff_ref[0] = e / jnp.sum(e, axis=0, keepdims=True)


def _outproj(x, conv, att, yf, yb, r, v, g, kd, gt1, sh2, sc2, g2, w_out, rk, lnw, lnb, ones_rw, router_t, *, tm):
    bsz, n, d = x.shape
    mod_map = (lambda b, i: (b, 0, 0)) if gt1.shape[0] == bsz else (lambda b, i: (0, 0, 0))
    tile = lambda w: pl.BlockSpec((1, tm, w), lambda b, i: (b, i, 0))
    mod = pl.BlockSpec((1, 1, d), mod_map)
    full = lambda a: pl.BlockSpec(a.shape, lambda b, i: (0,) * a.ndim)
    params = (g2, w_out, rk, lnw, lnb, ones_rw, router_t)
    return pl.pallas_call(
        _outproj_kernel,
        grid=(bsz, n // tm),
        in_specs=[tile(d), tile(CONV_W), tile(Q_COLS), tile(RW), tile(RW), tile(RW), tile(RW), tile(RW),
                  tile(2 * RW), mod, mod, mod] + [full(a) for a in params],
        out_specs=[tile(d), tile(d), pl.BlockSpec((1, N_EXPERTS, tm), lambda b, i: (b, 0, i))],
        out_shape=[
            jax.ShapeDtypeStruct((bsz, n, d), F32),
            jax.ShapeDtypeStruct((bsz, n, d), BF16),
            jax.ShapeDtypeStruct((bsz, N_EXPERTS, n), F32),
        ],
        compiler_params=_cparams("parallel", "parallel"),
        name="outproj",
    )(x, conv, att, yf, yb, r, v, g, kd, gt1, sh2, sc2, *params)


ROUTE_TILE = 256


def _cumsum_lanes(x):
    n = x.shape[1]
    w = min(n, ROUTE_TILE)
    ri = lax.broadcasted_iota(I32, (w, w), 0)
    ci = lax.broadcasted_iota(I32, (w, w), 1)
    tri = (ri <= ci).astype(BF16)
    carry = jnp.zeros((x.shape[0], 1), F32)
    parts, carries = [], [carry]
    for j in range(n // w):
        cs = _dot(x[:, j * w:(j + 1) * w].astype(BF16), tri) + carry
        carry = cs[:, w - 1:w]
        parts.append(cs)
        carries.append(carry)
    return (parts[0] if len(parts) == 1 else jnp.concatenate(parts, axis=1)), carries


def _to_token_major(x):
    n = x.shape[1]
    pad = jnp.zeros((LANES - x.shape[0], LANES), F32)
    blocks = [jnp.concatenate([x[:, j * LANES:(j + 1) * LANES], pad], axis=0).T for j in range(n // LANES)]
    return blocks[0] if len(blocks) == 1 else jnp.concatenate(blocks, axis=0)


def _route_kernel(aff_ref, pos_ref, pos_t_ref, aff_t_ref, off_ref, *, cap):
    aff = aff_ref[0]
    bits = pltpu.bitcast(aff, I32)

    def search(i, thr):
        cand = thr | (1 << (30 - i))
        cnt = jnp.sum((bits >= cand).astype(F32), axis=1, keepdims=True)
        return jnp.where(cnt >= cap, cand, thr)

    thr = lax.fori_loop(0, 31, search, jnp.zeros((aff.shape[0], 1), I32))
    above = bits > thr
    tied = bits == thr
    need = cap - jnp.sum(above.astype(F32), axis=1, keepdims=True)
    tied_f = tied.astype(F32)
    chosen = jnp.where(above | (tied & (_cumsum_lanes(tied_f)[0] <= need)), 1.0, 0.0)
    rank, before_tile = _cumsum_lanes(chosen)
    pos = jnp.where(chosen > 0.0, rank - 1.0, -1.0)
    pos_ref[0] = pos
    pos_t_ref[0] = _to_token_major(pos)
    aff_t_ref[0] = _to_token_major(aff)
    lane = lax.broadcasted_iota(I32, (aff.shape[0], LANES), 1)
    off = jnp.zeros((aff.shape[0], LANES), F32)
    for j, count in enumerate(before_tile):
        off = jnp.where(lane == j, count, off)
    off_ref[0] = off.astype(I32)


def _route(aff, *, cap):
    bsz, e, n = aff.shape
    return pl.pallas_call(
        functools.partial(_route_kernel, cap=cap),
        grid=(bsz,),
        in_specs=[pl.BlockSpec((1, e, n), lambda b: (b, 0, 0))],
        out_specs=[pl.BlockSpec((1, e, n), lambda b: (b, 0, 0)),
                   pl.BlockSpec((1, n, LANES), lambda b: (b, 0, 0)),
                   pl.BlockSpec((1, n, LANES), lambda b: (b, 0, 0)),
                   pl.BlockSpec((1, e, LANES), lambda b: (b, 0, 0))],
        out_shape=[jax.ShapeDtypeStruct((bsz, e, n), F32),
                   jax.ShapeDtypeStruct((bsz, n, LANES), F32),
                   jax.ShapeDtypeStruct((bsz, n, LANES), F32),
                   jax.ShapeDtypeStruct((bsz, e, LANES), I32)],
        compiler_params=_cparams("parallel"),
        name="route",
    )(aff)


def _tiles_meet(off_ref, b, e, token_tile, slot_tile, st):
    lo = off_ref[b, e, token_tile]
    hi = off_ref[b, e, token_tile + 1]
    return (lo < (slot_tile + 1) * st) & (hi > slot_tile * st)


def _gather_kernel(off_ref, pos_ref, h_ref, xs_ref, acc_ref, *, tt, st):
    b, e = pl.program_id(0), pl.program_id(1)
    n = h_ref.shape[1]
    slots = xs_ref.shape[2]
    for j in range(slots // st):
        acc_ref[...] = jnp.zeros_like(acc_ref)
        slot = (lax.broadcasted_iota(I32, (st, tt), 0) + j * st).astype(F32)
        for t in range(n // tt):
            @pl.when(_tiles_meet(off_ref, b, e, t, j, st))
            def _():
                pos = pos_ref[0, pl.ds(e, 1), t * tt:(t + 1) * tt]
                onehot = jnp.where(slot == pos, 1.0, 0.0).astype(BF16)
                acc_ref[...] += _dot(onehot, h_ref[0, t * tt:(t + 1) * tt, :])
        xs_ref[0, 0, j * st:(j + 1) * st, :] = acc_ref[...].astype(BF16)


def _gather(off, pos, h2, *, slots):
    bsz, n, d = h2.shape
    e = pos.shape[1]
    tt, st = min(ROUTE_TILE, n), min(ROUTE_TILE, slots)
    return pl.pallas_call(
        functools.partial(_gather_kernel, tt=tt, st=st),
        grid_spec=pltpu.PrefetchScalarGridSpec(
            num_scalar_prefetch=1,
            grid=(bsz, e),
            in_specs=[pl.BlockSpec((1, e, n), lambda b, j, off: (b, 0, 0)),
                      pl.BlockSpec((1, n, d), lambda b, j, off: (b, 0, 0))],
            out_specs=pl.BlockSpec((1, 1, slots, d), lambda b, j, off: (b, j, 0, 0)),
            scratch_shapes=[pltpu.VMEM((st, d), F32)],
        ),
        out_shape=jax.ShapeDtypeStruct((bsz, e, slots, d), BF16),
        compiler_params=_cparams("parallel", "arbitrary"),
        name="gather",
    )(off, pos, h2)


def _experts_kernel(*refs, streams):
    xs_refs, (wg_ref, wu_ref, wd_ref) = refs[:streams], refs[streams:streams + 3]
    y_refs, (wg_s, wu_s, wd_s) = refs[streams + 3:2 * streams + 3], refs[2 * streams + 3:]

    @pl.when(pl.program_id(1) == 0)
    def _():
        wg_s[...] = wg_ref[0, 0].astype(BF16)
        wu_s[...] = wu_ref[0, 0].astype(BF16)
        wd_s[...] = wd_ref[0, 0].astype(BF16)

    for xs_ref, y_ref in zip(xs_refs, y_refs):
        xs = xs_ref[0, 0]
        hg = _dot(xs, wg_s[...])
        hu = _dot(xs, wu_s[...])
        hid = (hg * _sigmoid(hg)) * hu
        y_ref[0, 0] = _dot(hid.astype(BF16), wd_s[...]).astype(BF16)


def _experts(xs_list, w_gate, w_up, w_down, layer):
    bsz, e, _, d = xs_list[0].shape
    f = w_gate.shape[-1]
    rows = lambda xs: pl.BlockSpec((1, 1, xs.shape[2], d), lambda j, b: (b, j, 0, 0))
    return pl.pallas_call(
        functools.partial(_experts_kernel, streams=len(xs_list)),
        grid=(e, bsz),
        in_specs=[rows(xs) for xs in xs_list] + [
            pl.BlockSpec((1, 1, d, f), lambda j, b: (layer, j, 0, 0)),
            pl.BlockSpec((1, 1, d, f), lambda j, b: (layer, j, 0, 0)),
            pl.BlockSpec((1, 1, f, d), lambda j, b: (layer, j, 0, 0))],
        out_specs=[rows(xs) for xs in xs_list],
        out_shape=[jax.ShapeDtypeStruct(xs.shape, BF16) for xs in xs_list],
        scratch_shapes=[pltpu.VMEM((d, f), BF16), pltpu.VMEM((d, f), BF16), pltpu.VMEM((f, d), BF16)],
        compiler_params=_cparams("parallel", "arbitrary"),
        name="experts",
    )(*xs_list, w_gate, w_up, w_down)


def _combine_kernel(off_ref, x_ref, gt_ref, pos_ref, gate_ref, y_ref, o_ref, acc_ref, *, st):
    b, t = pl.program_id(0), pl.program_id(1)
    slots = y_ref.shape[2]
    tt = x_ref.shape[1]
    acc_ref[...] = jnp.zeros_like(acc_ref)
    for e in range(N_EXPERTS):
        for j in range(slots // st):
            @pl.when(_tiles_meet(off_ref, b, e, t, j, st))
            def _():
                slot = (lax.broadcasted_iota(I32, (tt, st), 1) + j * st).astype(F32)
                onehot = jnp.where(slot == pos_ref[0, :, e:e + 1], 1.0, 0.0).astype(BF16)
                acc_ref[...] += _dot(onehot, y_ref[0, e, j * st:(j + 1) * st, :]) * gate_ref[0, :, e:e + 1]
    o_ref[0] = x_ref[0] + gt_ref[0] * acc_ref[...]


def _combine(off, x1, gt2, pos_t, aff_t, y):
    bsz, n, d = x1.shape
    e, slots = y.shape[1], y.shape[2]
    tt, st = min(ROUTE_TILE, n), min(ROUTE_TILE, slots)
    mod_map = (lambda b, i, off: (b, 0, 0)) if gt2.shape[0] == bsz else (lambda b, i, off: (0, 0, 0))
    tile = lambda w: pl.BlockSpec((1, tt, w), lambda b, i, off: (b, i, 0))
    return pl.pallas_call(
        functools.partial(_combine_kernel, st=st),
        grid_spec=pltpu.PrefetchScalarGridSpec(
            num_scalar_prefetch=1,
            grid=(bsz, n // tt),
            in_specs=[tile(d), pl.BlockSpec((1, 1, d), mod_map), tile(LANES), tile(LANES),
                      pl.BlockSpec((1, e, slots, d), lambda b, i, off: (b, 0, 0, 0))],
            out_specs=tile(d),
            scratch_shapes=[pltpu.VMEM((tt, d), F32)],
        ),
        out_shape=jax.ShapeDtypeStruct((bsz, n, d), F32),
        compiler_params=_cparams("parallel", "arbitrary"),
        name="combine",
    )(off, x1, gt2, pos_t, aff_t, y)


def _rope_tables(n):
    rows = n // GRID_W
    row = jnp.repeat(jnp.arange(rows, dtype=F32), GRID_W)
    col = jnp.tile(jnp.arange(GRID_W, dtype=F32), rows)
    half = HEAD // 2
    inv = ROPE_THETA ** (-jnp.arange(0, half, 2, dtype=F32) / half)
    ang = jnp.concatenate([row[:, None] * inv, col[:, None] * inv], axis=-1)
    cos, sin = jnp.cos(ang), jnp.sin(ang)
    cos = jnp.concatenate([cos, cos], axis=-1)
    sin = jnp.concatenate([-sin, sin], axis=-1)
    return jnp.tile(cos, (1, LANES // HEAD)), jnp.tile(sin, (1, LANES // HEAD))


def _block_diag2(m):
    z = jnp.zeros_like(m[0])
    return jnp.concatenate([jnp.concatenate([m[0], z], axis=1), jnp.concatenate([z, m[1]], axis=1)], axis=0)


def _permute_q_heads(w, axis):
    idx = jnp.concatenate([jnp.arange(h * HEAD, (h + 1) * HEAD) for h in Q_HEAD_ORDER])
    return jnp.take(w, idx, axis=axis)


def _layer_params(l, w_in, w_out, conv_w, q_norm_g, k_norm_g, rw_mu, rw_w0, rw_w_b, rw_a0, rw_a_b, rw_g_b,
                  rw_k_k, rw_k_a, rw_r_k, rw_ln_w, rw_ln_b, router_w, norm1_g, norm2_g):
    wi = w_in[l]
    q_cols = _permute_q_heads(wi[:, CONV_COLS:CONV_COLS + Q_COLS], 1)
    wi = jnp.concatenate([wi[:, :CONV_COLS], q_cols, wi[:, CONV_COLS + Q_COLS:],
                          jnp.zeros((wi.shape[0], RW_COLS_PAD - RW_COLS), F32)], axis=1).astype(BF16)
    wo = w_out[l]
    wo = jnp.concatenate([wo[:CONV_W], _permute_q_heads(wo[CONV_W:CONV_W + Q_COLS], 0),
                          wo[CONV_W + Q_COLS:]], axis=0).astype(BF16)
    row = lambda a: a.reshape(1, -1)
    return dict(
        w_in=wi, w_out=wo, conv_w=conv_w[l],
        qg=row(jnp.tile(q_norm_g[l], N_Q_HEADS)), kg=row(jnp.tile(k_norm_g[l], N_KV_HEADS)),
        mu=row(jnp.pad(rw_mu[l], (0, RW_COLS_PAD - RW_COLS))),
        w0=row(rw_w0[l]), wb=_block_diag2(rw_w_b[l]), a0=row(rw_a0[l]), ab=_block_diag2(rw_a_b[l]),
        gb=jnp.pad(rw_g_b[l], ((0, GATE_LORA_PAD - GATE_LORA), (0, 0))),
        kk=row(rw_k_k[l]), ka=row(rw_k_a[l]), rk=row(rw_r_k[l]), lnw=row(rw_ln_w[l]), lnb=row(rw_ln_b[l]),
        router_t=router_w[l].T, g1=row(norm1_g[l]), g2=row(norm2_g[l]),
    )


def _moe(streams, w_gate, w_up, w_down, layer):
    routed = []
    for x1, h2, aff, _ in streams:
        n = x1.shape[1]
        cap = CAPACITY_FACTOR * n // N_EXPERTS
        slots = -(-cap // LANES) * LANES
        pos, pos_t, aff_t, off = _route(aff, cap=cap)
        off = off[:, :, :n // min(ROUTE_TILE, n) + 1]
        routed.append((off, pos_t, aff_t, _gather(off, pos, h2, slots=slots)))
    ys = _experts([r[3] for r in routed], w_gate, w_up, w_down, layer)
    return [_combine(off, x1, gt2, pos_t, aff_t, y)
            for (x1, _, _, gt2), (off, pos_t, aff_t, _), y in zip(streams, routed, ys)]


def kernel(x, c, ctx, c_ctx, ada_w, ada_b, norm1_g, norm2_g, w_in, w_out, conv_w, q_norm_g, k_norm_g, rw_mu,
           rw_w0, rw_w_b, rw_a0, rw_a_b, rw_g_b, rw_k_k, rw_k_a, rw_r_k, rw_ln_w, rw_ln_b, router_w,
           exp_w_gate, exp_w_up, exp_w_down):
    bsz, n, d = x.shape
    n_ctx = ctx.shape[1]
    depth = ada_w.shape[0]
    tm = min(512, n)
    tm_c = min(512, n_ctx)
    tq = min(256, n)
    tq_c = min(256, n_ctx)

    rows = -(-(bsz + 1) // SUBLANES) * SUBLANES
    cvecs = jnp.concatenate([c, c_ctx[None], jnp.zeros((rows - bsz - 1, d), F32)], axis=0)
    mod = _adaln(cvecs, ada_w, ada_b).reshape(depth, rows, N_MOD, d)

    cos, sin = _rope_tables(n)
    cos_c = jnp.ones((n_ctx, LANES), F32)
    sin_c = jnp.zeros((n_ctx, LANES), F32)
    ones_q = _block_ones(Q_COLS)
    ones_rw = _block_ones(RW)
    zero_state = jnp.zeros((2, bsz, RW, RW), F32)

    xc = ctx
    for l in range(depth):
        update_ctx = l < depth - 1
        p = _layer_params(l, w_in, w_out, conv_w, q_norm_g, k_norm_g, rw_mu, rw_w0, rw_w_b, rw_a0, rw_a_b, rw_g_b,
                          rw_k_k, rw_k_a, rw_r_k, rw_ln_w, rw_ln_b, router_w, norm1_g, norm2_g)
        lat = [mod[l, :bsz, i][:, None, :] for i in range(N_MOD)]
        cmod = [mod[l, bsz:bsz + 1, i][:, None, :] for i in range(N_MOD)]

        pc_l, q_l, k_l, v_l, rw_l = _inproj(x, lat[0], lat[1], p["g1"], p["w_in"], ones_q, p["qg"], p["kg"],
                                            cos, sin, rope=True, tm=tm)
        pc_c, q_c, k_c, v_c, rw_c = _inproj(xc, cmod[0], cmod[1], p["g1"], p["w_in"], ones_q, p["qg"], p["kg"],
                                            cos_c, sin_c, rope=False, tm=tm_c)
        feat_args = (p["conv_w"], p["mu"], p["w0"], p["wb"], p["a0"], p["ab"], p["gb"], p["kk"], p["ka"], ones_rw)
        conv_l, r_l, vv_l, kk_l, g_l, lw_l, kd_l, bd_l = _features(pc_l, rw_l, *feat_args, tm=min(256, n))
        conv_c, r_c, vv_c, kk_c, g_c, lw_c, kd_c, bd_c = _features(pc_c, rw_c, *feat_args, tm=min(256, n_ctx))

        att_l = _attend(q_l, jnp.concatenate([k_l, k_c], axis=1), jnp.concatenate([v_l, v_c], axis=1), tq=tq)

        yf_c, yb_c, s_c = _wkv(r_c, vv_c, kk_c, lw_c, kd_c, bd_c, zero_state)
        yf_l, yb_l, _ = _wkv(r_l, vv_l, kk_l, lw_l, kd_l, bd_l, s_c)
        ys_l, ys_c = (yf_l, yb_l), (yf_c, yb_c)

        out_args = (p["g2"], p["w_out"], p["rk"], p["lnw"], p["lnb"], ones_rw, p["router_t"])
        x1, h2, aff = _outproj(x, conv_l, att_l, ys_l[0], ys_l[1], r_l, vv_l, g_l, kd_l,
                               lat[2], lat[3], lat[4], *out_args, tm=tm)
        streams = [(x1, h2, aff, lat[5])]
        if update_ctx:
            att_c = _attend(q_c, k_c, v_c, tq=tq_c)
            xc1, hc2, aff_c = _outproj(xc, conv_c, att_c, ys_c[0], ys_c[1], r_c, vv_c, g_c, kd_c,
                                       cmod[2], cmod[3], cmod[4], *out_args, tm=tm_c)
            streams.append((xc1, hc2, aff_c, cmod[5]))
        outs = _moe(streams, exp_w_gate, exp_w_up, exp_w_down, l)
        x = outs[0]
        if update_ctx:
            xc = outs[1]
    return x
```

```python
import functools

import jax
import jax.numpy as jnp
from jax import lax
from jax.experimental import pallas as pl
from jax.experimental.pallas import tpu as pltpu

F32 = jnp.float32
BF16 = jnp.bfloat16
I32 = jnp.int32
HIGHEST = lax.Precision.HIGHEST

HEAD = 64
N_Q_HEADS = 8
N_KV_HEADS = 2
CONV_W = 256
RW = 256
RW_HEADS = RW // HEAD
DECAY_LORA = 64
AAA_LORA = 64
GATE_LORA = 160
N_MOD = 6
N_EXPERTS = 16
CAPACITY_FACTOR = 2
GRID_W = 64
ROPE_THETA = 10000.0
NORM_EPS = 1e-6
GN_EPS = 64e-5
Q_SCALE = 1.4426950408889634 * HEAD ** -0.5

CONV_COLS = 3 * CONV_W
Q_COLS = N_Q_HEADS * HEAD
KV_COLS = N_KV_HEADS * HEAD
ATT_COLS = Q_COLS + 2 * KV_COLS
RW_COLS = 3 * RW + 2 * (DECAY_LORA + AAA_LORA) + GATE_LORA
LANES = 128
SUBLANES = 8
RW_COLS_PAD = -(-RW_COLS // LANES) * LANES
GATE_LORA_PAD = RW_COLS_PAD - (3 * RW + 2 * (DECAY_LORA + AAA_LORA))
PROJ_PAD = CONV_COLS + ATT_COLS + RW_COLS_PAD

WKV_CHUNK = 64
VMEM_LIMIT = 56 * 1024 * 1024
WKV_PRECISION = dict(score="bf16", inverse="bf16", apply="bf16", state="bf16")

Q_HEAD_ORDER = (0, 4, 1, 5, 2, 6, 3, 7)


def _cparams(*sem):
    return pltpu.CompilerParams(dimension_semantics=sem, vmem_limit_bytes=VMEM_LIMIT)


def _dot(a, b, precision=None):
    return jnp.dot(a, b, preferred_element_type=F32, precision=precision)


def _dot_nt(a, b, precision=None):
    return lax.dot_general(a, b, (((1,), (1,)), ((), ())), preferred_element_type=F32, precision=precision)


def _dot_tn(a, b, precision=None):
    return lax.dot_general(a, b, (((0,), (0,)), ((), ())), preferred_element_type=F32, precision=precision)


def _split_bf16(x):
    hi = x.astype(BF16)
    return hi, (x - hi.astype(F32)).astype(BF16)


_NN = ((1,), (0,))
_NT = ((1,), (1,))
_TN = ((0,), (0,))


def _contract(a, b, dims, mode):
    dn = (dims, ((), ()))
    if mode == "f32":
        return lax.dot_general(a, b, dn, preferred_element_type=F32, precision=HIGHEST)
    if mode == "bf16":
        return lax.dot_general(a.astype(BF16), b.astype(BF16), dn, preferred_element_type=F32)
    ah, al = _split_bf16(a)
    bh, bl = _split_bf16(b)
    dot = lambda x, y: lax.dot_general(x, y, dn, preferred_element_type=F32)
    return dot(ah, bh) + (dot(ah, bl) + dot(al, bh))


def _sigmoid(x):
    return 1.0 / (1.0 + jnp.exp(-x))


def _group_sum(t, ones_bd):
    hi = t.astype(BF16)
    lo = (t - hi.astype(F32)).astype(BF16)
    return _dot(hi, ones_bd) + _dot(lo, ones_bd)


def _block_ones(width):
    g = jnp.arange(width) // HEAD
    return (g[:, None] == g[None, :]).astype(BF16)


def _adaln_kernel(c_ref, w_ref, b_ref, o_ref):
    c = c_ref[...]
    s = c * _sigmoid(c)
    o_ref[0] = _dot(s, w_ref[0], HIGHEST) + b_ref[0]


def _adaln(cvecs, ada_w, ada_b):
    depth, d, n = ada_w.shape
    r = cvecs.shape[0]
    tn = 1024
    return pl.pallas_call(
        _adaln_kernel,
        grid=(depth, n // tn),
        in_specs=[
            pl.BlockSpec((r, d), lambda l, j: (0, 0)),
            pl.BlockSpec((1, d, tn), lambda l, j: (l, 0, j)),
            pl.BlockSpec((1, 1, tn), lambda l, j: (l, 0, j)),
        ],
        out_specs=pl.BlockSpec((1, r, tn), lambda l, j: (l, 0, j)),
        out_shape=jax.ShapeDtypeStruct((depth, r, n), F32),
        compiler_params=_cparams("parallel", "parallel"),
        name="adaln",
    )(cvecs, ada_w, ada_b.reshape(depth, 1, n))


def _modulated_norm(x, g, shift, scale):
    ms = jnp.mean(x * x, axis=-1, keepdims=True)
    return (x * lax.rsqrt(ms + NORM_EPS) * g) * (1.0 + scale) + shift


def _head_norm(t, ones_bd, gain):
    ss = _group_sum(t * t, ones_bd)
    return t * lax.rsqrt(ss * (1.0 / HEAD) + NORM_EPS) * gain


def _rope(t, cos, sin_signed):
    width = t.shape[1]
    lane = lax.broadcasted_iota(I32, t.shape, 1)
    first_half = (lane & (HEAD - 1)) < HEAD // 2
    swapped = jnp.where(first_half, pltpu.roll(t, width - HEAD // 2, 1), pltpu.roll(t, HEAD // 2, 1))
    reps = width // LANES
    if reps > 1:
        cos = jnp.concatenate([cos] * reps, axis=1)
        sin_signed = jnp.concatenate([sin_signed] * reps, axis=1)
    return t * cos + swapped * sin_signed


def _inproj_kernel(x_ref, sh_ref, sc_ref, g_ref, w_ref, bd_ref, qg_ref, kg_ref, cos_ref, sin_ref,
                   pc_ref, q_ref, k_ref, v_ref, rw_ref, *, rope):
    h = _modulated_norm(x_ref[0], g_ref[...], sh_ref[0], sc_ref[0])
    p = _dot(h.astype(BF16), w_ref[...])
    pc_ref[0] = p[:, :CONV_COLS]
    q = p[:, CONV_COLS:CONV_COLS + Q_COLS]
    k = p[:, CONV_COLS + Q_COLS:CONV_COLS + Q_COLS + KV_COLS]
    v = p[:, CONV_COLS + Q_COLS + KV_COLS:CONV_COLS + ATT_COLS]
    q = _head_norm(q, bd_ref[...], qg_ref[...])
    k = _head_norm(k, bd_ref[:KV_COLS, :KV_COLS], kg_ref[...])
    if rope:
        q = _rope(q, cos_ref[...], sin_ref[...])
        k = _rope(k, cos_ref[...], sin_ref[...])
    q_ref[0] = (q * Q_SCALE).astype(BF16)
    k_ref[0] = k.astype(BF16)
    v_ref[0] = v.astype(BF16)
    rw_ref[0] = p[:, CONV_COLS + ATT_COLS:]


def _inproj(x, shift, scale, g, w_in, ones_q, qg, kg, cos, sin, *, rope, tm):
    bsz, n, d = x.shape
    mod_map = (lambda b, i: (b, 0, 0)) if shift.shape[0] == bsz else (lambda b, i: (0, 0, 0))
    const2 = lambda b, i: (0, 0)
    tile = lambda w: pl.BlockSpec((1, tm, w), lambda b, i: (b, i, 0))
    return pl.pallas_call(
        functools.partial(_inproj_kernel, rope=rope),
        grid=(bsz, n // tm),
        in_specs=[
            tile(d),
            pl.BlockSpec((1, 1, d), mod_map),
            pl.BlockSpec((1, 1, d), mod_map),
            pl.BlockSpec((1, d), const2),
            pl.BlockSpec((d, PROJ_PAD), const2),
            pl.BlockSpec((Q_COLS, Q_COLS), const2),
            pl.BlockSpec((1, Q_COLS), const2),
            pl.BlockSpec((1, KV_COLS), const2),
            pl.BlockSpec((tm, LANES), lambda b, i: (i, 0)),
            pl.BlockSpec((tm, LANES), lambda b, i: (i, 0)),
        ],
        out_specs=[tile(CONV_COLS), tile(Q_COLS), tile(KV_COLS), tile(KV_COLS), tile(RW_COLS_PAD)],
        out_shape=[
            jax.ShapeDtypeStruct((bsz, n, CONV_COLS), F32),
            jax.ShapeDtypeStruct((bsz, n, Q_COLS), BF16),
            jax.ShapeDtypeStruct((bsz, n, KV_COLS), BF16),
            jax.ShapeDtypeStruct((bsz, n, KV_COLS), BF16),
            jax.ShapeDtypeStruct((bsz, n, RW_COLS_PAD), F32),
        ],
        compiler_params=_cparams("parallel", "parallel"),
        name="inproj",
    )(x, shift, scale, g, w_in, ones_q, qg, kg, cos, sin)


def _shifted(x, prev_row, next_row):
    rows = lax.broadcasted_iota(I32, x.shape, 0)
    tm = x.shape[0]
    before = jnp.where(rows == 0, prev_row, pltpu.roll(x, 1, 0))
    after = jnp.where(rows == tm - 1, next_row, pltpu.roll(x, tm - 1, 0))
    return before, after


def _features_kernel(pc_ref, pc_prev_ref, pc_next_ref, rw_ref, rw_prev_ref, rw_next_ref,
                     conv_w_ref, mu_ref, w0_ref, wb_ref, a0_ref, ab_ref, gb_ref, kk_ref, ka_ref, bd_ref,
                     conv_ref, r_ref, v_ref, kkn_ref, g_ref, lw_ref, kd_ref, bdir_ref):
    i = pl.program_id(1)
    has_prev = (i > 0).astype(F32)
    has_next = (i < pl.num_programs(1) - 1).astype(F32)

    pc = pc_ref[0]
    pcp = pc_prev_ref[0][SUBLANES - 1:SUBLANES, :] * has_prev
    pcn = pc_next_ref[0][0:1, :] * has_next
    z = pc[:, CONV_W:2 * CONV_W] * pc[:, 2 * CONV_W:]
    zp = pcp[:, CONV_W:2 * CONV_W] * pcp[:, 2 * CONV_W:]
    zn = pcn[:, CONV_W:2 * CONV_W] * pcn[:, 2 * CONV_W:]
    zb, za = _shifted(z, zp, zn)
    cw = conv_w_ref[...]
    conv_ref[0] = pc[:, :CONV_W] * (cw[0:1] * zb + cw[1:2] * z + cw[2:3] * za)

    p = rw_ref[0]
    pb, pa = _shifted(p, rw_prev_ref[0][SUBLANES - 1:SUBLANES, :] * has_prev, rw_next_ref[0][0:1, :] * has_next)
    p = p + mu_ref[...] * (0.5 * (pb + pa) - p)
    r = p[:, :RW]
    k = p[:, RW:2 * RW]
    v = p[:, 2 * RW:3 * RW]
    o = 3 * RW
    wl = p[:, o:o + 2 * DECAY_LORA]
    al = p[:, o + 2 * DECAY_LORA:o + 2 * (DECAY_LORA + AAA_LORA)]
    gl = p[:, o + 2 * (DECAY_LORA + AAA_LORA):]
    wz = -(w0_ref[...] + _dot(jnp.tanh(wl), wb_ref[...], HIGHEST))
    softplus = jnp.maximum(wz, 0.0) + jnp.log(1.0 + jnp.exp(-jnp.abs(wz)))
    lw_ref[0] = -jnp.exp(-softplus - 0.5)
    a = _sigmoid(a0_ref[...] + _dot(al, ab_ref[...], HIGHEST))
    kk = k * kk_ref[...]
    kk = kk * lax.rsqrt(jnp.maximum(_group_sum(kk * kk, bd_ref[...]), 1e-24))
    k2 = jnp.concatenate([k, k], axis=1)
    kk2 = jnp.concatenate([kk, kk], axis=1)
    ka2 = jnp.concatenate([ka_ref[...], ka_ref[...]], axis=1)
    kd_ref[0] = k2 * (1.0 + (a - 1.0) * ka2)
    bdir_ref[0] = kk2 * a
    r_ref[0] = r
    v_ref[0] = v
    kkn_ref[0] = kk
    g_ref[0] = _dot(_sigmoid(gl), gb_ref[...], HIGHEST)


def _features(pc, prw, conv_w, mu, w0, wb, a0, ab, gb, kk, ka, ones_rw, *, tm):
    bsz, n, _ = pc.shape
    nt = n // tm
    r8 = tm // SUBLANES
    tile = lambda w: pl.BlockSpec((1, tm, w), lambda b, i: (b, i, 0))
    prev = lambda w: pl.BlockSpec((1, SUBLANES, w), lambda b, i: (b, jnp.maximum(i * r8 - 1, 0), 0))
    nxt = lambda w: pl.BlockSpec((1, SUBLANES, w), lambda b, i: (b, jnp.minimum((i + 1) * r8, nt * r8 - 1), 0))
    full = lambda a: pl.BlockSpec(a.shape, lambda b, i: (0,) * a.ndim)
    params = (conv_w, mu, w0, wb, a0, ab, gb, kk, ka, ones_rw)
    widths = (CONV_W, RW, RW, RW, RW, 2 * RW, 2 * RW, 2 * RW)
    return pl.pallas_call(
        _features_kernel,
        grid=(bsz, nt),
        in_specs=[tile(CONV_COLS), prev(CONV_COLS), nxt(CONV_COLS),
                  tile(RW_COLS_PAD), prev(RW_COLS_PAD), nxt(RW_COLS_PAD)] + [full(a) for a in params],
        out_specs=[tile(w) for w in widths],
        out_shape=[jax.ShapeDtypeStruct((bsz, n, w), F32) for w in widths],
        compiler_params=_cparams("parallel", "parallel"),
        name="features",
    )(pc, pc, pc, prw, prw, prw, *params)


def _attend_kernel(q_ref, k_ref, v_ref, o_ref):
    k = k_ref[0]
    v = v_ref[0]
    lane = lax.broadcasted_iota(I32, (q_ref.shape[1], LANES), 1)
    low = lane < HEAD
    key_low = lax.broadcasted_iota(I32, v.shape, 1) < HEAD
    ones = jnp.ones_like(v)
    v_sides = {True: jnp.where(key_low, v, ones), False: jnp.where(key_low, ones, v)}
    for c in range(Q_COLS // LANES):
        qc = q_ref[0, :, c * LANES:(c + 1) * LANES]
        outs = []
        for use_low in (True, False):
            qm = jnp.where(low == use_low, qc, jnp.zeros_like(qc))
            s = _dot_nt(qm, k)
            p = jnp.exp2(s - jnp.max(s, axis=-1, keepdims=True))
            o = _dot(p.astype(BF16), v_sides[use_low])
            outs.append(o / pltpu.roll(o, HEAD, 1))
        o_ref[0, :, c * LANES:(c + 1) * LANES] = jnp.where(low, outs[0], outs[1]).astype(BF16)


def _attend(q, k, v, *, tq):
    bsz, n, _ = q.shape
    nk = k.shape[1]
    return pl.pallas_call(
        _attend_kernel,
        grid=(bsz, n // tq),
        in_specs=[
            pl.BlockSpec((1, tq, Q_COLS), lambda b, i: (b, i, 0)),
            pl.BlockSpec((1, nk, KV_COLS), lambda b, i: (b, 0, 0)),
            pl.BlockSpec((1, nk, KV_COLS), lambda b, i: (b, 0, 0)),
        ],
        out_specs=pl.BlockSpec((1, tq, Q_COLS), lambda b, i: (b, i, 0)),
        out_shape=jax.ShapeDtypeStruct((bsz, n, Q_COLS), BF16),
        compiler_params=_cparams("parallel", "parallel"),
        name="attend",
    )(q, k, v)


def _head_blocks(x, mode):
    if mode == "bf16":
        x = x.astype(BF16)
    head = lax.broadcasted_iota(I32, x.shape, 1) >> 6
    return jnp.concatenate([jnp.where(head == h, x, jnp.zeros_like(x)) for h in range(RW_HEADS)], axis=0)


def _wkv_chunks(chains, prec):
    c = WKV_CHUNK
    ps, pi, pa, pst = prec["score"], prec["inverse"], prec["apply"], prec["state"]
    ri = lax.broadcasted_iota(I32, (c, c), 0)
    ci = lax.broadcasted_iota(I32, (c, c), 1)
    row = lax.broadcasted_iota(I32, (c, RW_HEADS * c), 0)
    col = lax.broadcasted_iota(I32, (c, RW_HEADS * c), 1) & (c - 1)
    eye = (col == row).astype(F32)
    masks = {}
    for reverse in sorted({ch[7] for ch in chains}):
        masks[reverse] = (((ci >= ri) if reverse else (ci <= ri)).astype(BF16),
                          (col > row) if reverse else (col < row),
                          (col >= row) if reverse else (col <= row))

    pre = []
    for r, v, kk, lw, kd, bd, s, reverse in chains:
        tri = masks[reverse][0]
        lw_hi, lw_lo = _split_bf16(lw)
        lw_mid, lw_lo = _split_bf16(lw - lw_hi.astype(F32))
        cum = _dot(tri, lw_hi) + (_dot(tri, lw_mid) + _dot(tri, lw_lo))
        total = jnp.sum(lw, axis=0, keepdims=True)
        inv = jnp.exp(-cum)
        rem = jnp.exp(total - cum)
        pre.append(dict(a_t=-kk * jnp.exp(cum - lw), r_t=r * jnp.exp(cum), b_t=bd * inv, k_t=kd * inv,
                        b_end=bd * rem, k_end=kd * rem, decay=jnp.exp(total)))

    scores = []
    for ch, q in zip(chains, pre):
        _, strict, incl = masks[ch[7]]
        ar = jnp.concatenate([q["a_t"], q["r_t"]], axis=0)
        sb = _contract(ar, _head_blocks(q["b_t"], ps), _NT, ps)
        sk = _contract(ar, _head_blocks(q["k_t"], ps), _NT, ps)
        scores.append(dict(a_ab=jnp.where(strict, sb[:c], 0.0), a_rb=jnp.where(incl, sb[c:], 0.0),
                           a_ak=jnp.where(strict, sk[:c], 0.0), a_rk=jnp.where(incl, sk[c:], 0.0)))

    t_inv = [eye + sc["a_ab"] for sc in scores]
    m = [sc["a_ab"] for sc in scores]
    m_blocks = [_head_blocks(x, pi) for x in m]
    for _ in range((c - 1).bit_length() - 1):
        m = [_contract(x, xb, _NN, pi) for x, xb in zip(m, m_blocks)]
        m_blocks = [_head_blocks(x, pi) for x in m]
        t_inv = [t + _contract(t, xb, _NN, pi) for t, xb in zip(t_inv, m_blocks)]

    av = [_contract(jnp.concatenate([sc["a_ak"], sc["a_rk"]], axis=0), _head_blocks(ch[1], pa), _NN, pa)
          for ch, sc in zip(chains, scores)]
    w1 = [_contract(t, _head_blocks(q["a_t"], pa), _NN, pa) for t, q in zip(t_inv, pre)]
    w2 = [_contract(t, _head_blocks(x[:c], pa), _NN, pa) for t, x in zip(t_inv, av)]
    g = [_contract(jnp.concatenate([x, q["r_t"]], axis=0), ch[6], _NT, pst) for x, q, ch in zip(w1, pre, chains)]
    u = [x[:c] + y for x, y in zip(g, w2)]
    ys = [x[c:] + _contract(sc["a_rb"], _head_blocks(uu, pa), _NN, pa) + y[c:]
          for x, sc, uu, y in zip(g, scores, u, av)]
    upd = [_contract(jnp.concatenate([uu, ch[1]], axis=0), jnp.concatenate([q["b_end"], q["k_end"]], axis=0), _TN, pst)
           for uu, ch, q in zip(u, chains, pre)]
    hr = lax.broadcasted_iota(I32, (RW, RW), 0) >> 6
    hc = lax.broadcasted_iota(I32, (RW, RW), 1) >> 6
    s_new = [ch[6] * q["decay"] + jnp.where(hr == hc, x, 0.0) for ch, q, x in zip(chains, pre, upd)]
    return ys, s_new


def _wkv_kernel(rf_ref, vf_ref, kkf_ref, lwf_ref, kdf_ref, bdf_ref, rb_ref, vb_ref, kkb_ref, lwb_ref, kdb_ref,
                bdb_ref, s0_ref, yf_ref, yb_ref, s_out_ref, s_scr, *, prec):
    step = pl.program_id(0)

    @pl.when(step == 0)
    def _():
        s_scr[...] = s0_ref[...]

    bsz = rf_ref.shape[0]
    chains = []
    for b in range(bsz):
        chains.append((rf_ref[b], vf_ref[b], kkf_ref[b], lwf_ref[b], kdf_ref[b], bdf_ref[b], s_scr[0, b], False))
        chains.append((rb_ref[b], vb_ref[b], kkb_ref[b], lwb_ref[b], kdb_ref[b], bdb_ref[b], s_scr[1, b], True))
    ys, s_new = _wkv_chunks(chains, prec)
    for b in range(bsz):
        yf_ref[b] = ys[2 * b]
        yb_ref[b] = ys[2 * b + 1]
        s_scr[0, b] = s_new[2 * b]
        s_scr[1, b] = s_new[2 * b + 1]

    @pl.when(step == pl.num_programs(0) - 1)
    def _():
        s_out_ref[...] = s_scr[...]


def _wkv(r, v, kk, lw, kd, bd, s0, prec=None):
    prec = prec or WKV_PRECISION
    bsz, n, _ = r.shape
    steps = n // WKV_CHUNK
    blk = (bsz, WKV_CHUNK, RW)
    fwd = pl.BlockSpec(blk, lambda i: (0, i, 0))
    bwd = pl.BlockSpec(blk, lambda i: (0, steps - 1 - i, 0))
    bwd_dir = pl.BlockSpec(blk, lambda i: (0, steps - 1 - i, 1))
    state = pl.BlockSpec((2, bsz, RW, RW), lambda i: (0, 0, 0, 0))
    y_shape = jax.ShapeDtypeStruct((bsz, n, RW), F32)
    return pl.pallas_call(
        functools.partial(_wkv_kernel, prec=prec),
        grid=(steps,),
        in_specs=[fwd, fwd, fwd, fwd, fwd, fwd, bwd, bwd, bwd, bwd_dir, bwd_dir, bwd_dir, state],
        out_specs=[fwd, bwd, state],
        out_shape=[y_shape, y_shape, jax.ShapeDtypeStruct((2, bsz, RW, RW), F32)],
        scratch_shapes=[pltpu.VMEM((2, bsz, RW, RW), F32)],
        compiler_params=_cparams("arbitrary"),
        name="wkv",
    )(r, v, kk, lw, kd, bd, r, v, kk, lw, kd, bd, s0)


def _outproj_kernel(x_ref, conv_ref, att_ref, yf_ref, yb_ref, r_ref, v_ref, g_ref, kd_ref,
                    gt1_ref, sh2_ref, sc2_ref, g2_ref, w_ref, rk_ref, lnw_ref, lnb_ref, bd_ref, rwt_ref,
                    x1_ref, h2_ref, aff_ref):
    ones_bd = bd_ref[...]
    y = yf_ref[0] + yb_ref[0]
    mean = _group_sum(y, ones_bd) * (1.0 / HEAD)
    yc = y - mean
    var = _group_sum(yc * yc, ones_bd) * (1.0 / HEAD)
    yn = yc * lax.rsqrt(var + GN_EPS)
    kd = kd_ref[0]
    k_mean = 0.5 * (kd[:, :RW] + kd[:, RW:])
    bonus = _group_sum(r_ref[0] * k_mean * rk_ref[...], ones_bd) * v_ref[0]
    rwkv = (yn * lnw_ref[...] + lnb_ref[...] + bonus) * g_ref[0]

    mixed = (_dot(conv_ref[0].astype(BF16), w_ref[:CONV_W])
             + _dot(att_ref[0], w_ref[CONV_W:CONV_W + Q_COLS])
             + _dot(rwkv.astype(BF16), w_ref[CONV_W + Q_COLS:]))
    x1 = x_ref[0] + gt1_ref[0] * mixed
    x1_ref[0] = x1

    h2 = _modulated_norm(x1, g2_ref[...], sh2_ref[0], sc2_ref[0])
    h2_ref[0] = h2
    logits = _dot_nt(rwt_ref[...], h2, HIGHEST)
    e = jnp.exp(logits - jnp.max(logits, axis=0, keepdims=True))
    aff_ref[0] = e / jnp.sum(e, axis=0, keepdims=True)


def _outproj(x, conv, att, yf, yb, r, v, g, kd, gt1, sh2, sc2, g2, w_out, rk, lnw, lnb, ones_rw, router_t, *, tm):
    bsz, n, d = x.shape
    mod_map = (lambda b, i: (b, 0, 0)) if gt1.shape[0] == bsz else (lambda b, i: (0, 0, 0))
    tile = lambda w: pl.BlockSpec((1, tm, w), lambda b, i: (b, i, 0))
    mod = pl.BlockSpec((1, 1, d), mod_map)
    full = lambda a: pl.BlockSpec(a.shape, lambda b, i: (0,) * a.ndim)
    params = (g2, w_out, rk, lnw, lnb, ones_rw, router_t)
    return pl.pallas_call(
        _outproj_kernel,
        grid=(bsz, n // tm),
        in_specs=[tile(d), tile(CONV_W), tile(Q_COLS), tile(RW), tile(RW), tile(RW), tile(RW), tile(RW),
                  tile(2 * RW), mod, mod, mod] + [full(a) for a in params],
        out_specs=[tile(d), tile(d), pl.BlockSpec((1, N_EXPERTS, tm), lambda b, i: (b, 0, i))],
        out_shape=[
            jax.ShapeDtypeStruct((bsz, n, d), F32),
            jax.ShapeDtypeStruct((bsz, n, d), F32),
            jax.ShapeDtypeStruct((bsz, N_EXPERTS, n), F32),
        ],
        compiler_params=_cparams("parallel", "parallel"),
        name="outproj",
    )(x, conv, att, yf, yb, r, v, g, kd, gt1, sh2, sc2, *params)


def _cumsum_lanes(x):
    n = x.shape[1]
    w = min(n, 256)
    ri = lax.broadcasted_iota(I32, (w, w), 0)
    ci = lax.broadcasted_iota(I32, (w, w), 1)
    tri = (ri <= ci).astype(BF16)
    carry = jnp.zeros((x.shape[0], 1), F32)
    parts, carries = [], [carry]
    for j in range(n // w):
        cs = _dot(x[:, j * w:(j + 1) * w].astype(BF16), tri) + carry
        carry = cs[:, w - 1:w]
        parts.append(cs)
        carries.append(carry)
    return (parts[0] if len(parts) == 1 else jnp.concatenate(parts, axis=1)), carries


def _to_token_major(x):
    n = x.shape[1]
    pad = jnp.zeros((LANES - x.shape[0], LANES), F32)
    blocks = [jnp.concatenate([x[:, j * LANES:(j + 1) * LANES], pad], axis=0).T for j in range(n // LANES)]
    return blocks[0] if len(blocks) == 1 else jnp.concatenate(blocks, axis=0)


def _route_kernel(aff_ref, idx_ref, aff_t_ref, *, cap, slots):
    aff = aff_ref[0]
    n_exp, n = aff.shape
    bits = pltpu.bitcast(aff, I32)

    def search(i, thr):
        cand = thr | (1 << (30 - i))
        cnt = jnp.sum((bits >= cand).astype(F32), axis=1, keepdims=True)
        return jnp.where(cnt >= cap, cand, thr)

    thr = lax.fori_loop(0, 31, search, jnp.zeros((aff.shape[0], 1), I32))
    above = bits > thr
    tied = bits == thr
    need = cap - jnp.sum(above.astype(F32), axis=1, keepdims=True)
    tied_f = tied.astype(F32)
    chosen = jnp.where(above | (tied & (_cumsum_lanes(tied_f)[0] <= need)), 1.0, 0.0)
    rank = _cumsum_lanes(chosen)[0]
    aff_t_ref[0] = _to_token_major(aff)
    rank_t = _to_token_major(rank)
    rows = min(n, 512)
    slot = lax.broadcasted_iota(I32, (rows, slots), 1).astype(F32)
    expert = lax.broadcasted_iota(I32, (n_exp, slots), 0)
    idx = jnp.zeros((n_exp, slots), F32)
    for e in range(n_exp):
        cnt = jnp.zeros((1, slots), F32)
        for r0 in range(0, n, rows):
            cnt = cnt + jnp.sum(jnp.where(rank_t[r0:r0 + rows, e:e + 1] <= slot, 1.0, 0.0), axis=0, keepdims=True)
        idx = jnp.where(expert == e, cnt, idx)
    idx_ref[0] = jnp.minimum(idx, n - 1.0).astype(I32)


def _route(aff, *, cap, slots):
    bsz, e, n = aff.shape
    return pl.pallas_call(
        functools.partial(_route_kernel, cap=cap, slots=slots),
        grid=(bsz,),
        in_specs=[pl.BlockSpec((1, e, n), lambda b: (b, 0, 0))],
        out_specs=[pl.BlockSpec((1, e, slots), lambda b: (b, 0, 0)),
                   pl.BlockSpec((1, n, LANES), lambda b: (b, 0, 0))],
        out_shape=[jax.ShapeDtypeStruct((bsz, e, slots), I32),
                   jax.ShapeDtypeStruct((bsz, n, LANES), F32)],
        compiler_params=_cparams("parallel"),
        name="route",
    )(aff)


ROW_UNROLL = 8


def _gather_kernel(idx_ref, h_ref, aff_ref, xs_ref, gs_ref):
    def rows(c, carry):
        for u in range(ROW_UNROLL):
            j = c * ROW_UNROLL + u
            i = idx_ref[0, 0, j]
            xs_ref[j] = h_ref[i]
            gs_ref[j] = aff_ref[i]
        return carry

    lax.fori_loop(0, xs_ref.shape[0] // ROW_UNROLL, rows, 0)


def _gather(idx, h2, aff_t, *, n):
    d = h2.shape[-1]
    be, _, slots = idx.shape
    bsz = h2.shape[0] // n
    e = be // bsz
    return pl.pallas_call(
        _gather_kernel,
        grid=(bsz, e),
        in_specs=[pl.BlockSpec((1, 1, slots), lambda b, j: (b * e + j, 0, 0), memory_space=pltpu.SMEM),
                  pl.BlockSpec((n, 1, d), lambda b, j: (b, 0, 0)),
                  pl.BlockSpec((n, 1, LANES), lambda b, j: (b, 0, 0))],
        out_specs=[pl.BlockSpec((slots, 1, d), lambda b, j: (b * e + j, 0, 0)),
                   pl.BlockSpec((slots, 1, LANES), lambda b, j: (b * e + j, 0, 0))],
        out_shape=[jax.ShapeDtypeStruct((be * slots, 1, d), F32),
                   jax.ShapeDtypeStruct((be * slots, 1, LANES), F32)],
        compiler_params=_cparams("parallel", "arbitrary"),
        name="gather",
    )(idx, h2, aff_t)


def _experts_kernel(*refs, streams):
    xs_refs, gs_refs = refs[:streams], refs[streams:2 * streams]
    wg_ref, wu_ref, wd_ref = refs[2 * streams:2 * streams + 3]
    y_refs, (wg_s, wu_s, wd_s) = refs[2 * streams + 3:3 * streams + 3], refs[3 * streams + 3:]
    e = pl.program_id(0)

    @pl.when(pl.program_id(1) == 0)
    def _():
        wg_s[...] = wg_ref[0, 0].astype(BF16)
        wu_s[...] = wu_ref[0, 0].astype(BF16)
        wd_s[...] = wd_ref[0, 0].astype(BF16)

    for xs_ref, gs_ref, y_ref in zip(xs_refs, gs_refs, y_refs):
        xs = xs_ref[...].astype(BF16)
        hg = _dot(xs, wg_s[...])
        hu = _dot(xs, wu_s[...])
        hid = (hg * _sigmoid(hg)) * hu
        gs = gs_ref[...]
        lane = lax.broadcasted_iota(I32, gs.shape, 1)
        gate = jnp.sum(jnp.where(lane == e, gs, 0.0), axis=1, keepdims=True)
        y_ref[...] = _dot(hid.astype(BF16), wd_s[...]) * gate


def _experts(streams, w_gate, w_up, w_down, layer, *, bsz):
    e, d, f = w_gate.shape[1], w_gate.shape[2], w_gate.shape[3]
    slot_counts = [xs.shape[0] // (bsz * e) for xs, _ in streams]
    rows = lambda s, w: pl.BlockSpec((s, w), lambda j, b: (b * e + j, 0))
    return pl.pallas_call(
        functools.partial(_experts_kernel, streams=len(streams)),
        grid=(e, bsz),
        in_specs=[rows(s, d) for s in slot_counts] + [rows(s, LANES) for s in slot_counts] + [
            pl.BlockSpec((1, 1, d, f), lambda j, b: (layer, j, 0, 0)),
            pl.BlockSpec((1, 1, d, f), lambda j, b: (layer, j, 0, 0)),
            pl.BlockSpec((1, 1, f, d), lambda j, b: (layer, j, 0, 0))],
        out_specs=[rows(s, d) for s in slot_counts],
        out_shape=[jax.ShapeDtypeStruct(xs.shape, F32) for xs, _ in streams],
        scratch_shapes=[pltpu.VMEM((d, f), BF16), pltpu.VMEM((d, f), BF16), pltpu.VMEM((f, d), BF16)],
        compiler_params=_cparams("parallel", "arbitrary"),
        name="experts",
    )(*[xs for xs, _ in streams], *[gs for _, gs in streams], w_gate, w_up, w_down)


def _combine_kernel(idx_ref, y_ref, x_ref, gt_ref, o_ref, *, cap):
    e = pl.program_id(1)

    @pl.when(e == 0)
    def _():
        o_ref[...] = jnp.zeros_like(o_ref)

    def rows(c, carry):
        js = [c * ROW_UNROLL + u for u in range(ROW_UNROLL)]
        tokens = [idx_ref[0, 0, j] for j in js]
        sums = [o_ref[i] + y_ref[j] for i, j in zip(tokens, js)]
        for i, s in zip(tokens, sums):
            o_ref[i] = s
        return carry

    lax.fori_loop(0, cap // ROW_UNROLL, rows, 0)

    @pl.when(e == pl.num_programs(1) - 1)
    def _():
        o_ref[...] = x_ref[...] + gt_ref[...] * o_ref[...]


def _combine(idx, y, x1, gt2, *, n, cap):
    d = x1.shape[-1]
    be, _, slots = idx.shape
    bsz = x1.shape[0] // n
    e = be // bsz
    mod_map = (lambda b, j: (b, 0, 0)) if gt2.shape[0] == bsz else (lambda b, j: (0, 0, 0))
    return pl.pallas_call(
        functools.partial(_combine_kernel, cap=cap),
        grid=(bsz, e),
        in_specs=[pl.BlockSpec((1, 1, slots), lambda b, j: (b * e + j, 0, 0), memory_space=pltpu.SMEM),
                  pl.BlockSpec((slots, 1, d), lambda b, j: (b * e + j, 0, 0)),
                  pl.BlockSpec((n, 1, d), lambda b, j: (b, 0, 0), pipeline_mode=pl.Buffered(1)),
                  pl.BlockSpec((1, 1, d), mod_map)],
        out_specs=pl.BlockSpec((n, 1, d), lambda b, j: (b, 0, 0)),
        out_shape=jax.ShapeDtypeStruct(x1.shape, F32),
        compiler_params=_cparams("parallel", "arbitrary"),
        name="combine",
    )(idx, y, x1, gt2)


def _rope_tables(n):
    rows = n // GRID_W
    row = jnp.repeat(jnp.arange(rows, dtype=F32), GRID_W)
    col = jnp.tile(jnp.arange(GRID_W, dtype=F32), rows)
    half = HEAD // 2
    inv = ROPE_THETA ** (-jnp.arange(0, half, 2, dtype=F32) / half)
    ang = jnp.concatenate([row[:, None] * inv, col[:, None] * inv], axis=-1)
    cos, sin = jnp.cos(ang), jnp.sin(ang)
    cos = jnp.concatenate([cos, cos], axis=-1)
    sin = jnp.concatenate([-sin, sin], axis=-1)
    return jnp.tile(cos, (1, LANES // HEAD)), jnp.tile(sin, (1, LANES // HEAD))


def _block_diag2(m):
    z = jnp.zeros_like(m[0])
    return jnp.concatenate([jnp.concatenate([m[0], z], axis=1), jnp.concatenate([z, m[1]], axis=1)], axis=0)


def _permute_q_heads(w, axis):
    idx = jnp.concatenate([jnp.arange(h * HEAD, (h + 1) * HEAD) for h in Q_HEAD_ORDER])
    return jnp.take(w, idx, axis=axis)


def _layer_params(l, w_in, w_out, conv_w, q_norm_g, k_norm_g, rw_mu, rw_w0, rw_w_b, rw_a0, rw_a_b, rw_g_b,
                  rw_k_k, rw_k_a, rw_r_k, rw_ln_w, rw_ln_b, router_w, norm1_g, norm2_g):
    wi = w_in[l]
    q_cols = _permute_q_heads(wi[:, CONV_COLS:CONV_COLS + Q_COLS], 1)
    wi = jnp.concatenate([wi[:, :CONV_COLS], q_cols, wi[:, CONV_COLS + Q_COLS:],
                          jnp.zeros((wi.shape[0], RW_COLS_PAD - RW_COLS), F32)], axis=1).astype(BF16)
    wo = w_out[l]
    wo = jnp.concatenate([wo[:CONV_W], _permute_q_heads(wo[CONV_W:CONV_W + Q_COLS], 0),
                          wo[CONV_W + Q_COLS:]], axis=0).astype(BF16)
    row = lambda a: a.reshape(1, -1)
    return dict(
        w_in=wi, w_out=wo, conv_w=conv_w[l],
        qg=row(jnp.tile(q_norm_g[l], N_Q_HEADS)), kg=row(jnp.tile(k_norm_g[l], N_KV_HEADS)),
        mu=row(jnp.pad(rw_mu[l], (0, RW_COLS_PAD - RW_COLS))),
        w0=row(rw_w0[l]), wb=_block_diag2(rw_w_b[l]), a0=row(rw_a0[l]), ab=_block_diag2(rw_a_b[l]),
        gb=jnp.pad(rw_g_b[l], ((0, GATE_LORA_PAD - GATE_LORA), (0, 0))),
        kk=row(rw_k_k[l]), ka=row(rw_k_a[l]), rk=row(rw_r_k[l]), lnw=row(rw_ln_w[l]), lnb=row(rw_ln_b[l]),
        router_t=router_w[l].T, g1=row(norm1_g[l]), g2=row(norm2_g[l]),
    )


def _moe(streams, w_gate, w_up, w_down, layer):
    bsz, _, d = streams[0][0].shape
    routed = []
    for x1, h2, aff, _ in streams:
        n = x1.shape[1]
        cap = CAPACITY_FACTOR * n // N_EXPERTS
        slots = -(-cap // LANES) * LANES
        idx, aff_t = _route(aff, cap=cap, slots=slots)
        idx = idx.reshape(bsz * N_EXPERTS, 1, slots)
        xs, gs = _gather(idx, h2.reshape(bsz * n, 1, d), aff_t.reshape(bsz * n, 1, LANES), n=n)
        routed.append((idx, n, cap, xs.reshape(-1, d), gs.reshape(-1, LANES)))
    ys = _experts([(r[3], r[4]) for r in routed], w_gate, w_up, w_down, layer, bsz=bsz)
    return [_combine(idx, y.reshape(-1, 1, d), x1.reshape(bsz * n, 1, d), gt2, n=n, cap=cap).reshape(bsz, n, d)
            for (x1, _, _, gt2), (idx, n, cap, _, _), y in zip(streams, routed, ys)]


def kernel(x, c, ctx, c_ctx, ada_w, ada_b, norm1_g, norm2_g, w_in, w_out, conv_w, q_norm_g, k_norm_g, rw_mu,
           rw_w0, rw_w_b, rw_a0, rw_a_b, rw_g_b, rw_k_k, rw_k_a, rw_r_k, rw_ln_w, rw_ln_b, router_w,
           exp_w_gate, exp_w_up, exp_w_down):
    bsz, n, d = x.shape
    n_ctx = ctx.shape[1]
    depth = ada_w.shape[0]
    tm = min(512, n)
    tm_c = min(512, n_ctx)
    tq = min(256, n)
    tq_c = min(256, n_ctx)

    rows = -(-(bsz + 1) // SUBLANES) * SUBLANES
    cvecs = jnp.concatenate([c, c_ctx[None], jnp.zeros((rows - bsz - 1, d), F32)], axis=0)
    mod = _adaln(cvecs, ada_w, ada_b).reshape(depth, rows, N_MOD, d)

    cos, sin = _rope_tables(n)
    cos_c = jnp.ones((n_ctx, LANES), F32)
    sin_c = jnp.zeros((n_ctx, LANES), F32)
    ones_q = _block_ones(Q_COLS)
    ones_rw = _block_ones(RW)
    zero_state = jnp.zeros((2, bsz, RW, RW), F32)

    xc = ctx
    for l in range(depth):
        update_ctx = l < depth - 1
        p = _layer_params(l, w_in, w_out, conv_w, q_norm_g, k_norm_g, rw_mu, rw_w0, rw_w_b, rw_a0, rw_a_b, rw_g_b,
                          rw_k_k, rw_k_a, rw_r_k, rw_ln_w, rw_ln_b, router_w, norm1_g, norm2_g)
        lat = [mod[l, :bsz, i][:, None, :] for i in range(N_MOD)]
        cmod = [mod[l, bsz:bsz + 1, i][:, None, :] for i in range(N_MOD)]

        pc_l, q_l, k_l, v_l, rw_l = _inproj(x, lat[0], lat[1], p["g1"], p["w_in"], ones_q, p["qg"], p["kg"],
                                            cos, sin, rope=True, tm=tm)
        pc_c, q_c, k_c, v_c, rw_c = _inproj(xc, cmod[0], cmod[1], p["g1"], p["w_in"], ones_q, p["qg"], p["kg"],
                                            cos_c, sin_c, rope=False, tm=tm_c)
        feat_args = (p["conv_w"], p["mu"], p["w0"], p["wb"], p["a0"], p["ab"], p["gb"], p["kk"], p["ka"], ones_rw)
        conv_l, r_l, vv_l, kk_l, g_l, lw_l, kd_l, bd_l = _features(pc_l, rw_l, *feat_args, tm=min(256, n))
        conv_c, r_c, vv_c, kk_c, g_c, lw_c, kd_c, bd_c = _features(pc_c, rw_c, *feat_args, tm=min(256, n_ctx))

        att_l = _attend(q_l, jnp.concatenate([k_l, k_c], axis=1), jnp.concatenate([v_l, v_c], axis=1), tq=tq)

        yf_c, yb_c, s_c = _wkv(r_c, vv_c, kk_c, lw_c, kd_c, bd_c, zero_state)
        yf_l, yb_l, _ = _wkv(r_l, vv_l, kk_l, lw_l, kd_l, bd_l, s_c)
        ys_l, ys_c = (yf_l, yb_l), (yf_c, yb_c)

        out_args = (p["g2"], p["w_out"], p["rk"], p["lnw"], p["lnb"], ones_rw, p["router_t"])
        x1, h2, aff = _outproj(x, conv_l, att_l, ys_l[0], ys_l[1], r_l, vv_l, g_l, kd_l,
                               lat[2], lat[3], lat[4], *out_args, tm=tm)
        streams = [(x1, h2, aff, lat[5])]
        if update_ctx:
            att_c = _attend(q_c, k_c, v_c, tq=tq_c)
            xc1, hc2, aff_c = _outproj(xc, conv_c, att_c, ys_c[0], ys_c[1], r_c, vv_c, g_c, kd_c,
                                       cmod[2], cmod[3], cmod[4], *out_args, tm=tm_c)
            streams.append((xc1, hc2, aff_c, cmod[5]))
        outs = _moe(streams, exp_w_gate, exp_w_up, exp_w_down, l)
        x = outs[0]
        if update_ctx:
            xc = outs[1]
    return x
```

```python
import functools

import jax
import jax.numpy as jnp
from jax import lax
from jax.experimental import pallas as pl
from jax.experimental.pallas import tpu as pltpu

F32 = jnp.float32
BF16 = jnp.bfloat16
I32 = jnp.int32
HIGHEST = lax.Precision.HIGHEST

HEAD = 64
N_Q_HEADS = 8
N_KV_HEADS = 2
CONV_W = 256
RW = 256
RW_HEADS = RW // HEAD
DECAY_LORA = 64
AAA_LORA = 64
GATE_LORA = 160
N_MOD = 6
N_EXPERTS = 16
CAPACITY_FACTOR = 2
GRID_W = 64
ROPE_THETA = 10000.0
NORM_EPS = 1e-6
GN_EPS = 64e-5
Q_SCALE = 1.4426950408889634 * HEAD ** -0.5

CONV_COLS = 3 * CONV_W
Q_COLS = N_Q_HEADS * HEAD
KV_COLS = N_KV_HEADS * HEAD
ATT_COLS = Q_COLS + 2 * KV_COLS
RW_COLS = 3 * RW + 2 * (DECAY_LORA + AAA_LORA) + GATE_LORA
LANES = 128
SUBLANES = 8
RW_COLS_PAD = -(-RW_COLS // LANES) * LANES
GATE_LORA_PAD = RW_COLS_PAD - (3 * RW + 2 * (DECAY_LORA + AAA_LORA))
PROJ_PAD = CONV_COLS + ATT_COLS + RW_COLS_PAD

WKV_CHUNK = 64
VMEM_LIMIT = 56 * 1024 * 1024
WKV_PRECISION = dict(score="bf16", inverse="bf16", apply="bf16", state="bf16")

Q_HEAD_ORDER = (0, 4, 1, 5, 2, 6, 3, 7)


def _cparams(*sem):
    return pltpu.CompilerParams(dimension_semantics=sem, vmem_limit_bytes=VMEM_LIMIT)


def _dot(a, b, precision=None):
    return jnp.dot(a, b, preferred_element_type=F32, precision=precision)


def _dot_nt(a, b, precision=None):
    return lax.dot_general(a, b, (((1,), (1,)), ((), ())), preferred_element_type=F32, precision=precision)


def _dot_tn(a, b, precision=None):
    return lax.dot_general(a, b, (((0,), (0,)), ((), ())), preferred_element_type=F32, precision=precision)


def _split_bf16(x):
    hi = x.astype(BF16)
    return hi, (x - hi.astype(F32)).astype(BF16)


_NN = ((1,), (0,))
_NT = ((1,), (1,))
_TN = ((0,), (0,))


def _contract(a, b, dims, mode):
    dn = (dims, ((), ()))
    if mode == "f32":
        return lax.dot_general(a, b, dn, preferred_element_type=F32, precision=HIGHEST)
    if mode == "bf16":
        return lax.dot_general(a.astype(BF16), b.astype(BF16), dn, preferred_element_type=F32)
    ah, al = _split_bf16(a)
    bh, bl = _split_bf16(b)
    dot = lambda x, y: lax.dot_general(x, y, dn, preferred_element_type=F32)
    return dot(ah, bh) + (dot(ah, bl) + dot(al, bh))


def _sigmoid(x):
    return 1.0 / (1.0 + jnp.exp(-x))


def _group_sum(t, ones_bd):
    hi = t.astype(BF16)
    lo = (t - hi.astype(F32)).astype(BF16)
    return _dot(hi, ones_bd) + _dot(lo, ones_bd)


def _block_ones(width):
    g = jnp.arange(width) // HEAD
    return (g[:, None] == g[None, :]).astype(BF16)


def _adaln_kernel(c_ref, w_ref, b_ref, o_ref):
    c = c_ref[...]
    s = c * _sigmoid(c)
    o_ref[0] = _dot(s, w_ref[0], HIGHEST) + b_ref[0]


def _adaln(cvecs, ada_w, ada_b):
    depth, d, n = ada_w.shape
    r = cvecs.shape[0]
    tn = 1024
    return pl.pallas_call(
        _adaln_kernel,
        grid=(depth, n // tn),
        in_specs=[
            pl.BlockSpec((r, d), lambda l, j: (0, 0)),
            pl.BlockSpec((1, d, tn), lambda l, j: (l, 0, j)),
            pl.BlockSpec((1, 1, tn), lambda l, j: (l, 0, j)),
        ],
        out_specs=pl.BlockSpec((1, r, tn), lambda l, j: (l, 0, j)),
        out_shape=jax.ShapeDtypeStruct((depth, r, n), F32),
        compiler_params=_cparams("parallel", "parallel"),
        name="adaln",
    )(cvecs, ada_w, ada_b.reshape(depth, 1, n))


def _modulated_norm(x, g, shift, scale):
    ms = jnp.mean(x * x, axis=-1, keepdims=True)
    return (x * lax.rsqrt(ms + NORM_EPS) * g) * (1.0 + scale) + shift


def _head_norm(t, ones_bd, gain):
    ss = _group_sum(t * t, ones_bd)
    return t * lax.rsqrt(ss * (1.0 / HEAD) + NORM_EPS) * gain


def _rope(t, cos, sin_signed):
    width = t.shape[1]
    lane = lax.broadcasted_iota(I32, t.shape, 1)
    first_half = (lane & (HEAD - 1)) < HEAD // 2
    swapped = jnp.where(first_half, pltpu.roll(t, width - HEAD // 2, 1), pltpu.roll(t, HEAD // 2, 1))
    reps = width // LANES
    if reps > 1:
        cos = jnp.concatenate([cos] * reps, axis=1)
        sin_signed = jnp.concatenate([sin_signed] * reps, axis=1)
    return t * cos + swapped * sin_signed


def _inproj_kernel(x_ref, sh_ref, sc_ref, g_ref, w_ref, bd_ref, qg_ref, kg_ref, cos_ref, sin_ref,
                   pc_ref, q_ref, k_ref, v_ref, rw_ref, *, rope):
    h = _modulated_norm(x_ref[0], g_ref[...], sh_ref[0], sc_ref[0])
    p = _dot(h.astype(BF16), w_ref[...])
    pc_ref[0] = p[:, :CONV_COLS]
    q = p[:, CONV_COLS:CONV_COLS + Q_COLS]
    k = p[:, CONV_COLS + Q_COLS:CONV_COLS + Q_COLS + KV_COLS]
    v = p[:, CONV_COLS + Q_COLS + KV_COLS:CONV_COLS + ATT_COLS]
    q = _head_norm(q, bd_ref[...], qg_ref[...])
    k = _head_norm(k, bd_ref[:KV_COLS, :KV_COLS], kg_ref[...])
    if rope:
        q = _rope(q, cos_ref[...], sin_ref[...])
        k = _rope(k, cos_ref[...], sin_ref[...])
    q_ref[0] = (q * Q_SCALE).astype(BF16)
    k_ref[0] = k.astype(BF16)
    v_ref[0] = v.astype(BF16)
    rw_ref[0] = p[:, CONV_COLS + ATT_COLS:]


def _inproj(x, shift, scale, g, w_in, ones_q, qg, kg, cos, sin, *, rope, tm):
    bsz, n, d = x.shape
    mod_map = (lambda b, i: (b, 0, 0)) if shift.shape[0] == bsz else (lambda b, i: (0, 0, 0))
    const2 = lambda b, i: (0, 0)
    tile = lambda w: pl.BlockSpec((1, tm, w), lambda b, i: (b, i, 0))
    return pl.pallas_call(
        functools.partial(_inproj_kernel, rope=rope),
        grid=(bsz, n // tm),
        in_specs=[
            tile(d),
            pl.BlockSpec((1, 1, d), mod_map),
            pl.BlockSpec((1, 1, d), mod_map),
            pl.BlockSpec((1, d), const2),
            pl.BlockSpec((d, PROJ_PAD), const2),
            pl.BlockSpec((Q_COLS, Q_COLS), const2),
            pl.BlockSpec((1, Q_COLS), const2),
            pl.BlockSpec((1, KV_COLS), const2),
            pl.BlockSpec((tm, LANES), lambda b, i: (i, 0)),
            pl.BlockSpec((tm, LANES), lambda b, i: (i, 0)),
        ],
        out_specs=[tile(CONV_COLS), tile(Q_COLS), tile(KV_COLS), tile(KV_COLS), tile(RW_COLS_PAD)],
        out_shape=[
            jax.ShapeDtypeStruct((bsz, n, CONV_COLS), F32),
            jax.ShapeDtypeStruct((bsz, n, Q_COLS), BF16),
            jax.ShapeDtypeStruct((bsz, n, KV_COLS), BF16),
            jax.ShapeDtypeStruct((bsz, n, KV_COLS), BF16),
            jax.ShapeDtypeStruct((bsz, n, RW_COLS_PAD), F32),
        ],
        compiler_params=_cparams("parallel", "parallel"),
        name="inproj",
    )(x, shift, scale, g, w_in, ones_q, qg, kg, cos, sin)


def _shifted(x, prev_row, next_row):
    rows = lax.broadcasted_iota(I32, x.shape, 0)
    tm = x.shape[0]
    before = jnp.where(rows == 0, prev_row, pltpu.roll(x, 1, 0))
    after = jnp.where(rows == tm - 1, next_row, pltpu.roll(x, tm - 1, 0))
    return before, after


def _features_kernel(pc_ref, pc_prev_ref, pc_next_ref, rw_ref, rw_prev_ref, rw_next_ref,
                     conv_w_ref, mu_ref, w0_ref, wb_ref, a0_ref, ab_ref, gb_ref, kk_ref, ka_ref, bd_ref,
                     conv_ref, r_ref, v_ref, kkn_ref, g_ref, lw_ref, kd_ref, bdir_ref):
    i = pl.program_id(1)
    has_prev = (i > 0).astype(F32)
    has_next = (i < pl.num_programs(1) - 1).astype(F32)

    pc = pc_ref[0]
    pcp = pc_prev_ref[0][SUBLANES - 1:SUBLANES, :] * has_prev
    pcn = pc_next_ref[0][0:1, :] * has_next
    z = pc[:, CONV_W:2 * CONV_W] * pc[:, 2 * CONV_W:]
    zp = pcp[:, CONV_W:2 * CONV_W] * pcp[:, 2 * CONV_W:]
    zn = pcn[:, CONV_W:2 * CONV_W] * pcn[:, 2 * CONV_W:]
    zb, za = _shifted(z, zp, zn)
    cw = conv_w_ref[...]
    conv_ref[0] = pc[:, :CONV_W] * (cw[0:1] * zb + cw[1:2] * z + cw[2:3] * za)

    p = rw_ref[0]
    pb, pa = _shifted(p, rw_prev_ref[0][SUBLANES - 1:SUBLANES, :] * has_prev, rw_next_ref[0][0:1, :] * has_next)
    p = p + mu_ref[...] * (0.5 * (pb + pa) - p)
    r = p[:, :RW]
    k = p[:, RW:2 * RW]
    v = p[:, 2 * RW:3 * RW]
    o = 3 * RW
    wl = p[:, o:o + 2 * DECAY_LORA]
    al = p[:, o + 2 * DECAY_LORA:o + 2 * (DECAY_LORA + AAA_LORA)]
    gl = p[:, o + 2 * (DECAY_LORA + AAA_LORA):]
    wz = -(w0_ref[...] + _dot(jnp.tanh(wl), wb_ref[...], HIGHEST))
    softplus = jnp.maximum(wz, 0.0) + jnp.log(1.0 + jnp.exp(-jnp.abs(wz)))
    lw_ref[0] = -jnp.exp(-softplus - 0.5)
    a = _sigmoid(a0_ref[...] + _dot(al, ab_ref[...], HIGHEST))
    kk = k * kk_ref[...]
    kk = kk * lax.rsqrt(jnp.maximum(_group_sum(kk * kk, bd_ref[...]), 1e-24))
    k2 = jnp.concatenate([k, k], axis=1)
    kk2 = jnp.concatenate([kk, kk], axis=1)
    ka2 = jnp.concatenate([ka_ref[...], ka_ref[...]], axis=1)
    kd_ref[0] = k2 * (1.0 + (a - 1.0) * ka2)
    bdir_ref[0] = kk2 * a
    r_ref[0] = r
    v_ref[0] = v
    kkn_ref[0] = kk
    g_ref[0] = _dot(_sigmoid(gl), gb_ref[...], HIGHEST)


def _features(pc, prw, conv_w, mu, w0, wb, a0, ab, gb, kk, ka, ones_rw, *, tm):
    bsz, n, _ = pc.shape
    nt = n // tm
    r8 = tm // SUBLANES
    tile = lambda w: pl.BlockSpec((1, tm, w), lambda b, i: (b, i, 0))
    prev = lambda w: pl.BlockSpec((1, SUBLANES, w), lambda b, i: (b, jnp.maximum(i * r8 - 1, 0), 0))
    nxt = lambda w: pl.BlockSpec((1, SUBLANES, w), lambda b, i: (b, jnp.minimum((i + 1) * r8, nt * r8 - 1), 0))
    full = lambda a: pl.BlockSpec(a.shape, lambda b, i: (0,) * a.ndim)
    params = (conv_w, mu, w0, wb, a0, ab, gb, kk, ka, ones_rw)
    widths = (CONV_W, RW, RW, RW, RW, 2 * RW, 2 * RW, 2 * RW)
    return pl.pallas_call(
        _features_kernel,
        grid=(bsz, nt),
        in_specs=[tile(CONV_COLS), prev(CONV_COLS), nxt(CONV_COLS),
                  tile(RW_COLS_PAD), prev(RW_COLS_PAD), nxt(RW_COLS_PAD)] + [full(a) for a in params],
        out_specs=[tile(w) for w in widths],
        out_shape=[jax.ShapeDtypeStruct((bsz, n, w), F32) for w in widths],
        compiler_params=_cparams("parallel", "parallel"),
        name="features",
    )(pc, pc, pc, prw, prw, prw, *params)


def _attend_kernel(q_ref, k_ref, v_ref, o_ref):
    k = k_ref[0]
    v = v_ref[0]
    lane = lax.broadcasted_iota(I32, (q_ref.shape[1], LANES), 1)
    low = lane < HEAD
    key_low = lax.broadcasted_iota(I32, v.shape, 1) < HEAD
    ones = jnp.ones_like(v)
    v_sides = {True: jnp.where(key_low, v, ones), False: jnp.where(key_low, ones, v)}
    for c in range(Q_COLS // LANES):
        qc = q_ref[0, :, c * LANES:(c + 1) * LANES]
        outs = []
        for use_low in (True, False):
            qm = jnp.where(low == use_low, qc, jnp.zeros_like(qc))
            s = _dot_nt(qm, k)
            p = jnp.exp2(s - jnp.max(s, axis=-1, keepdims=True))
            o = _dot(p.astype(BF16), v_sides[use_low])
            outs.append(o / pltpu.roll(o, HEAD, 1))
        o_ref[0, :, c * LANES:(c + 1) * LANES] = jnp.where(low, outs[0], outs[1]).astype(BF16)


def _attend(q, k, v, *, tq):
    bsz, n, _ = q.shape
    nk = k.shape[1]
    return pl.pallas_call(
        _attend_kernel,
        grid=(bsz, n // tq),
        in_specs=[
            pl.BlockSpec((1, tq, Q_COLS), lambda b, i: (b, i, 0)),
            pl.BlockSpec((1, nk, KV_COLS), lambda b, i: (b, 0, 0)),
            pl.BlockSpec((1, nk, KV_COLS), lambda b, i: (b, 0, 0)),
        ],
        out_specs=pl.BlockSpec((1, tq, Q_COLS), lambda b, i: (b, i, 0)),
        out_shape=jax.ShapeDtypeStruct((bsz, n, Q_COLS), BF16),
        compiler_params=_cparams("parallel", "parallel"),
        name="attend",
    )(q, k, v)


def _head_blocks(x, mode):
    if mode == "bf16":
        x = x.astype(BF16)
    head = lax.broadcasted_iota(I32, x.shape, 1) >> 6
    return jnp.concatenate([jnp.where(head == h, x, jnp.zeros_like(x)) for h in range(RW_HEADS)], axis=0)


def _wkv_chunks(chains, prec):
    c = WKV_CHUNK
    ps, pi, pa, pst = prec["score"], prec["inverse"], prec["apply"], prec["state"]
    ri = lax.broadcasted_iota(I32, (c, c), 0)
    ci = lax.broadcasted_iota(I32, (c, c), 1)
    row = lax.broadcasted_iota(I32, (c, RW_HEADS * c), 0)
    col = lax.broadcasted_iota(I32, (c, RW_HEADS * c), 1) & (c - 1)
    eye = (col == row).astype(F32)
    masks = {}
    for reverse in sorted({ch[7] for ch in chains}):
        masks[reverse] = (((ci >= ri) if reverse else (ci <= ri)).astype(BF16),
                          (col > row) if reverse else (col < row),
                          (col >= row) if reverse else (col <= row))

    pre = []
    for r, v, kk, lw, kd, bd, s, reverse in chains:
        tri = masks[reverse][0]
        lw_hi, lw_lo = _split_bf16(lw)
        lw_mid, lw_lo = _split_bf16(lw - lw_hi.astype(F32))
        cum = _dot(tri, lw_hi) + (_dot(tri, lw_mid) + _dot(tri, lw_lo))
        total = jnp.sum(lw, axis=0, keepdims=True)
        inv = jnp.exp(-cum)
        rem = jnp.exp(total - cum)
        pre.append(dict(a_t=-kk * jnp.exp(cum - lw), r_t=r * jnp.exp(cum), b_t=bd * inv, k_t=kd * inv,
                        b_end=bd * rem, k_end=kd * rem, decay=jnp.exp(total)))

    scores = []
    for ch, q in zip(chains, pre):
        _, strict, incl = masks[ch[7]]
        ar = jnp.concatenate([q["a_t"], q["r_t"]], axis=0)
        sb = _contract(ar, _head_blocks(q["b_t"], ps), _NT, ps)
        sk = _contract(ar, _head_blocks(q["k_t"], ps), _NT, ps)
        scores.append(dict(a_ab=jnp.where(strict, sb[:c], 0.0), a_rb=jnp.where(incl, sb[c:], 0.0),
                           a_ak=jnp.where(strict, sk[:c], 0.0), a_rk=jnp.where(incl, sk[c:], 0.0)))

    t_inv = [eye + sc["a_ab"] for sc in scores]
    m = [sc["a_ab"] for sc in scores]
    m_blocks = [_head_blocks(x, pi) for x in m]
    for _ in range((c - 1).bit_length() - 1):
        m = [_contract(x, xb, _NN, pi) for x, xb in zip(m, m_blocks)]
        m_blocks = [_head_blocks(x, pi) for x in m]
        t_inv = [t + _contract(t, xb, _NN, pi) for t, xb in zip(t_inv, m_blocks)]

    av = [_contract(jnp.concatenate([sc["a_ak"], sc["a_rk"]], axis=0), _head_blocks(ch[1], pa), _NN, pa)
          for ch, sc in zip(chains, scores)]
    w1 = [_contract(t, _head_blocks(q["a_t"], pa), _NN, pa) for t, q in zip(t_inv, pre)]
    w2 = [_contract(t, _head_blocks(x[:c], pa), _NN, pa) for t, x in zip(t_inv, av)]
    g = [_contract(jnp.concatenate([x, q["r_t"]], axis=0), ch[6], _NT, pst) for x, q, ch in zip(w1, pre, chains)]
    u = [x[:c] + y for x, y in zip(g, w2)]
    ys = [x[c:] + _contract(sc["a_rb"], _head_blocks(uu, pa), _NN, pa) + y[c:]
          for x, sc, uu, y in zip(g, scores, u, av)]
    upd = [_contract(jnp.concatenate([uu, ch[1]], axis=0), jnp.concatenate([q["b_end"], q["k_end"]], axis=0), _TN, pst)
           for uu, ch, q in zip(u, chains, pre)]
    hr = lax.broadcasted_iota(I32, (RW, RW), 0) >> 6
    hc = lax.broadcasted_iota(I32, (RW, RW), 1) >> 6
    s_new = [ch[6] * q["decay"] + jnp.where(hr == hc, x, 0.0) for ch, q, x in zip(chains, pre, upd)]
    return ys, s_new


def _wkv_kernel(rf_ref, vf_ref, kkf_ref, lwf_ref, kdf_ref, bdf_ref, rb_ref, vb_ref, kkb_ref, lwb_ref, kdb_ref,
                bdb_ref, s0_ref, yf_ref, yb_ref, s_out_ref, s_scr, *, prec):
    step = pl.program_id(0)

    @pl.when(step == 0)
    def _():
        s_scr[...] = s0_ref[...]

    bsz = rf_ref.shape[0]
    chains = []
    for b in range(bsz):
        chains.append((rf_ref[b], vf_ref[b], kkf_ref[b], lwf_ref[b], kdf_ref[b], bdf_ref[b], s_scr[0, b], False))
        chains.append((rb_ref[b], vb_ref[b], kkb_ref[b], lwb_ref[b], kdb_ref[b], bdb_ref[b], s_scr[1, b], True))
    ys, s_new = _wkv_chunks(chains, prec)
    for b in range(bsz):
        yf_ref[b] = ys[2 * b]
        yb_ref[b] = ys[2 * b + 1]
        s_scr[0, b] = s_new[2 * b]
        s_scr[1, b] = s_new[2 * b + 1]

    @pl.when(step == pl.num_programs(0) - 1)
    def _():
        s_out_ref[...] = s_scr[...]


def _wkv(r, v, kk, lw, kd, bd, s0, prec=None):
    prec = prec or WKV_PRECISION
    bsz, n, _ = r.shape
    steps = n // WKV_CHUNK
    blk = (bsz, WKV_CHUNK, RW)
    fwd = pl.BlockSpec(blk, lambda i: (0, i, 0))
    bwd = pl.BlockSpec(blk, lambda i: (0, steps - 1 - i, 0))
    bwd_dir = pl.BlockSpec(blk, lambda i: (0, steps - 1 - i, 1))
    state = pl.BlockSpec((2, bsz, RW, RW), lambda i: (0, 0, 0, 0))
    y_shape = jax.ShapeDtypeStruct((bsz, n, RW), F32)
    return pl.pallas_call(
        functools.partial(_wkv_kernel, prec=prec),
        grid=(steps,),
        in_specs=[fwd, fwd, fwd, fwd, fwd, fwd, bwd, bwd, bwd, bwd_dir, bwd_dir, bwd_dir, state],
        out_specs=[fwd, bwd, state],
        out_shape=[y_shape, y_shape, jax.ShapeDtypeStruct((2, bsz, RW, RW), F32)],
        scratch_shapes=[pltpu.VMEM((2, bsz, RW, RW), F32)],
        compiler_params=_cparams("arbitrary"),
        name="wkv",
    )(r, v, kk, lw, kd, bd, r, v, kk, lw, kd, bd, s0)


def _outproj_kernel(x_ref, conv_ref, att_ref, yf_ref, yb_ref, r_ref, v_ref, g_ref, kd_ref,
                    gt1_ref, sh2_ref, sc2_ref, g2_ref, w_ref, rk_ref, lnw_ref, lnb_ref, bd_ref, rwt_ref,
                    x1_ref, h2_ref, aff_ref):
    ones_bd = bd_ref[...]
    y = yf_ref[0] + yb_ref[0]
    mean = _group_sum(y, ones_bd) * (1.0 / HEAD)
    yc = y - mean
    var = _group_sum(yc * yc, ones_bd) * (1.0 / HEAD)
    yn = yc * lax.rsqrt(var + GN_EPS)
    kd = kd_ref[0]
    k_mean = 0.5 * (kd[:, :RW] + kd[:, RW:])
    bonus = _group_sum(r_ref[0] * k_mean * rk_ref[...], ones_bd) * v_ref[0]
    rwkv = (yn * lnw_ref[...] + lnb_ref[...] + bonus) * g_ref[0]

    mixed = (_dot(conv_ref[0].astype(BF16), w_ref[:CONV_W])
             + _dot(att_ref[0], w_ref[CONV_W:CONV_W + Q_COLS])
             + _dot(rwkv.astype(BF16), w_ref[CONV_W + Q_COLS:]))
    x1 = x_ref[0] + gt1_ref[0] * mixed
    x1_ref[0] = x1

    h2 = _modulated_norm(x1, g2_ref[...], sh2_ref[0], sc2_ref[0])
    h2_ref[0] = h2
    logits = _dot_nt(rwt_ref[...], h2, HIGHEST)
    e = jnp.exp(logits - jnp.max(logits, axis=0, keepdims=True))
    aff_ref[0] = e / jnp.sum(e, axis=0, keepdims=True)


def _outproj(x, conv, att, yf, yb, r, v, g, kd, gt1, sh2, sc2, g2, w_out, rk, lnw, lnb, ones_rw, router_t, *, tm):
    bsz, n, d = x.shape
    mod_map = (lambda b, i: (b, 0, 0)) if gt1.shape[0] == bsz else (lambda b, i: (0, 0, 0))
    tile = lambda w: pl.BlockSpec((1, tm, w), lambda b, i: (b, i, 0))
    mod = pl.BlockSpec((1, 1, d), mod_map)
    full = lambda a: pl.BlockSpec(a.shape, lambda b, i: (0,) * a.ndim)
    params = (g2, w_out, rk, lnw, lnb, ones_rw, router_t)
    return pl.pallas_call(
        _outproj_kernel,
        grid=(bsz, n // tm),
        in_specs=[tile(d), tile(CONV_W), tile(Q_COLS), tile(RW), tile(RW), tile(RW), tile(RW), tile(RW),
                  tile(2 * RW), mod, mod, mod] + [full(a) for a in params],
        out_specs=[tile(d), tile(d), pl.BlockSpec((1, N_EXPERTS, tm), lambda b, i: (b, 0, i))],
        out_shape=[
            jax.ShapeDtypeStruct((bsz, n, d), F32),
            jax.ShapeDtypeStruct((bsz, n, d), F32),
            jax.ShapeDtypeStruct((bsz, N_EXPERTS, n), F32),
        ],
        compiler_params=_cparams("parallel", "parallel"),
        name="outproj",
    )(x, conv, att, yf, yb, r, v, g, kd, gt1, sh2, sc2, *params)


def _cumsum_lanes(x):
    n = x.shape[1]
    w = min(n, 256)
    ri = lax.broadcasted_iota(I32, (w, w), 0)
    ci = lax.broadcasted_iota(I32, (w, w), 1)
    tri = (ri <= ci).astype(BF16)
    carry = jnp.zeros((x.shape[0], 1), F32)
    parts = []
    for j in range(n // w):
        cs = _dot(x[:, j * w:(j + 1) * w].astype(BF16), tri) + carry
        carry = cs[:, w - 1:w]
        parts.append(cs)
    return parts[0] if len(parts) == 1 else jnp.concatenate(parts, axis=1)


def _to_token_major(x):
    n = x.shape[1]
    pad = jnp.zeros((LANES - x.shape[0], LANES), F32)
    blocks = [jnp.concatenate([x[:, j * LANES:(j + 1) * LANES], pad], axis=0).T for j in range(n // LANES)]
    return blocks[0] if len(blocks) == 1 else jnp.concatenate(blocks, axis=0)


def _route_kernel(aff_ref, idx_ref, aff_t_ref, *, cap, slots):
    aff = aff_ref[0]
    n_exp, n = aff.shape
    bits = pltpu.bitcast(aff, I32)

    def search(i, thr):
        cand = thr | (1 << (30 - i))
        cnt = jnp.sum((bits >= cand).astype(F32), axis=1, keepdims=True)
        return jnp.where(cnt >= cap, cand, thr)

    thr = lax.fori_loop(0, 31, search, jnp.zeros((aff.shape[0], 1), I32))
    above = bits > thr
    tied = bits == thr
    need = cap - jnp.sum(above.astype(F32), axis=1, keepdims=True)
    tied_f = tied.astype(F32)
    chosen = jnp.where(above | (tied & (_cumsum_lanes(tied_f) <= need)), 1.0, 0.0)
    rank = _cumsum_lanes(chosen)
    aff_t_ref[0] = _to_token_major(aff)
    rank_t = _to_token_major(rank)
    rows = min(n, 512)
    slot = lax.broadcasted_iota(I32, (rows, slots), 1).astype(F32)
    expert = lax.broadcasted_iota(I32, (n_exp, slots), 0)
    idx = jnp.zeros((n_exp, slots), F32)
    for e in range(n_exp):
        cnt = jnp.zeros((1, slots), F32)
        for r0 in range(0, n, rows):
            cnt = cnt + jnp.sum(jnp.where(rank_t[r0:r0 + rows, e:e + 1] <= slot, 1.0, 0.0), axis=0, keepdims=True)
        idx = jnp.where(expert == e, cnt, idx)
    idx_ref[0] = jnp.minimum(idx, n - 1.0).astype(I32)


def _route(aff, *, cap, slots):
    bsz, e, n = aff.shape
    return pl.pallas_call(
        functools.partial(_route_kernel, cap=cap, slots=slots),
        grid=(bsz,),
        in_specs=[pl.BlockSpec((1, e, n), lambda b: (b, 0, 0))],
        out_specs=[pl.BlockSpec((1, e, slots), lambda b: (b, 0, 0)),
                   pl.BlockSpec((1, n, LANES), lambda b: (b, 0, 0))],
        out_shape=[jax.ShapeDtypeStruct((bsz, e, slots), I32),
                   jax.ShapeDtypeStruct((bsz, n, LANES), F32)],
        compiler_params=_cparams("parallel"),
        name="route",
    )(aff)


ROW_GROUP = SUBLANES
RELAYOUT_ROWS = 256


def _gather_kernel(idx_ref, h_ref, aff_ref, xs_ref, gs_ref, h_rows, a_rows):
    n, d = h_ref.shape

    @pl.when(pl.program_id(1) == 0)
    def _():
        for r in range(0, n, RELAYOUT_ROWS):
            m = min(RELAYOUT_ROWS, n - r)
            h_rows[r:r + m] = h_ref[r:r + m, :].reshape(m, 1, d)
            a_rows[r:r + m] = aff_ref[r:r + m, :].reshape(m, 1, LANES)

    def rows(c, carry):
        base = pl.multiple_of(c * ROW_GROUP, ROW_GROUP)
        tokens = [idx_ref[0, 0, c * ROW_GROUP + u] for u in range(ROW_GROUP)]
        xs_ref[pl.ds(base, ROW_GROUP), :] = jnp.concatenate([h_rows[i] for i in tokens], axis=0)
        gs_ref[pl.ds(base, ROW_GROUP), :] = jnp.concatenate([a_rows[i] for i in tokens], axis=0)
        return carry

    lax.fori_loop(0, xs_ref.shape[0] // ROW_GROUP, rows, 0)


def _gather(idx, h2, aff_t):
    bsz, n, d = h2.shape
    be, _, slots = idx.shape
    e = be // bsz
    return pl.pallas_call(
        _gather_kernel,
        grid=(bsz, e),
        in_specs=[pl.BlockSpec((1, 1, slots), lambda b, j: (b * e + j, 0, 0), memory_space=pltpu.SMEM),
                  pl.BlockSpec((None, n, d), lambda b, j: (b, 0, 0), pipeline_mode=pl.Buffered(1)),
                  pl.BlockSpec((None, n, LANES), lambda b, j: (b, 0, 0))],
        out_specs=[pl.BlockSpec((slots, d), lambda b, j: (b * e + j, 0)),
                   pl.BlockSpec((slots, LANES), lambda b, j: (b * e + j, 0))],
        out_shape=[jax.ShapeDtypeStruct((be * slots, d), F32),
                   jax.ShapeDtypeStruct((be * slots, LANES), F32)],
        scratch_shapes=[pltpu.VMEM((n, 1, d), F32), pltpu.VMEM((n, 1, LANES), F32)],
        compiler_params=_cparams("parallel", "arbitrary"),
        name="gather",
    )(idx, h2, aff_t)


def _experts_kernel(*refs, streams):
    xs_refs, gs_refs = refs[:streams], refs[streams:2 * streams]
    wg_ref, wu_ref, wd_ref = refs[2 * streams:2 * streams + 3]
    y_refs, (wg_s, wu_s, wd_s) = refs[2 * streams + 3:3 * streams + 3], refs[3 * streams + 3:]
    e = pl.program_id(0)

    @pl.when(pl.program_id(1) == 0)
    def _():
        wg_s[...] = wg_ref[0, 0].astype(BF16)
        wu_s[...] = wu_ref[0, 0].astype(BF16)
        wd_s[...] = wd_ref[0, 0].astype(BF16)

    for xs_ref, gs_ref, y_ref in zip(xs_refs, gs_refs, y_refs):
        xs = xs_ref[...].astype(BF16)
        hg = _dot(xs, wg_s[...])
        hu = _dot(xs, wu_s[...])
        hid = (hg * _sigmoid(hg)) * hu
        gs = gs_ref[...]
        lane = lax.broadcasted_iota(I32, gs.shape, 1)
        gate = jnp.sum(jnp.where(lane == e, gs, 0.0), axis=1, keepdims=True)
        y_ref[...] = _dot(hid.astype(BF16), wd_s[...]) * gate


def _experts(streams, w_gate, w_up, w_down, layer, *, bsz):
    e, d, f = w_gate.shape[1], w_gate.shape[2], w_gate.shape[3]
    slot_counts = [xs.shape[0] // (bsz * e) for xs, _ in streams]
    rows = lambda s, w: pl.BlockSpec((s, w), lambda j, b: (b * e + j, 0))
    return pl.pallas_call(
        functools.partial(_experts_kernel, streams=len(streams)),
        grid=(e, bsz),
        in_specs=[rows(s, d) for s in slot_counts] + [rows(s, LANES) for s in slot_counts] + [
            pl.BlockSpec((1, 1, d, f), lambda j, b: (layer, j, 0, 0)),
            pl.BlockSpec((1, 1, d, f), lambda j, b: (layer, j, 0, 0)),
            pl.BlockSpec((1, 1, f, d), lambda j, b: (layer, j, 0, 0))],
        out_specs=[rows(s, d) for s in slot_counts],
        out_shape=[jax.ShapeDtypeStruct(xs.shape, F32) for xs, _ in streams],
        scratch_shapes=[pltpu.VMEM((d, f), BF16), pltpu.VMEM((d, f), BF16), pltpu.VMEM((f, d), BF16)],
        compiler_params=_cparams("parallel", "arbitrary"),
        name="experts",
    )(*[xs for xs, _ in streams], *[gs for _, gs in streams], w_gate, w_up, w_down)


def _combine_kernel(idx_ref, y_ref, x_ref, gt_ref, o_ref, acc, *, cap):
    e = pl.program_id(1)
    n, d = x_ref.shape

    @pl.when(e == 0)
    def _():
        acc[...] = jnp.zeros_like(acc)

    def rows(c, carry):
        base = pl.multiple_of(c * ROW_GROUP, ROW_GROUP)
        tile = y_ref[pl.ds(base, ROW_GROUP), :].reshape(ROW_GROUP, 1, d)
        tokens = [idx_ref[0, 0, c * ROW_GROUP + u] for u in range(ROW_GROUP)]
        sums = [acc[i] + tile[u] for u, i in enumerate(tokens)]
        for i, s in zip(tokens, sums):
            acc[i] = s
        return carry

    lax.fori_loop(0, cap // ROW_GROUP, rows, 0)

    @pl.when(e == pl.num_programs(1) - 1)
    def _():
        for r in range(0, n, RELAYOUT_ROWS):
            m = min(RELAYOUT_ROWS, n - r)
            o_ref[r:r + m, :] = x_ref[r:r + m, :] + gt_ref[0] * acc[r:r + m].reshape(m, d)


def _combine(idx, y, x1, gt2, *, cap):
    bsz, n, d = x1.shape
    be, _, slots = idx.shape
    e = be // bsz
    mod_map = (lambda b, j: (b, 0, 0)) if gt2.shape[0] == bsz else (lambda b, j: (0, 0, 0))
    return pl.pallas_call(
        functools.partial(_combine_kernel, cap=cap),
        grid=(bsz, e),
        in_specs=[pl.BlockSpec((1, 1, slots), lambda b, j: (b * e + j, 0, 0), memory_space=pltpu.SMEM),
                  pl.BlockSpec((slots, d), lambda b, j: (b * e + j, 0)),
                  pl.BlockSpec((None, n, d), lambda b, j: (b, 0, 0), pipeline_mode=pl.Buffered(1)),
                  pl.BlockSpec((1, 1, d), mod_map)],
        out_specs=pl.BlockSpec((None, n, d), lambda b, j: (b, 0, 0), pipeline_mode=pl.Buffered(1)),
        out_shape=jax.ShapeDtypeStruct(x1.shape, F32),
        scratch_shapes=[pltpu.VMEM((n, 1, d), F32)],
        compiler_params=_cparams("parallel", "arbitrary"),
        name="combine",
    )(idx, y, x1, gt2)


def _rope_tables(n):
    rows = n // GRID_W
    row = jnp.repeat(jnp.arange(rows, dtype=F32), GRID_W)
    col = jnp.tile(jnp.arange(GRID_W, dtype=F32), rows)
    half = HEAD // 2
    inv = ROPE_THETA ** (-jnp.arange(0, half, 2, dtype=F32) / half)
    ang = jnp.concatenate([row[:, None] * inv, col[:, None] * inv], axis=-1)
    cos, sin = jnp.cos(ang), jnp.sin(ang)
    cos = jnp.concatenate([cos, cos], axis=-1)
    sin = jnp.concatenate([-sin, sin], axis=-1)
    return jnp.tile(cos, (1, LANES // HEAD)), jnp.tile(sin, (1, LANES // HEAD))


def _block_diag2(m):
    z = jnp.zeros_like(m[0])
    return jnp.concatenate([jnp.concatenate([m[0], z], axis=1), jnp.concatenate([z, m[1]], axis=1)], axis=0)


def _permute_q_heads(w, axis):
    idx = jnp.concatenate([jnp.arange(h * HEAD, (h + 1) * HEAD) for h in Q_HEAD_ORDER])
    return jnp.take(w, idx, axis=axis)


def _layer_params(l, w_in, w_out, conv_w, q_norm_g, k_norm_g, rw_mu, rw_w0, rw_w_b, rw_a0, rw_a_b, rw_g_b,
                  rw_k_k, rw_k_a, rw_r_k, rw_ln_w, rw_ln_b, router_w, norm1_g, norm2_g):
    wi = w_in[l]
    q_cols = _permute_q_heads(wi[:, CONV_COLS:CONV_COLS + Q_COLS], 1)
    wi = jnp.concatenate([wi[:, :CONV_COLS], q_cols, wi[:, CONV_COLS + Q_COLS:],
                          jnp.zeros((wi.shape[0], RW_COLS_PAD - RW_COLS), F32)], axis=1).astype(BF16)
    wo = w_out[l]
    wo = jnp.concatenate([wo[:CONV_W], _permute_q_heads(wo[CONV_W:CONV_W + Q_COLS], 0),
                          wo[CONV_W + Q_COLS:]], axis=0).astype(BF16)
    row = lambda a: a.reshape(1, -1)
    return dict(
        w_in=wi, w_out=wo, conv_w=conv_w[l],
        qg=row(jnp.tile(q_norm_g[l], N_Q_HEADS)), kg=row(jnp.tile(k_norm_g[l], N_KV_HEADS)),
        mu=row(jnp.pad(rw_mu[l], (0, RW_COLS_PAD - RW_COLS))),
        w0=row(rw_w0[l]), wb=_block_diag2(rw_w_b[l]), a0=row(rw_a0[l]), ab=_block_diag2(rw_a_b[l]),
        gb=jnp.pad(rw_g_b[l], ((0, GATE_LORA_PAD - GATE_LORA), (0, 0))),
        kk=row(rw_k_k[l]), ka=row(rw_k_a[l]), rk=row(rw_r_k[l]), lnw=row(rw_ln_w[l]), lnb=row(rw_ln_b[l]),
        router_t=router_w[l].T, g1=row(norm1_g[l]), g2=row(norm2_g[l]),
    )


def _moe(streams, w_gate, w_up, w_down, layer):
    bsz, _, d = streams[0][0].shape
    routed = []
    for x1, h2, aff, _ in streams:
        n = x1.shape[1]
        cap = CAPACITY_FACTOR * n // N_EXPERTS
        slots = -(-cap // LANES) * LANES
        idx, aff_t = _route(aff, cap=cap, slots=slots)
        idx = idx.reshape(bsz * N_EXPERTS, 1, slots)
        routed.append((idx, cap) + tuple(_gather(idx, h2, aff_t)))
    ys = _experts([(r[2], r[3]) for r in routed], w_gate, w_up, w_down, layer, bsz=bsz)
    return [_combine(idx, y, x1, gt2, cap=cap) for (x1, _, _, gt2), (idx, cap, _, _), y in zip(streams, routed, ys)]


def kernel(x, c, ctx, c_ctx, ada_w, ada_b, norm1_g, norm2_g, w_in, w_out, conv_w, q_norm_g, k_norm_g, rw_mu,
           rw_w0, rw_w_b, rw_a0, rw_a_b, rw_g_b, rw_k_k, rw_k_a, rw_r_k, rw_ln_w, rw_ln_b, router_w,
           exp_w_gate, exp_w_up, exp_w_down):
    bsz, n, d = x.shape
    n_ctx = ctx.shape[1]
    depth = ada_w.shape[0]
    tm = min(512, n)
    tm_c = min(512, n_ctx)
    tq = min(256, n)
    tq_c = min(256, n_ctx)

    rows = -(-(bsz + 1) // SUBLANES) * SUBLANES
    cvecs = jnp.concatenate([c, c_ctx[None], jnp.zeros((rows - bsz - 1, d), F32)], axis=0)
    mod = _adaln(cvecs, ada_w, ada_b).reshape(depth, rows, N_MOD, d)

    cos, sin = _rope_tables(n)
    cos_c = jnp.ones((n_ctx, LANES), F32)
    sin_c = jnp.zeros((n_ctx, LANES), F32)
    ones_q = _block_ones(Q_COLS)
    ones_rw = _block_ones(RW)
    zero_state = jnp.zeros((2, bsz, RW, RW), F32)

    xc = ctx
    for l in range(depth):
        update_ctx = l < depth - 1
        p = _layer_params(l, w_in, w_out, conv_w, q_norm_g, k_norm_g, rw_mu, rw_w0, rw_w_b, rw_a0, rw_a_b, rw_g_b,
                          rw_k_k, rw_k_a, rw_r_k, rw_ln_w, rw_ln_b, router_w, norm1_g, norm2_g)
        lat = [mod[l, :bsz, i][:, None, :] for i in range(N_MOD)]
        cmod = [mod[l, bsz:bsz + 1, i][:, None, :] for i in range(N_MOD)]

        pc_l, q_l, k_l, v_l, rw_l = _inproj(x, lat[0], lat[1], p["g1"], p["w_in"], ones_q, p["qg"], p["kg"],
                                            cos, sin, rope=True, tm=tm)
        pc_c, q_c, k_c, v_c, rw_c = _inproj(xc, cmod[0], cmod[1], p["g1"], p["w_in"], ones_q, p["qg"], p["kg"],
                                            cos_c, sin_c, rope=False, tm=tm_c)
        feat_args = (p["conv_w"], p["mu"], p["w0"], p["wb"], p["a0"], p["ab"], p["gb"], p["kk"], p["ka"], ones_rw)
        conv_l, r_l, vv_l, kk_l, g_l, lw_l, kd_l, bd_l = _features(pc_l, rw_l, *feat_args, tm=min(256, n))
        conv_c, r_c, vv_c, kk_c, g_c, lw_c, kd_c, bd_c = _features(pc_c, rw_c, *feat_args, tm=min(256, n_ctx))

        att_l = _attend(q_l, jnp.concatenate([k_l, k_c], axis=1), jnp.concatenate([v_l, v_c], axis=1), tq=tq)

        yf_c, yb_c, s_c = _wkv(r_c, vv_c, kk_c, lw_c, kd_c, bd_c, zero_state)
        yf_l, yb_l, _ = _wkv(r_l, vv_l, kk_l, lw_l, kd_l, bd_l, s_c)
        ys_l, ys_c = (yf_l, yb_l), (yf_c, yb_c)

        out_args = (p["g2"], p["w_out"], p["rk"], p["lnw"], p["lnb"], ones_rw, p["router_t"])
        x1, h2, aff = _outproj(x, conv_l, att_l, ys_l[0], ys_l[1], r_l, vv_l, g_l, kd_l,
                               lat[2], lat[3], lat[4], *out_args, tm=tm)
        streams = [(x1, h2, aff, lat[5])]
        if update_ctx:
            att_c = _attend(q_c, k_c, v_c, tq=tq_c)
            xc1, hc2, aff_c = _outproj(xc, conv_c, att_c, ys_c[0], ys_c[1], r_c, vv_c, g_c, kd_c,
                                       cmod[2], cmod[3], cmod[4], *out_args, tm=tm_c)
            streams.append((xc1, hc2, aff_c, cmod[5]))
        outs = _moe(streams, exp_w_gate, exp_w_up, exp_w_down, l)
        x = outs[0]
        if update_ctx:
            xc = outs[1]
    return x
```

```python
import functools

import jax
import jax.numpy as jnp
from jax import lax
from jax.experimental import pallas as pl
from jax.experimental.pallas import tpu as pltpu

F32 = jnp.float32
BF16 = jnp.bfloat16
I32 = jnp.int32
HIGHEST = lax.Precision.HIGHEST

HEAD = 64
N_Q_HEADS = 8
N_KV_HEADS = 2
CONV_W = 256
RW = 256
RW_HEADS = RW // HEAD
DECAY_LORA = 64
AAA_LORA = 64
GATE_LORA = 160
N_MOD = 6
N_EXPERTS = 16
CAPACITY_FACTOR = 2
GRID_W = 64
ROPE_THETA = 10000.0
NORM_EPS = 1e-6
GN_EPS = 64e-5
Q_SCALE = 1.4426950408889634 * HEAD ** -0.5

CONV_COLS = 3 * CONV_W
Q_COLS = N_Q_HEADS * HEAD
KV_COLS = N_KV_HEADS * HEAD
ATT_COLS = Q_COLS + 2 * KV_COLS
RW_COLS = 3 * RW + 2 * (DECAY_LORA + AAA_LORA) + GATE_LORA
LANES = 128
SUBLANES = 8
RW_COLS_PAD = -(-RW_COLS // LANES) * LANES
GATE_LORA_PAD = RW_COLS_PAD - (3 * RW + 2 * (DECAY_LORA + AAA_LORA))
PROJ_PAD = CONV_COLS + ATT_COLS + RW_COLS_PAD

WKV_CHUNK = 64
VMEM_LIMIT = 56 * 1024 * 1024
WKV_PRECISION = dict(score="bf16", inverse="bf16", apply="bf16", state="bf16")

Q_HEAD_ORDER = (0, 4, 1, 5, 2, 6, 3, 7)


def _cparams(*sem):
    return pltpu.CompilerParams(dimension_semantics=sem, vmem_limit_bytes=VMEM_LIMIT)


def _dot(a, b, precision=None):
    return jnp.dot(a, b, preferred_element_type=F32, precision=precision)


def _dot_nt(a, b, precision=None):
    return lax.dot_general(a, b, (((1,), (1,)), ((), ())), preferred_element_type=F32, precision=precision)


def _dot_tn(a, b, precision=None):
    return lax.dot_general(a, b, (((0,), (0,)), ((), ())), preferred_element_type=F32, precision=precision)


def _split_bf16(x):
    hi = x.astype(BF16)
    return hi, (x - hi.astype(F32)).astype(BF16)


_NN = ((1,), (0,))
_NT = ((1,), (1,))
_TN = ((0,), (0,))


def _contract(a, b, dims, mode):
    dn = (dims, ((), ()))
    if mode == "f32":
        return lax.dot_general(a, b, dn, preferred_element_type=F32, precision=HIGHEST)
    if mode == "bf16":
        return lax.dot_general(a.astype(BF16), b.astype(BF16), dn, preferred_element_type=F32)
    ah, al = _split_bf16(a)
    bh, bl = _split_bf16(b)
    dot = lambda x, y: lax.dot_general(x, y, dn, preferred_element_type=F32)
    return dot(ah, bh) + (dot(ah, bl) + dot(al, bh))


def _sigmoid(x):
    return 1.0 / (1.0 + jnp.exp(-x))


def _group_sum(t, ones_bd):
    hi = t.astype(BF16)
    lo = (t - hi.astype(F32)).astype(BF16)
    return _dot(hi, ones_bd) + _dot(lo, ones_bd)


def _block_ones(width):
    g = jnp.arange(width) // HEAD
    return (g[:, None] == g[None, :]).astype(BF16)


def _adaln_kernel(c_ref, w_ref, b_ref, o_ref):
    c = c_ref[...]
    s = c * _sigmoid(c)
    o_ref[0] = _dot(s, w_ref[0], HIGHEST) + b_ref[0]


def _adaln(cvecs, ada_w, ada_b):
    depth, d, n = ada_w.shape
    r = cvecs.shape[0]
    tn = 1024
    return pl.pallas_call(
        _adaln_kernel,
        grid=(depth, n // tn),
        in_specs=[
            pl.BlockSpec((r, d), lambda l, j: (0, 0)),
            pl.BlockSpec((1, d, tn), lambda l, j: (l, 0, j)),
            pl.BlockSpec((1, 1, tn), lambda l, j: (l, 0, j)),
        ],
        out_specs=pl.BlockSpec((1, r, tn), lambda l, j: (l, 0, j)),
        out_shape=jax.ShapeDtypeStruct((depth, r, n), F32),
        compiler_params=_cparams("parallel", "parallel"),
        name="adaln",
    )(cvecs, ada_w, ada_b.reshape(depth, 1, n))


def _modulated_norm(x, g, shift, scale):
    ms = jnp.mean(x * x, axis=-1, keepdims=True)
    return (x * lax.rsqrt(ms + NORM_EPS) * g) * (1.0 + scale) + shift


def _head_norm(t, ones_bd, gain):
    ss = _group_sum(t * t, ones_bd)
    return t * lax.rsqrt(ss * (1.0 / HEAD) + NORM_EPS) * gain


def _rope(t, cos, sin_signed):
    width = t.shape[1]
    lane = lax.broadcasted_iota(I32, t.shape, 1)
    first_half = (lane & (HEAD - 1)) < HEAD // 2
    swapped = jnp.where(first_half, pltpu.roll(t, width - HEAD // 2, 1), pltpu.roll(t, HEAD // 2, 1))
    reps = width // LANES
    if reps > 1:
        cos = jnp.concatenate([cos] * reps, axis=1)
        sin_signed = jnp.concatenate([sin_signed] * reps, axis=1)
    return t * cos + swapped * sin_signed


def _inproj_kernel(x_ref, sh_ref, sc_ref, g_ref, w_ref, bd_ref, qg_ref, kg_ref, cos_ref, sin_ref,
                   pc_ref, q_ref, k_ref, v_ref, rw_ref, *, rope):
    h = _modulated_norm(x_ref[0], g_ref[...], sh_ref[0], sc_ref[0])
    p = _dot(h.astype(BF16), w_ref[...])
    pc_ref[0] = p[:, :CONV_COLS]
    q = p[:, CONV_COLS:CONV_COLS + Q_COLS]
    k = p[:, CONV_COLS + Q_COLS:CONV_COLS + Q_COLS + KV_COLS]
    v = p[:, CONV_COLS + Q_COLS + KV_COLS:CONV_COLS + ATT_COLS]
    q = _head_norm(q, bd_ref[...], qg_ref[...])
    k = _head_norm(k, bd_ref[:KV_COLS, :KV_COLS], kg_ref[...])
    if rope:
        q = _rope(q, cos_ref[...], sin_ref[...])
        k = _rope(k, cos_ref[...], sin_ref[...])
    q_ref[0] = (q * Q_SCALE).astype(BF16)
    k_ref[0] = k.astype(BF16)
    v_ref[0] = v.astype(BF16)
    rw_ref[0] = p[:, CONV_COLS + ATT_COLS:]


def _inproj(x, shift, scale, g, w_in, ones_q, qg, kg, cos, sin, *, rope, tm):
    bsz, n, d = x.shape
    mod_map = (lambda b, i: (b, 0, 0)) if shift.shape[0] == bsz else (lambda b, i: (0, 0, 0))
    const2 = lambda b, i: (0, 0)
    tile = lambda w: pl.BlockSpec((1, tm, w), lambda b, i: (b, i, 0))
    return pl.pallas_call(
        functools.partial(_inproj_kernel, rope=rope),
        grid=(bsz, n // tm),
        in_specs=[
            tile(d),
            pl.BlockSpec((1, 1, d), mod_map),
            pl.BlockSpec((1, 1, d), mod_map),
            pl.BlockSpec((1, d), const2),
            pl.BlockSpec((d, PROJ_PAD), const2),
            pl.BlockSpec((Q_COLS, Q_COLS), const2),
            pl.BlockSpec((1, Q_COLS), const2),
            pl.BlockSpec((1, KV_COLS), const2),
            pl.BlockSpec((tm, LANES), lambda b, i: (i, 0)),
            pl.BlockSpec((tm, LANES), lambda b, i: (i, 0)),
        ],
        out_specs=[tile(CONV_COLS), tile(Q_COLS), tile(KV_COLS), tile(KV_COLS), tile(RW_COLS_PAD)],
        out_shape=[
            jax.ShapeDtypeStruct((bsz, n, CONV_COLS), F32),
            jax.ShapeDtypeStruct((bsz, n, Q_COLS), BF16),
            jax.ShapeDtypeStruct((bsz, n, KV_COLS), BF16),
            jax.ShapeDtypeStruct((bsz, n, KV_COLS), BF16),
            jax.ShapeDtypeStruct((bsz, n, RW_COLS_PAD), F32),
        ],
        compiler_params=_cparams("parallel", "parallel"),
        name="inproj",
    )(x, shift, scale, g, w_in, ones_q, qg, kg, cos, sin)


def _shifted(x, prev_row, next_row):
    rows = lax.broadcasted_iota(I32, x.shape, 0)
    tm = x.shape[0]
    before = jnp.where(rows == 0, prev_row, pltpu.roll(x, 1, 0))
    after = jnp.where(rows == tm - 1, next_row, pltpu.roll(x, tm - 1, 0))
    return before, after


def _features_kernel(pc_ref, pc_prev_ref, pc_next_ref, rw_ref, rw_prev_ref, rw_next_ref,
                     conv_w_ref, mu_ref, w0_ref, wb_ref, a0_ref, ab_ref, gb_ref, kk_ref, ka_ref, bd_ref,
                     conv_ref, r_ref, v_ref, kkn_ref, g_ref, lw_ref, kd_ref, bdir_ref):
    i = pl.program_id(1)
    has_prev = (i > 0).astype(F32)
    has_next = (i < pl.num_programs(1) - 1).astype(F32)

    pc = pc_ref[0]
    pcp = pc_prev_ref[0][SUBLANES - 1:SUBLANES, :] * has_prev
    pcn = pc_next_ref[0][0:1, :] * has_next
    z = pc[:, CONV_W:2 * CONV_W] * pc[:, 2 * CONV_W:]
    zp = pcp[:, CONV_W:2 * CONV_W] * pcp[:, 2 * CONV_W:]
    zn = pcn[:, CONV_W:2 * CONV_W] * pcn[:, 2 * CONV_W:]
    zb, za = _shifted(z, zp, zn)
    cw = conv_w_ref[...]
    conv_ref[0] = pc[:, :CONV_W] * (cw[0:1] * zb + cw[1:2] * z + cw[2:3] * za)

    p = rw_ref[0]
    pb, pa = _shifted(p, rw_prev_ref[0][SUBLANES - 1:SUBLANES, :] * has_prev, rw_next_ref[0][0:1, :] * has_next)
    p = p + mu_ref[...] * (0.5 * (pb + pa) - p)
    r = p[:, :RW]
    k = p[:, RW:2 * RW]
    v = p[:, 2 * RW:3 * RW]
    o = 3 * RW
    wl = p[:, o:o + 2 * DECAY_LORA]
    al = p[:, o + 2 * DECAY_LORA:o + 2 * (DECAY_LORA + AAA_LORA)]
    gl = p[:, o + 2 * (DECAY_LORA + AAA_LORA):]
    wz = -(w0_ref[...] + _contract(jnp.tanh(wl), wb_ref[...], _NN, "bf16"))
    softplus = jnp.maximum(wz, 0.0) + jnp.log(1.0 + jnp.exp(-jnp.abs(wz)))
    lw_ref[0] = -jnp.exp(-softplus - 0.5)
    a = _sigmoid(a0_ref[...] + _contract(al, ab_ref[...], _NN, "bf16"))
    kk = k * kk_ref[...]
    kk = kk * lax.rsqrt(jnp.maximum(_group_sum(kk * kk, bd_ref[...]), 1e-24))
    k2 = jnp.concatenate([k, k], axis=1)
    kk2 = jnp.concatenate([kk, kk], axis=1)
    ka2 = jnp.concatenate([ka_ref[...], ka_ref[...]], axis=1)
    kd_ref[0] = k2 * (1.0 + (a - 1.0) * ka2)
    bdir_ref[0] = kk2 * a
    r_ref[0] = r
    v_ref[0] = v
    kkn_ref[0] = kk
    g_ref[0] = _contract(_sigmoid(gl), gb_ref[...], _NN, "bf16")


def _features(pc, prw, conv_w, mu, w0, wb, a0, ab, gb, kk, ka, ones_rw, *, tm):
    bsz, n, _ = pc.shape
    nt = n // tm
    r8 = tm // SUBLANES
    tile = lambda w: pl.BlockSpec((1, tm, w), lambda b, i: (b, i, 0))
    prev = lambda w: pl.BlockSpec((1, SUBLANES, w), lambda b, i: (b, jnp.maximum(i * r8 - 1, 0), 0))
    nxt = lambda w: pl.BlockSpec((1, SUBLANES, w), lambda b, i: (b, jnp.minimum((i + 1) * r8, nt * r8 - 1), 0))
    full = lambda a: pl.BlockSpec(a.shape, lambda b, i: (0,) * a.ndim)
    params = (conv_w, mu, w0, wb, a0, ab, gb, kk, ka, ones_rw)
    widths = (CONV_W, RW, RW, RW, RW, 2 * RW, 2 * RW, 2 * RW)
    return pl.pallas_call(
        _features_kernel,
        grid=(bsz, nt),
        in_specs=[tile(CONV_COLS), prev(CONV_COLS), nxt(CONV_COLS),
                  tile(RW_COLS_PAD), prev(RW_COLS_PAD), nxt(RW_COLS_PAD)] + [full(a) for a in params],
        out_specs=[tile(w) for w in widths],
        out_shape=[jax.ShapeDtypeStruct((bsz, n, w), F32) for w in widths],
        compiler_params=_cparams("parallel", "parallel"),
        name="features",
    )(pc, pc, pc, prw, prw, prw, *params)


def _attend_kernel(q_ref, k_ref, v_ref, o_ref):
    k = k_ref[0]
    v = v_ref[0]
    lane = lax.broadcasted_iota(I32, (q_ref.shape[1], LANES), 1)
    low = lane < HEAD
    key_low = lax.broadcasted_iota(I32, v.shape, 1) < HEAD
    ones = jnp.ones_like(v)
    v_sides = {True: jnp.where(key_low, v, ones), False: jnp.where(key_low, ones, v)}
    for c in range(Q_COLS // LANES):
        qc = q_ref[0, :, c * LANES:(c + 1) * LANES]
        outs = []
        for use_low in (True, False):
            qm = jnp.where(low == use_low, qc, jnp.zeros_like(qc))
            s = _dot_nt(qm, k)
            p = jnp.exp2(s - jnp.max(s, axis=-1, keepdims=True))
            o = _dot(p.astype(BF16), v_sides[use_low])
            outs.append(o / pltpu.roll(o, HEAD, 1))
        o_ref[0, :, c * LANES:(c + 1) * LANES] = jnp.where(low, outs[0], outs[1]).astype(BF16)


def _attend(q, k, v, *, tq):
    bsz, n, _ = q.shape
    nk = k.shape[1]
    return pl.pallas_call(
        _attend_kernel,
        grid=(bsz, n // tq),
        in_specs=[
            pl.BlockSpec((1, tq, Q_COLS), lambda b, i: (b, i, 0)),
            pl.BlockSpec((1, nk, KV_COLS), lambda b, i: (b, 0, 0)),
            pl.BlockSpec((1, nk, KV_COLS), lambda b, i: (b, 0, 0)),
        ],
        out_specs=pl.BlockSpec((1, tq, Q_COLS), lambda b, i: (b, i, 0)),
        out_shape=jax.ShapeDtypeStruct((bsz, n, Q_COLS), BF16),
        compiler_params=_cparams("parallel", "parallel"),
        name="attend",
    )(q, k, v)


def _head_blocks(x, mode):
    if mode == "bf16":
        x = x.astype(BF16)
    head = lax.broadcasted_iota(I32, x.shape, 1) >> 6
    return jnp.concatenate([jnp.where(head == h, x, jnp.zeros_like(x)) for h in range(RW_HEADS)], axis=0)


def _wkv_chunks(chains, prec):
    c = WKV_CHUNK
    ps, pi, pa, pst = prec["score"], prec["inverse"], prec["apply"], prec["state"]
    ri = lax.broadcasted_iota(I32, (c, c), 0)
    ci = lax.broadcasted_iota(I32, (c, c), 1)
    row = lax.broadcasted_iota(I32, (c, RW_HEADS * c), 0)
    col = lax.broadcasted_iota(I32, (c, RW_HEADS * c), 1) & (c - 1)
    eye = (col == row).astype(F32)
    masks = {}
    for reverse in sorted({ch[7] for ch in chains}):
        masks[reverse] = (((ci >= ri) if reverse else (ci <= ri)).astype(BF16),
                          (col > row) if reverse else (col < row),
                          (col >= row) if reverse else (col <= row))

    pre = []
    for r, v, kk, lw, kd, bd, s, reverse in chains:
        tri = masks[reverse][0]
        lw_hi, lw_lo = _split_bf16(lw)
        lw_mid, lw_lo = _split_bf16(lw - lw_hi.astype(F32))
        cum = _dot(tri, lw_hi) + (_dot(tri, lw_mid) + _dot(tri, lw_lo))
        total = jnp.sum(lw, axis=0, keepdims=True)
        inv = jnp.exp(-cum)
        rem = jnp.exp(total - cum)
        pre.append(dict(a_t=-kk * jnp.exp(cum - lw), r_t=r * jnp.exp(cum), b_t=bd * inv, k_t=kd * inv,
                        b_end=bd * rem, k_end=kd * rem, decay=jnp.exp(total)))

    scores = []
    for ch, q in zip(chains, pre):
        _, strict, incl = masks[ch[7]]
        ar = jnp.concatenate([q["a_t"], q["r_t"]], axis=0)
        sb = _contract(ar, _head_blocks(q["b_t"], ps), _NT, ps)
        sk = _contract(ar, _head_blocks(q["k_t"], ps), _NT, ps)
        scores.append(dict(a_ab=jnp.where(strict, sb[:c], 0.0), a_rb=jnp.where(incl, sb[c:], 0.0),
                           a_ak=jnp.where(strict, sk[:c], 0.0), a_rk=jnp.where(incl, sk[c:], 0.0)))

    t_inv = [eye + sc["a_ab"] for sc in scores]
    m = [_contract(sc["a_ab"], _head_blocks(sc["a_ab"], pi), _NN, pi) for sc in scores]
    for _ in range((c - 1).bit_length() - 2):
        both = [_contract(jnp.concatenate([x, t], axis=0), _head_blocks(x, pi), _NN, pi) for x, t in zip(m, t_inv)]
        m = [x[:c] for x in both]
        t_inv = [t + x[c:] for t, x in zip(t_inv, both)]
    t_inv = [t + _contract(t, _head_blocks(x, pi), _NN, pi) for t, x in zip(t_inv, m)]

    av = [_contract(jnp.concatenate([sc["a_ak"], sc["a_rk"]], axis=0), _head_blocks(ch[1], pa), _NN, pa)
          for ch, sc in zip(chains, scores)]
    w1 = [_contract(t, _head_blocks(q["a_t"], pa), _NN, pa) for t, q in zip(t_inv, pre)]
    w2 = [_contract(t, _head_blocks(x[:c], pa), _NN, pa) for t, x in zip(t_inv, av)]
    g = [_contract(jnp.concatenate([x, q["r_t"]], axis=0), ch[6], _NT, pst) for x, q, ch in zip(w1, pre, chains)]
    u = [x[:c] + y for x, y in zip(g, w2)]
    ys = [x[c:] + _contract(sc["a_rb"], _head_blocks(uu, pa), _NN, pa) + y[c:]
          for x, sc, uu, y in zip(g, scores, u, av)]
    upd = [_contract(jnp.concatenate([uu, ch[1]], axis=0), jnp.concatenate([q["b_end"], q["k_end"]], axis=0), _TN, pst)
           for uu, ch, q in zip(u, chains, pre)]
    hr = lax.broadcasted_iota(I32, (RW, RW), 0) >> 6
    hc = lax.broadcasted_iota(I32, (RW, RW), 1) >> 6
    s_new = [ch[6] * q["decay"] + jnp.where(hr == hc, x, 0.0) for ch, q, x in zip(chains, pre, upd)]
    return ys, s_new


def _wkv_kernel(rf_ref, vf_ref, kkf_ref, lwf_ref, kdf_ref, bdf_ref, rb_ref, vb_ref, kkb_ref, lwb_ref, kdb_ref,
                bdb_ref, s0_ref, yf_ref, yb_ref, s_out_ref, s_scr, *, prec):
    step = pl.program_id(0)

    @pl.when(step == 0)
    def _():
        s_scr[...] = s0_ref[...]

    bsz = rf_ref.shape[0]
    chains = []
    for b in range(bsz):
        chains.append((rf_ref[b], vf_ref[b], kkf_ref[b], lwf_ref[b], kdf_ref[b], bdf_ref[b], s_scr[0, b], False))
        chains.append((rb_ref[b], vb_ref[b], kkb_ref[b], lwb_ref[b], kdb_ref[b], bdb_ref[b], s_scr[1, b], True))
    ys, s_new = _wkv_chunks(chains, prec)
    for b in range(bsz):
        yf_ref[b] = ys[2 * b]
        yb_ref[b] = ys[2 * b + 1]
        s_scr[0, b] = s_new[2 * b]
        s_scr[1, b] = s_new[2 * b + 1]

    @pl.when(step == pl.num_programs(0) - 1)
    def _():
        s_out_ref[...] = s_scr[...]


def _wkv(r, v, kk, lw, kd, bd, s0, prec=None):
    prec = prec or WKV_PRECISION
    bsz, n, _ = r.shape
    steps = n // WKV_CHUNK
    blk = (bsz, WKV_CHUNK, RW)
    fwd = pl.BlockSpec(blk, lambda i: (0, i, 0))
    bwd = pl.BlockSpec(blk, lambda i: (0, steps - 1 - i, 0))
    bwd_dir = pl.BlockSpec(blk, lambda i: (0, steps - 1 - i, 1))
    state = pl.BlockSpec((2, bsz, RW, RW), lambda i: (0, 0, 0, 0))
    y_shape = jax.ShapeDtypeStruct((bsz, n, RW), F32)
    return pl.pallas_call(
        functools.partial(_wkv_kernel, prec=prec),
        grid=(steps,),
        in_specs=[fwd, fwd, fwd, fwd, fwd, fwd, bwd, bwd, bwd, bwd_dir, bwd_dir, bwd_dir, state],
        out_specs=[fwd, bwd, state],
        out_shape=[y_shape, y_shape, jax.ShapeDtypeStruct((2, bsz, RW, RW), F32)],
        scratch_shapes=[pltpu.VMEM((2, bsz, RW, RW), F32)],
        compiler_params=_cparams("arbitrary"),
        name="wkv",
    )(r, v, kk, lw, kd, bd, r, v, kk, lw, kd, bd, s0)


def _outproj_kernel(x_ref, conv_ref, att_ref, yf_ref, yb_ref, r_ref, v_ref, g_ref, kd_ref,
                    gt1_ref, sh2_ref, sc2_ref, g2_ref, w_ref, rk_ref, lnw_ref, lnb_ref, bd_ref, rwt_ref,
                    x1_ref, h2_ref, aff_ref):
    ones_bd = bd_ref[...]
    y = yf_ref[0] + yb_ref[0]
    mean = _group_sum(y, ones_bd) * (1.0 / HEAD)
    yc = y - mean
    var = _group_sum(yc * yc, ones_bd) * (1.0 / HEAD)
    yn = yc * lax.rsqrt(var + GN_EPS)
    kd = kd_ref[0]
    k_mean = 0.5 * (kd[:, :RW] + kd[:, RW:])
    bonus = _group_sum(r_ref[0] * k_mean * rk_ref[...], ones_bd) * v_ref[0]
    rwkv = (yn * lnw_ref[...] + lnb_ref[...] + bonus) * g_ref[0]

    mixed = (_dot(conv_ref[0].astype(BF16), w_ref[:CONV_W])
             + _dot(att_ref[0], w_ref[CONV_W:CONV_W + Q_COLS])
             + _dot(rwkv.astype(BF16), w_ref[CONV_W + Q_COLS:]))
    x1 = x_ref[0] + gt1_ref[0] * mixed
    x1_ref[0] = x1

    h2 = _modulated_norm(x1, g2_ref[...], sh2_ref[0], sc2_ref[0])
    h2_ref[0] = h2
    logits = _contract(rwt_ref[...], h2, _NT, "x3")
    e = jnp.exp(logits - jnp.max(logits, axis=0, keepdims=True))
    aff_ref[0] = e / jnp.sum(e, axis=0, keepdims=True)


def _outproj(x, conv, att, yf, yb, r, v, g, kd, gt1, sh2, sc2, g2, w_out, rk, lnw, lnb, ones_rw, router_t, *, tm):
    bsz, n, d = x.shape
    mod_map = (lambda b, i: (b, 0, 0)) if gt1.shape[0] == bsz else (lambda b, i: (0, 0, 0))
    tile = lambda w: pl.BlockSpec((1, tm, w), lambda b, i: (b, i, 0))
    mod = pl.BlockSpec((1, 1, d), mod_map)
    full = lambda a: pl.BlockSpec(a.shape, lambda b, i: (0,) * a.ndim)
    params = (g2, w_out, rk, lnw, lnb, ones_rw, router_t)
    return pl.pallas_call(
        _outproj_kernel,
        grid=(bsz, n // tm),
        in_specs=[tile(d), tile(CONV_W), tile(Q_COLS), tile(RW), tile(RW), tile(RW), tile(RW), tile(RW),
                  tile(2 * RW), mod, mod, mod] + [full(a) for a in params],
        out_specs=[tile(d), tile(d), pl.BlockSpec((1, N_EXPERTS, tm), lambda b, i: (b, 0, i))],
        out_shape=[
            jax.ShapeDtypeStruct((bsz, n, d), F32),
            jax.ShapeDtypeStruct((bsz, n, d), F32),
            jax.ShapeDtypeStruct((bsz, N_EXPERTS, n), F32),
        ],
        compiler_params=_cparams("parallel", "parallel"),
        name="outproj",
    )(x, conv, att, yf, yb, r, v, g, kd, gt1, sh2, sc2, *params)


def _cumsum_lanes(x):
    n = x.shape[1]
    w = min(n, 256)
    ri = lax.broadcasted_iota(I32, (w, w), 0)
    ci = lax.broadcasted_iota(I32, (w, w), 1)
    tri = (ri <= ci).astype(BF16)
    carry = jnp.zeros((x.shape[0], 1), F32)
    parts = []
    for j in range(n // w):
        cs = _dot(x[:, j * w:(j + 1) * w].astype(BF16), tri) + carry
        carry = cs[:, w - 1:w]
        parts.append(cs)
    return parts[0] if len(parts) == 1 else jnp.concatenate(parts, axis=1)


def _to_token_major(x):
    n = x.shape[1]
    pad = jnp.zeros((LANES - x.shape[0], LANES), F32)
    blocks = [jnp.concatenate([x[:, j * LANES:(j + 1) * LANES], pad], axis=0).T for j in range(n // LANES)]
    return blocks[0] if len(blocks) == 1 else jnp.concatenate(blocks, axis=0)


def _route_kernel(aff_ref, idx_ref, aff_t_ref, *, cap, slots):
    aff = aff_ref[0]
    n_exp, n = aff.shape
    bits = pltpu.bitcast(aff, I32)

    def search(i, thr):
        cand = thr | (1 << (30 - i))
        cnt = jnp.sum((bits >= cand).astype(F32), axis=1, keepdims=True)
        return jnp.where(cnt >= cap, cand, thr)

    thr = lax.fori_loop(0, 31, search, jnp.zeros((aff.shape[0], 1), I32))
    above = bits > thr
    tied = bits == thr
    need = cap - jnp.sum(above.astype(F32), axis=1, keepdims=True)
    tied_f = tied.astype(F32)
    chosen = jnp.where(above | (tied & (_cumsum_lanes(tied_f) <= need)), 1.0, 0.0)
    rank = _cumsum_lanes(chosen)
    aff_t_ref[0] = _to_token_major(aff)
    rank_t = _to_token_major(rank)
    rows = min(n, 512)
    slot = lax.broadcasted_iota(I32, (rows, slots), 1).astype(F32)
    expert = lax.broadcasted_iota(I32, (n_exp, slots), 0)
    idx = jnp.zeros((n_exp, slots), F32)
    for e in range(n_exp):
        cnt = jnp.zeros((1, slots), F32)
        for r0 in range(0, n, rows):
            cnt = cnt + jnp.sum(jnp.where(rank_t[r0:r0 + rows, e:e + 1] <= slot, 1.0, 0.0), axis=0, keepdims=True)
        idx = jnp.where(expert == e, cnt, idx)
    idx_ref[0] = jnp.minimum(idx, n - 1.0).astype(I32)


def _route(aff, *, cap, slots):
    bsz, e, n = aff.shape
    return pl.pallas_call(
        functools.partial(_route_kernel, cap=cap, slots=slots),
        grid=(bsz,),
        in_specs=[pl.BlockSpec((1, e, n), lambda b: (b, 0, 0))],
        out_specs=[pl.BlockSpec((1, e, slots), lambda b: (b, 0, 0)),
                   pl.BlockSpec((1, n, LANES), lambda b: (b, 0, 0))],
        out_shape=[jax.ShapeDtypeStruct((bsz, e, slots), I32),
                   jax.ShapeDtypeStruct((bsz, n, LANES), F32)],
        compiler_params=_cparams("parallel"),
        name="route",
    )(aff)


ROW_GROUP = SUBLANES
RELAYOUT_ROWS = 256


def _gather_kernel(idx_ref, h_ref, aff_ref, xs_ref, gs_ref, h_rows, a_rows):
    n, d = h_ref.shape

    @pl.when(pl.program_id(1) == 0)
    def _():
        for r in range(0, n, RELAYOUT_ROWS):
            m = min(RELAYOUT_ROWS, n - r)
            h_rows[r:r + m] = h_ref[r:r + m, :].reshape(m, 1, d)
            a_rows[r:r + m] = aff_ref[r:r + m, :].reshape(m, 1, LANES)

    def rows(c, carry):
        base = pl.multiple_of(c * ROW_GROUP, ROW_GROUP)
        tokens = [idx_ref[0, 0, c * ROW_GROUP + u] for u in range(ROW_GROUP)]
        xs_ref[pl.ds(base, ROW_GROUP), :] = jnp.concatenate([h_rows[i] for i in tokens], axis=0)
        gs_ref[pl.ds(base, ROW_GROUP), :] = jnp.concatenate([a_rows[i] for i in tokens], axis=0)
        return carry

    lax.fori_loop(0, xs_ref.shape[0] // ROW_GROUP, rows, 0)


def _gather(idx, h2, aff_t):
    bsz, n, d = h2.shape
    be, _, slots = idx.shape
    e = be // bsz
    return pl.pallas_call(
        _gather_kernel,
        grid=(bsz, e),
        in_specs=[pl.BlockSpec((1, 1, slots), lambda b, j: (b * e + j, 0, 0), memory_space=pltpu.SMEM),
                  pl.BlockSpec((None, n, d), lambda b, j: (b, 0, 0), pipeline_mode=pl.Buffered(1)),
                  pl.BlockSpec((None, n, LANES), lambda b, j: (b, 0, 0))],
        out_specs=[pl.BlockSpec((slots, d), lambda b, j: (b * e + j, 0)),
                   pl.BlockSpec((slots, LANES), lambda b, j: (b * e + j, 0))],
        out_shape=[jax.ShapeDtypeStruct((be * slots, d), F32),
                   jax.ShapeDtypeStruct((be * slots, LANES), F32)],
        scratch_shapes=[pltpu.VMEM((n, 1, d), F32), pltpu.VMEM((n, 1, LANES), F32)],
        compiler_params=_cparams("parallel", "arbitrary"),
        name="gather",
    )(idx, h2, aff_t)


def _experts_kernel(*refs, streams):
    xs_refs, gs_refs = refs[:streams], refs[streams:2 * streams]
    wg_ref, wu_ref, wd_ref = refs[2 * streams:2 * streams + 3]
    y_refs, (wg_s, wu_s, wd_s) = refs[2 * streams + 3:3 * streams + 3], refs[3 * streams + 3:]
    e = pl.program_id(0)

    @pl.when(pl.program_id(1) == 0)
    def _():
        wg_s[...] = wg_ref[0, 0].astype(BF16)
        wu_s[...] = wu_ref[0, 0].astype(BF16)
        wd_s[...] = wd_ref[0, 0].astype(BF16)

    for xs_ref, gs_ref, y_ref in zip(xs_refs, gs_refs, y_refs):
        xs = xs_ref[...].astype(BF16)
        hg = _dot(xs, wg_s[...])
        hu = _dot(xs, wu_s[...])
        hid = (hg * _sigmoid(hg)) * hu
        gs = gs_ref[...]
        lane = lax.broadcasted_iota(I32, gs.shape, 1)
        gate = jnp.sum(jnp.where(lane == e, gs, 0.0), axis=1, keepdims=True)
        y_ref[...] = _dot(hid.astype(BF16), wd_s[...]) * gate


def _experts(streams, w_gate, w_up, w_down, layer, *, bsz):
    e, d, f = w_gate.shape[1], w_gate.shape[2], w_gate.shape[3]
    slot_counts = [xs.shape[0] // (bsz * e) for xs, _ in streams]
    rows = lambda s, w: pl.BlockSpec((s, w), lambda j, b: (b * e + j, 0))
    return pl.pallas_call(
        functools.partial(_experts_kernel, streams=len(streams)),
        grid=(e, bsz),
        in_specs=[rows(s, d) for s in slot_counts] + [rows(s, LANES) for s in slot_counts] + [
            pl.BlockSpec((1, 1, d, f), lambda j, b: (layer, j, 0, 0)),
            pl.BlockSpec((1, 1, d, f), lambda j, b: (layer, j, 0, 0)),
            pl.BlockSpec((1, 1, f, d), lambda j, b: (layer, j, 0, 0))],
        out_specs=[rows(s, d) for s in slot_counts],
        out_shape=[jax.ShapeDtypeStruct(xs.shape, F32) for xs, _ in streams],
        scratch_shapes=[pltpu.VMEM((d, f), BF16), pltpu.VMEM((d, f), BF16), pltpu.VMEM((f, d), BF16)],
        compiler_params=_cparams("parallel", "arbitrary"),
        name="experts",
    )(*[xs for xs, _ in streams], *[gs for _, gs in streams], w_gate, w_up, w_down)


def _combine_kernel(idx_ref, y_ref, x_ref, gt_ref, o_ref, acc, *, cap):
    e = pl.program_id(1)
    n, d = x_ref.shape

    @pl.when(e == 0)
    def _():
        acc[...] = jnp.zeros_like(acc)

    def rows(c, carry):
        base = pl.multiple_of(c * ROW_GROUP, ROW_GROUP)
        tile = y_ref[pl.ds(base, ROW_GROUP), :].reshape(ROW_GROUP, 1, d)
        tokens = [idx_ref[0, 0, c * ROW_GROUP + u] for u in range(ROW_GROUP)]
        sums = [acc[i] + tile[u] for u, i in enumerate(tokens)]
        for i, s in zip(tokens, sums):
            acc[i] = s
        return carry

    lax.fori_loop(0, cap // ROW_GROUP, rows, 0)

    @pl.when(e == pl.num_programs(1) - 1)
    def _():
        for r in range(0, n, RELAYOUT_ROWS):
            m = min(RELAYOUT_ROWS, n - r)
            o_ref[r:r + m, :] = x_ref[r:r + m, :] + gt_ref[0] * acc[r:r + m].reshape(m, d)


def _combine(idx, y, x1, gt2, *, cap):
    bsz, n, d = x1.shape
    be, _, slots = idx.shape
    e = be // bsz
    mod_map = (lambda b, j: (b, 0, 0)) if gt2.shape[0] == bsz else (lambda b, j: (0, 0, 0))
    return pl.pallas_call(
        functools.partial(_combine_kernel, cap=cap),
        grid=(bsz, e),
        in_specs=[pl.BlockSpec((1, 1, slots), lambda b, j: (b * e + j, 0, 0), memory_space=pltpu.SMEM),
                  pl.BlockSpec((slots, d), lambda b, j: (b * e + j, 0)),
                  pl.BlockSpec((None, n, d), lambda b, j: (b, 0, 0), pipeline_mode=pl.Buffered(1)),
                  pl.BlockSpec((1, 1, d), mod_map)],
        out_specs=pl.BlockSpec((None, n, d), lambda b, j: (b, 0, 0), pipeline_mode=pl.Buffered(1)),
        out_shape=jax.ShapeDtypeStruct(x1.shape, F32),
        scratch_shapes=[pltpu.VMEM((n, 1, d), F32)],
        compiler_params=_cparams("parallel", "arbitrary"),
        name="combine",
    )(idx, y, x1, gt2)


def _rope_tables(n):
    rows = n // GRID_W
    row = jnp.repeat(jnp.arange(rows, dtype=F32), GRID_W)
    col = jnp.tile(jnp.arange(GRID_W, dtype=F32), rows)
    half = HEAD // 2
    inv = ROPE_THETA ** (-jnp.arange(0, half, 2, dtype=F32) / half)
    ang = jnp.concatenate([row[:, None] * inv, col[:, None] * inv], axis=-1)
    cos, sin = jnp.cos(ang), jnp.sin(ang)
    cos = jnp.concatenate([cos, cos], axis=-1)
    sin = jnp.concatenate([-sin, sin], axis=-1)
    return jnp.tile(cos, (1, LANES // HEAD)), jnp.tile(sin, (1, LANES // HEAD))


def _block_diag2(m):
    z = jnp.zeros_like(m[0])
    return jnp.concatenate([jnp.concatenate([m[0], z], axis=1), jnp.concatenate([z, m[1]], axis=1)], axis=0)


def _permute_q_heads(w, axis):
    idx = jnp.concatenate([jnp.arange(h * HEAD, (h + 1) * HEAD) for h in Q_HEAD_ORDER])
    return jnp.take(w, idx, axis=axis)


def _layer_params(l, w_in, w_out, conv_w, q_norm_g, k_norm_g, rw_mu, rw_w0, rw_w_b, rw_a0, rw_a_b, rw_g_b,
                  rw_k_k, rw_k_a, rw_r_k, rw_ln_w, rw_ln_b, router_w, norm1_g, norm2_g):
    wi = w_in[l]
    q_cols = _permute_q_heads(wi[:, CONV_COLS:CONV_COLS + Q_COLS], 1)
    wi = jnp.concatenate([wi[:, :CONV_COLS], q_cols, wi[:, CONV_COLS + Q_COLS:],
                          jnp.zeros((wi.shape[0], RW_COLS_PAD - RW_COLS), F32)], axis=1).astype(BF16)
    wo = w_out[l]
    wo = jnp.concatenate([wo[:CONV_W], _permute_q_heads(wo[CONV_W:CONV_W + Q_COLS], 0),
                          wo[CONV_W + Q_COLS:]], axis=0).astype(BF16)
    row = lambda a: a.reshape(1, -1)
    return dict(
        w_in=wi, w_out=wo, conv_w=conv_w[l],
        qg=row(jnp.tile(q_norm_g[l], N_Q_HEADS)), kg=row(jnp.tile(k_norm_g[l], N_KV_HEADS)),
        mu=row(jnp.pad(rw_mu[l], (0, RW_COLS_PAD - RW_COLS))),
        w0=row(rw_w0[l]), wb=_block_diag2(rw_w_b[l]), a0=row(rw_a0[l]), ab=_block_diag2(rw_a_b[l]),
        gb=jnp.pad(rw_g_b[l], ((0, GATE_LORA_PAD - GATE_LORA), (0, 0))),
        kk=row(rw_k_k[l]), ka=row(rw_k_a[l]), rk=row(rw_r_k[l]), lnw=row(rw_ln_w[l]), lnb=row(rw_ln_b[l]),
        router_t=router_w[l].T, g1=row(norm1_g[l]), g2=row(norm2_g[l]),
    )


def _moe(streams, w_gate, w_up, w_down, layer):
    bsz, _, d = streams[0][0].shape
    routed = []
    for x1, h2, aff, _ in streams:
        n = x1.shape[1]
        cap = CAPACITY_FACTOR * n // N_EXPERTS
        slots = -(-cap // LANES) * LANES
        idx, aff_t = _route(aff, cap=cap, slots=slots)
        idx = idx.reshape(bsz * N_EXPERTS, 1, slots)
        routed.append((idx, cap) + tuple(_gather(idx, h2, aff_t)))
    ys = _experts([(r[2], r[3]) for r in routed], w_gate, w_up, w_down, layer, bsz=bsz)
    return [_combine(idx, y, x1, gt2, cap=cap) for (x1, _, _, gt2), (idx, cap, _, _), y in zip(streams, routed, ys)]


def kernel(x, c, ctx, c_ctx, ada_w, ada_b, norm1_g, norm2_g, w_in, w_out, conv_w, q_norm_g, k_norm_g, rw_mu,
           rw_w0, rw_w_b, rw_a0, rw_a_b, rw_g_b, rw_k_k, rw_k_a, rw_r_k, rw_ln_w, rw_ln_b, router_w,
           exp_w_gate, exp_w_up, exp_w_down):
    bsz, n, d = x.shape
    n_ctx = ctx.shape[1]
    depth = ada_w.shape[0]
    tm = min(512, n)
    tm_c = min(512, n_ctx)
    tq = min(256, n)
    tq_c = min(256, n_ctx)

    rows = -(-(bsz + 1) // SUBLANES) * SUBLANES
    cvecs = jnp.concatenate([c, c_ctx[None], jnp.zeros((rows - bsz - 1, d), F32)], axis=0)
    mod = _adaln(cvecs, ada_w, ada_b).reshape(depth, rows, N_MOD, d)

    cos, sin = _rope_tables(n)
    cos_c = jnp.ones((n_ctx, LANES), F32)
    sin_c = jnp.zeros((n_ctx, LANES), F32)
    ones_q = _block_ones(Q_COLS)
    ones_rw = _block_ones(RW)
    zero_state = jnp.zeros((2, bsz, RW, RW), F32)

    xc = ctx
    for l in range(depth):
        update_ctx = l < depth - 1
        p = _layer_params(l, w_in, w_out, conv_w, q_norm_g, k_norm_g, rw_mu, rw_w0, rw_w_b, rw_a0, rw_a_b, rw_g_b,
                          rw_k_k, rw_k_a, rw_r_k, rw_ln_w, rw_ln_b, router_w, norm1_g, norm2_g)
        lat = [mod[l, :bsz, i][:, None, :] for i in range(N_MOD)]
        cmod = [mod[l, bsz:bsz + 1, i][:, None, :] for i in range(N_MOD)]

        pc_l, q_l, k_l, v_l, rw_l = _inproj(x, lat[0], lat[1], p["g1"], p["w_in"], ones_q, p["qg"], p["kg"],
                                            cos, sin, rope=True, tm=tm)
        pc_c, q_c, k_c, v_c, rw_c = _inproj(xc, cmod[0], cmod[1], p["g1"], p["w_in"], ones_q, p["qg"], p["kg"],
                                            cos_c, sin_c, rope=False, tm=tm_c)
        feat_args = (p["conv_w"], p["mu"], p["w0"], p["wb"], p["a0"], p["ab"], p["gb"], p["kk"], p["ka"], ones_rw)
        conv_l, r_l, vv_l, kk_l, g_l, lw_l, kd_l, bd_l = _features(pc_l, rw_l, *feat_args, tm=min(256, n))
        conv_c, r_c, vv_c, kk_c, g_c, lw_c, kd_c, bd_c = _features(pc_c, rw_c, *feat_args, tm=min(256, n_ctx))

        att_l = _attend(q_l, jnp.concatenate([k_l, k_c], axis=1), jnp.concatenate([v_l, v_c], axis=1), tq=tq)

        yf_c, yb_c, s_c = _wkv(r_c, vv_c, kk_c, lw_c, kd_c, bd_c, zero_state)
        yf_l, yb_l, _ = _wkv(r_l, vv_l, kk_l, lw_l, kd_l, bd_l, s_c)
        ys_l, ys_c = (yf_l, yb_l), (yf_c, yb_c)

        out_args = (p["g2"], p["w_out"], p["rk"], p["lnw"], p["lnb"], ones_rw, p["router_t"])
        x1, h2, aff = _outproj(x, conv_l, att_l, ys_l[0], ys_l[1], r_l, vv_l, g_l, kd_l,
                               lat[2], lat[3], lat[4], *out_args, tm=tm)
        streams = [(x1, h2, aff, lat[5])]
        if update_ctx:
            att_c = _attend(q_c, k_c, v_c, tq=tq_c)
            xc1, hc2, aff_c = _outproj(xc, conv_c, att_c, ys_c[0], ys_c[1], r_c, vv_c, g_c, kd_c,
                                       cmod[2], cmod[3], cmod[4], *out_args, tm=tm_c)
            streams.append((xc1, hc2, aff_c, cmod[5]))
        outs = _moe(streams, exp_w_gate, exp_w_up, exp_w_down, l)
        x = outs[0]
        if update_ctx:
            xc = outs[1]
    return x
```

```python
import functools

import jax
import jax.numpy as jnp
from jax import lax
from jax.experimental import pallas as pl
from jax.experimental.pallas import tpu as pltpu

F32 = jnp.float32
BF16 = jnp.bfloat16
I32 = jnp.int32
HIGHEST = lax.Precision.HIGHEST

HEAD = 64
N_Q_HEADS = 8
N_KV_HEADS = 2
CONV_W = 256
RW = 256
RW_HEADS = RW // HEAD
DECAY_LORA = 64
AAA_LORA = 64
GATE_LORA = 160
N_MOD = 6
N_EXPERTS = 16
CAPACITY_FACTOR = 2
GRID_W = 64
ROPE_THETA = 10000.0
NORM_EPS = 1e-6
GN_EPS = 64e-5
Q_SCALE = 1.4426950408889634 * HEAD ** -0.5

CONV_COLS = 3 * CONV_W
Q_COLS = N_Q_HEADS * HEAD
KV_COLS = N_KV_HEADS * HEAD
ATT_COLS = Q_COLS + 2 * KV_COLS
RW_COLS = 3 * RW + 2 * (DECAY_LORA + AAA_LORA) + GATE_LORA
LANES = 128
SUBLANES = 8
RW_COLS_PAD = -(-RW_COLS // LANES) * LANES
GATE_LORA_PAD = RW_COLS_PAD - (3 * RW + 2 * (DECAY_LORA + AAA_LORA))
PROJ_PAD = CONV_COLS + ATT_COLS + RW_COLS_PAD

WKV_CHUNK = 64
VMEM_LIMIT = 56 * 1024 * 1024
WKV_PRECISION = dict(score="bf16", inverse="bf16", apply="bf16", state="bf16")

Q_HEAD_ORDER = (0, 4, 1, 5, 2, 6, 3, 7)


def _cparams(*sem):
    return pltpu.CompilerParams(dimension_semantics=sem, vmem_limit_bytes=VMEM_LIMIT)


def _dot(a, b, precision=None):
    return jnp.dot(a, b, preferred_element_type=F32, precision=precision)


def _dot_nt(a, b, precision=None):
    return lax.dot_general(a, b, (((1,), (1,)), ((), ())), preferred_element_type=F32, precision=precision)


def _dot_tn(a, b, precision=None):
    return lax.dot_general(a, b, (((0,), (0,)), ((), ())), preferred_element_type=F32, precision=precision)


def _split_bf16(x):
    hi = x.astype(BF16)
    return hi, (x - hi.astype(F32)).astype(BF16)


_NN = ((1,), (0,))
_NT = ((1,), (1,))
_TN = ((0,), (0,))


def _contract(a, b, dims, mode):
    dn = (dims, ((), ()))
    if mode == "f32":
        return lax.dot_general(a, b, dn, preferred_element_type=F32, precision=HIGHEST)
    if mode == "bf16":
        return lax.dot_general(a.astype(BF16), b.astype(BF16), dn, preferred_element_type=F32)
    ah, al = _split_bf16(a)
    bh, bl = _split_bf16(b)
    dot = lambda x, y: lax.dot_general(x, y, dn, preferred_element_type=F32)
    return dot(ah, bh) + (dot(ah, bl) + dot(al, bh))


def _sigmoid(x):
    return 1.0 / (1.0 + jnp.exp(-x))


def _group_sum(t, ones_bd):
    hi = t.astype(BF16)
    lo = (t - hi.astype(F32)).astype(BF16)
    return _dot(hi, ones_bd) + _dot(lo, ones_bd)


def _block_ones(width):
    g = jnp.arange(width) // HEAD
    return (g[:, None] == g[None, :]).astype(BF16)


def _adaln_kernel(c_ref, w_ref, b_ref, o_ref):
    c = c_ref[...]
    s = c * _sigmoid(c)
    o_ref[0] = _dot(s, w_ref[0], HIGHEST) + b_ref[0]


def _adaln(cvecs, ada_w, ada_b):
    depth, d, n = ada_w.shape
    r = cvecs.shape[0]
    tn = 1024
    return pl.pallas_call(
        _adaln_kernel,
        grid=(depth, n // tn),
        in_specs=[
            pl.BlockSpec((r, d), lambda l, j: (0, 0)),
            pl.BlockSpec((1, d, tn), lambda l, j: (l, 0, j)),
            pl.BlockSpec((1, 1, tn), lambda l, j: (l, 0, j)),
        ],
        out_specs=pl.BlockSpec((1, r, tn), lambda l, j: (l, 0, j)),
        out_shape=jax.ShapeDtypeStruct((depth, r, n), F32),
        compiler_params=_cparams("parallel", "parallel"),
        name="adaln",
    )(cvecs, ada_w, ada_b.reshape(depth, 1, n))


def _modulated_norm(x, g, shift, scale):
    ms = jnp.mean(x * x, axis=-1, keepdims=True)
    return (x * lax.rsqrt(ms + NORM_EPS) * g) * (1.0 + scale) + shift


def _head_norm(t, ones_bd, gain):
    ss = _group_sum(t * t, ones_bd)
    return t * lax.rsqrt(ss * (1.0 / HEAD) + NORM_EPS) * gain


def _rope(t, cos, sin_signed):
    width = t.shape[1]
    lane = lax.broadcasted_iota(I32, t.shape, 1)
    first_half = (lane & (HEAD - 1)) < HEAD // 2
    swapped = jnp.where(first_half, pltpu.roll(t, width - HEAD // 2, 1), pltpu.roll(t, HEAD // 2, 1))
    reps = width // LANES
    if reps > 1:
        cos = jnp.concatenate([cos] * reps, axis=1)
        sin_signed = jnp.concatenate([sin_signed] * reps, axis=1)
    return t * cos + swapped * sin_signed


def _inproj_kernel(x_ref, sh_ref, sc_ref, g_ref, w_ref, bd_ref, qg_ref, kg_ref, cos_ref, sin_ref,
                   pc_ref, q_ref, k_ref, v_ref, rw_ref, *, rope):
    h = _modulated_norm(x_ref[0], g_ref[...], sh_ref[0], sc_ref[0])
    p = _dot(h.astype(BF16), w_ref[...])
    pc_ref[0] = p[:, :CONV_COLS]
    q = p[:, CONV_COLS:CONV_COLS + Q_COLS]
    k = p[:, CONV_COLS + Q_COLS:CONV_COLS + Q_COLS + KV_COLS]
    v = p[:, CONV_COLS + Q_COLS + KV_COLS:CONV_COLS + ATT_COLS]
    q = _head_norm(q, bd_ref[...], qg_ref[...])
    k = _head_norm(k, bd_ref[:KV_COLS, :KV_COLS], kg_ref[...])
    if rope:
        q = _rope(q, cos_ref[...], sin_ref[...])
        k = _rope(k, cos_ref[...], sin_ref[...])
    q_ref[0] = (q * Q_SCALE).astype(BF16)
    k_ref[0] = k.astype(BF16)
    v_ref[0] = v.astype(BF16)
    rw_ref[0] = p[:, CONV_COLS + ATT_COLS:]


def _inproj(x, shift, scale, g, w_in, ones_q, qg, kg, cos, sin, *, rope, tm):
    bsz, n, d = x.shape
    mod_map = (lambda b, i: (b, 0, 0)) if shift.shape[0] == bsz else (lambda b, i: (0, 0, 0))
    const2 = lambda b, i: (0, 0)
    tile = lambda w: pl.BlockSpec((1, tm, w), lambda b, i: (b, i, 0))
    return pl.pallas_call(
        functools.partial(_inproj_kernel, rope=rope),
        grid=(bsz, n // tm),
        in_specs=[
            tile(d),
            pl.BlockSpec((1, 1, d), mod_map),
            pl.BlockSpec((1, 1, d), mod_map),
            pl.BlockSpec((1, d), const2),
            pl.BlockSpec((d, PROJ_PAD), const2),
            pl.BlockSpec((Q_COLS, Q_COLS), const2),
            pl.BlockSpec((1, Q_COLS), const2),
            pl.BlockSpec((1, KV_COLS), const2),
            pl.BlockSpec((tm, LANES), lambda b, i: (i, 0)),
            pl.BlockSpec((tm, LANES), lambda b, i: (i, 0)),
        ],
        out_specs=[tile(CONV_COLS), tile(Q_COLS), tile(KV_COLS), tile(KV_COLS), tile(RW_COLS_PAD)],
        out_shape=[
            jax.ShapeDtypeStruct((bsz, n, CONV_COLS), F32),
            jax.ShapeDtypeStruct((bsz, n, Q_COLS), BF16),
            jax.ShapeDtypeStruct((bsz, n, KV_COLS), BF16),
            jax.ShapeDtypeStruct((bsz, n, KV_COLS), BF16),
            jax.ShapeDtypeStruct((bsz, n, RW_COLS_PAD), F32),
        ],
        compiler_params=_cparams("parallel", "parallel"),
        name="inproj",
    )(x, shift, scale, g, w_in, ones_q, qg, kg, cos, sin)


def _shifted(x, prev_row, next_row):
    rows = lax.broadcasted_iota(I32, x.shape, 0)
    tm = x.shape[0]
    before = jnp.where(rows == 0, prev_row, pltpu.roll(x, 1, 0))
    after = jnp.where(rows == tm - 1, next_row, pltpu.roll(x, tm - 1, 0))
    return before, after


def _features_kernel(pc_ref, pc_prev_ref, pc_next_ref, rw_ref, rw_prev_ref, rw_next_ref,
                     conv_w_ref, mu_ref, w0_ref, wb_ref, a0_ref, ab_ref, gb_ref, kk_ref, ka_ref, bd_ref,
                     conv_ref, r_ref, v_ref, kkn_ref, g_ref, lw_ref, kd_ref, bdir_ref):
    i = pl.program_id(1)
    has_prev = (i > 0).astype(F32)
    has_next = (i < pl.num_programs(1) - 1).astype(F32)

    pc = pc_ref[0]
    pcp = pc_prev_ref[0][SUBLANES - 1:SUBLANES, :] * has_prev
    pcn = pc_next_ref[0][0:1, :] * has_next
    z = pc[:, CONV_W:2 * CONV_W] * pc[:, 2 * CONV_W:]
    zp = pcp[:, CONV_W:2 * CONV_W] * pcp[:, 2 * CONV_W:]
    zn = pcn[:, CONV_W:2 * CONV_W] * pcn[:, 2 * CONV_W:]
    zb, za = _shifted(z, zp, zn)
    cw = conv_w_ref[...]
    conv_ref[0] = pc[:, :CONV_W] * (cw[0:1] * zb + cw[1:2] * z + cw[2:3] * za)

    p = rw_ref[0]
    pb, pa = _shifted(p, rw_prev_ref[0][SUBLANES - 1:SUBLANES, :] * has_prev, rw_next_ref[0][0:1, :] * has_next)
    p = p + mu_ref[...] * (0.5 * (pb + pa) - p)
    r = p[:, :RW]
    k = p[:, RW:2 * RW]
    v = p[:, 2 * RW:3 * RW]
    o = 3 * RW
    wl = p[:, o:o + 2 * DECAY_LORA]
    al = p[:, o + 2 * DECAY_LORA:o + 2 * (DECAY_LORA + AAA_LORA)]
    gl = p[:, o + 2 * (DECAY_LORA + AAA_LORA):]
    wz = -(w0_ref[...] + _contract(jnp.tanh(wl), wb_ref[...], _NN, "bf16"))
    softplus = jnp.maximum(wz, 0.0) + jnp.log(1.0 + jnp.exp(-jnp.abs(wz)))
    lw_ref[0] = -jnp.exp(-softplus - 0.5)
    a = _sigmoid(a0_ref[...] + _contract(al, ab_ref[...], _NN, "bf16"))
    kk = k * kk_ref[...]
    kk = kk * lax.rsqrt(jnp.maximum(_group_sum(kk * kk, bd_ref[...]), 1e-24))
    k2 = jnp.concatenate([k, k], axis=1)
    kk2 = jnp.concatenate([kk, kk], axis=1)
    ka2 = jnp.concatenate([ka_ref[...], ka_ref[...]], axis=1)
    kd_ref[0] = k2 * (1.0 + (a - 1.0) * ka2)
    bdir_ref[0] = kk2 * a
    r_ref[0] = r
    v_ref[0] = v
    kkn_ref[0] = kk
    g_ref[0] = _contract(_sigmoid(gl), gb_ref[...], _NN, "bf16")


def _features(pc, prw, conv_w, mu, w0, wb, a0, ab, gb, kk, ka, ones_rw, *, tm):
    bsz, n, _ = pc.shape
    nt = n // tm
    r8 = tm // SUBLANES
    tile = lambda w: pl.BlockSpec((1, tm, w), lambda b, i: (b, i, 0))
    prev = lambda w: pl.BlockSpec((1, SUBLANES, w), lambda b, i: (b, jnp.maximum(i * r8 - 1, 0), 0))
    nxt = lambda w: pl.BlockSpec((1, SUBLANES, w), lambda b, i: (b, jnp.minimum((i + 1) * r8, nt * r8 - 1), 0))
    full = lambda a: pl.BlockSpec(a.shape, lambda b, i: (0,) * a.ndim)
    params = (conv_w, mu, w0, wb, a0, ab, gb, kk, ka, ones_rw)
    widths = (CONV_W, RW, RW, RW, RW, 2 * RW, 2 * RW, 2 * RW)
    return pl.pallas_call(
        _features_kernel,
        grid=(bsz, nt),
        in_specs=[tile(CONV_COLS), prev(CONV_COLS), nxt(CONV_COLS),
                  tile(RW_COLS_PAD), prev(RW_COLS_PAD), nxt(RW_COLS_PAD)] + [full(a) for a in params],
        out_specs=[tile(w) for w in widths],
        out_shape=[jax.ShapeDtypeStruct((bsz, n, w), F32) for w in widths],
        compiler_params=_cparams("parallel", "parallel"),
        name="features",
    )(pc, pc, pc, prw, prw, prw, *params)


def _attend_kernel(q_ref, k_ref, v_ref, o_ref):
    k = k_ref[0]
    v = v_ref[0]
    lane = lax.broadcasted_iota(I32, (q_ref.shape[1], LANES), 1)
    low = lane < HEAD
    key_low = lax.broadcasted_iota(I32, v.shape, 1) < HEAD
    ones = jnp.ones_like(v)
    v_sides = {True: jnp.where(key_low, v, ones), False: jnp.where(key_low, ones, v)}
    for c in range(Q_COLS // LANES):
        qc = q_ref[0, :, c * LANES:(c + 1) * LANES]
        outs = []
        for use_low in (True, False):
            qm = jnp.where(low == use_low, qc, jnp.zeros_like(qc))
            s = _dot_nt(qm, k)
            p = jnp.exp2(s - jnp.max(s, axis=-1, keepdims=True))
            o = _dot(p.astype(BF16), v_sides[use_low])
            outs.append(o / pltpu.roll(o, HEAD, 1))
        o_ref[0, :, c * LANES:(c + 1) * LANES] = jnp.where(low, outs[0], outs[1]).astype(BF16)


def _attend(q, k, v, *, tq):
    bsz, n, _ = q.shape
    nk = k.shape[1]
    return pl.pallas_call(
        _attend_kernel,
        grid=(bsz, n // tq),
        in_specs=[
            pl.BlockSpec((1, tq, Q_COLS), lambda b, i: (b, i, 0)),
            pl.BlockSpec((1, nk, KV_COLS), lambda b, i: (b, 0, 0)),
            pl.BlockSpec((1, nk, KV_COLS), lambda b, i: (b, 0, 0)),
        ],
        out_specs=pl.BlockSpec((1, tq, Q_COLS), lambda b, i: (b, i, 0)),
        out_shape=jax.ShapeDtypeStruct((bsz, n, Q_COLS), BF16),
        compiler_params=_cparams("parallel", "parallel"),
        name="attend",
    )(q, k, v)


def _head_blocks(x, mode):
    if mode == "bf16":
        x = x.astype(BF16)
    head = lax.broadcasted_iota(I32, x.shape, 1) >> 6
    return jnp.concatenate([jnp.where(head == h, x, jnp.zeros_like(x)) for h in range(RW_HEADS)], axis=0)


def _wkv_chunks(chains, prec):
    c = WKV_CHUNK
    ps, pi, pa, pst = prec["score"], prec["inverse"], prec["apply"], prec["state"]
    ri = lax.broadcasted_iota(I32, (c, c), 0)
    ci = lax.broadcasted_iota(I32, (c, c), 1)
    row = lax.broadcasted_iota(I32, (c, RW_HEADS * c), 0)
    col = lax.broadcasted_iota(I32, (c, RW_HEADS * c), 1) & (c - 1)
    eye = (col == row).astype(F32)
    masks = {}
    for reverse in sorted({ch[7] for ch in chains}):
        masks[reverse] = (((ci >= ri) if reverse else (ci <= ri)).astype(BF16),
                          (col > row) if reverse else (col < row),
                          (col >= row) if reverse else (col <= row))

    pre = []
    for r, v, kk, lw, kd, bd, s, reverse in chains:
        tri = masks[reverse][0]
        lw_hi, lw_lo = _split_bf16(lw)
        lw_mid, lw_lo = _split_bf16(lw - lw_hi.astype(F32))
        cum = _dot(tri, lw_hi) + (_dot(tri, lw_mid) + _dot(tri, lw_lo))
        total = jnp.sum(lw, axis=0, keepdims=True)
        inv = jnp.exp(-cum)
        rem = jnp.exp(total - cum)
        pre.append(dict(a_t=-kk * jnp.exp(cum - lw), r_t=r * jnp.exp(cum), b_t=bd * inv, k_t=kd * inv,
                        b_end=bd * rem, k_end=kd * rem, decay=jnp.exp(total)))

    scores = []
    for ch, q in zip(chains, pre):
        _, strict, incl = masks[ch[7]]
        ar = jnp.concatenate([q["a_t"], q["r_t"]], axis=0)
        sb = _contract(ar, _head_blocks(q["b_t"], ps), _NT, ps)
        sk = _contract(ar, _head_blocks(q["k_t"], ps), _NT, ps)
        scores.append(dict(a_ab=jnp.where(strict, sb[:c], 0.0), a_rb=jnp.where(incl, sb[c:], 0.0),
                           a_ak=jnp.where(strict, sk[:c], 0.0), a_rk=jnp.where(incl, sk[c:], 0.0)))

    t_inv = [eye + sc["a_ab"] for sc in scores]
    m = [_contract(sc["a_ab"], _head_blocks(sc["a_ab"], pi), _NN, pi) for sc in scores]
    for _ in range((c - 1).bit_length() - 2):
        both = [_contract(jnp.concatenate([x, t], axis=0), _head_blocks(x, pi), _NN, pi) for x, t in zip(m, t_inv)]
        m = [x[:c] for x in both]
        t_inv = [t + x[c:] for t, x in zip(t_inv, both)]
    t_inv = [t + _contract(t, _head_blocks(x, pi), _NN, pi) for t, x in zip(t_inv, m)]

    av = [_contract(jnp.concatenate([sc["a_ak"], sc["a_rk"]], axis=0), _head_blocks(ch[1], pa), _NN, pa)
          for ch, sc in zip(chains, scores)]
    w1 = [_contract(t, _head_blocks(q["a_t"], pa), _NN, pa) for t, q in zip(t_inv, pre)]
    w2 = [_contract(t, _head_blocks(x[:c], pa), _NN, pa) for t, x in zip(t_inv, av)]
    g = [_contract(jnp.concatenate([x, q["r_t"]], axis=0), ch[6], _NT, pst) for x, q, ch in zip(w1, pre, chains)]
    u = [x[:c] + y for x, y in zip(g, w2)]
    ys = [x[c:] + _contract(sc["a_rb"], _head_blocks(uu, pa), _NN, pa) + y[c:]
          for x, sc, uu, y in zip(g, scores, u, av)]
    upd = [_contract(jnp.concatenate([uu, ch[1]], axis=0), jnp.concatenate([q["b_end"], q["k_end"]], axis=0), _TN, pst)
           for uu, ch, q in zip(u, chains, pre)]
    hr = lax.broadcasted_iota(I32, (RW, RW), 0) >> 6
    hc = lax.broadcasted_iota(I32, (RW, RW), 1) >> 6
    s_new = [ch[6] * q["decay"] + jnp.where(hr == hc, x, 0.0) for ch, q, x in zip(chains, pre, upd)]
    return ys, s_new


def _wkv_kernel(rf_ref, vf_ref, kkf_ref, lwf_ref, kdf_ref, bdf_ref, rb_ref, vb_ref, kkb_ref, lwb_ref, kdb_ref,
                bdb_ref, s0_ref, yf_ref, yb_ref, s_out_ref, s_scr, *, prec):
    step = pl.program_id(0)

    @pl.when(step == 0)
    def _():
        s_scr[...] = s0_ref[...]

    bsz = rf_ref.shape[0]
    chains = []
    for b in range(bsz):
        chains.append((rf_ref[b], vf_ref[b], kkf_ref[b], lwf_ref[b], kdf_ref[b], bdf_ref[b], s_scr[0, b], False))
        chains.append((rb_ref[b], vb_ref[b], kkb_ref[b], lwb_ref[b], kdb_ref[b], bdb_ref[b], s_scr[1, b], True))
    ys, s_new = _wkv_chunks(chains, prec)
    for b in range(bsz):
        yf_ref[b] = ys[2 * b]
        yb_ref[b] = ys[2 * b + 1]
        s_scr[0, b] = s_new[2 * b]
        s_scr[1, b] = s_new[2 * b + 1]

    @pl.when(step == pl.num_programs(0) - 1)
    def _():
        s_out_ref[...] = s_scr[...]


def _wkv(r, v, kk, lw, kd, bd, s0, prec=None):
    prec = prec or WKV_PRECISION
    bsz, n, _ = r.shape
    steps = n // WKV_CHUNK
    blk = (bsz, WKV_CHUNK, RW)
    fwd = pl.BlockSpec(blk, lambda i: (0, i, 0))
    bwd = pl.BlockSpec(blk, lambda i: (0, steps - 1 - i, 0))
    bwd_dir = pl.BlockSpec(blk, lambda i: (0, steps - 1 - i, 1))
    state = pl.BlockSpec((2, bsz, RW, RW), lambda i: (0, 0, 0, 0))
    y_shape = jax.ShapeDtypeStruct((bsz, n, RW), F32)
    return pl.pallas_call(
        functools.partial(_wkv_kernel, prec=prec),
        grid=(steps,),
        in_specs=[fwd, fwd, fwd, fwd, fwd, fwd, bwd, bwd, bwd, bwd_dir, bwd_dir, bwd_dir, state],
        out_specs=[fwd, bwd, state],
        out_shape=[y_shape, y_shape, jax.ShapeDtypeStruct((2, bsz, RW, RW), F32)],
        scratch_shapes=[pltpu.VMEM((2, bsz, RW, RW), F32)],
        compiler_params=_cparams("arbitrary"),
        name="wkv",
    )(r, v, kk, lw, kd, bd, r, v, kk, lw, kd, bd, s0)


def _outproj_kernel(x_ref, conv_ref, att_ref, yf_ref, yb_ref, r_ref, v_ref, g_ref, kd_ref,
                    gt1_ref, sh2_ref, sc2_ref, g2_ref, w_ref, rk_ref, lnw_ref, lnb_ref, bd_ref, rwt_ref,
                    x1_ref, h2_ref, aff_ref):
    ones_bd = bd_ref[...]
    y = yf_ref[0] + yb_ref[0]
    mean = _group_sum(y, ones_bd) * (1.0 / HEAD)
    yc = y - mean
    var = _group_sum(yc * yc, ones_bd) * (1.0 / HEAD)
    yn = yc * lax.rsqrt(var + GN_EPS)
    kd = kd_ref[0]
    k_mean = 0.5 * (kd[:, :RW] + kd[:, RW:])
    bonus = _group_sum(r_ref[0] * k_mean * rk_ref[...], ones_bd) * v_ref[0]
    rwkv = (yn * lnw_ref[...] + lnb_ref[...] + bonus) * g_ref[0]

    mixed = (_dot(conv_ref[0].astype(BF16), w_ref[:CONV_W])
             + _dot(att_ref[0], w_ref[CONV_W:CONV_W + Q_COLS])
             + _dot(rwkv.astype(BF16), w_ref[CONV_W + Q_COLS:]))
    x1 = x_ref[0] + gt1_ref[0] * mixed
    x1_ref[0] = x1

    h2 = _modulated_norm(x1, g2_ref[...], sh2_ref[0], sc2_ref[0])
    h2_ref[0] = h2.astype(BF16)
    logits = _contract(rwt_ref[...], h2, _NT, "x3")
    e = jnp.exp(logits - jnp.max(logits, axis=0, keepdims=True))
    aff_ref[0] = e / jnp.sum(e, axis=0, keepdims=True)


def _outproj(x, conv, att, yf, yb, r, v, g, kd, gt1, sh2, sc2, g2, w_out, rk, lnw, lnb, ones_rw, router_t, *, tm):
    bsz, n, d = x.shape
    mod_map = (lambda b, i: (b, 0, 0)) if gt1.shape[0] == bsz else (lambda b, i: (0, 0, 0))
    tile = lambda w: pl.BlockSpec((1, tm, w), lambda b, i: (b, i, 0))
    mod = pl.BlockSpec((1, 1, d), mod_map)
    full = lambda a: pl.BlockSpec(a.shape, lambda b, i: (0,) * a.ndim)
    params = (g2, w_out, rk, lnw, lnb, ones_rw, router_t)
    return pl.pallas_call(
        _outproj_kernel,
        grid=(bsz, n // tm),
        in_specs=[tile(d), tile(CONV_W), tile(Q_COLS), tile(RW), tile(RW), tile(RW), tile(RW), tile(RW),
                  tile(2 * RW), mod, mod, mod] + [full(a) for a in params],
        out_specs=[tile(d), tile(d), pl.BlockSpec((1, N_EXPERTS, tm), lambda b, i: (b, 0, i))],
        out_shape=[
            jax.ShapeDtypeStruct((bsz, n, d), F32),
            jax.ShapeDtypeStruct((bsz, n, d), BF16),
            jax.ShapeDtypeStruct((bsz, N_EXPERTS, n), F32),
        ],
        compiler_params=_cparams("parallel", "parallel"),
        name="outproj",
    )(x, conv, att, yf, yb, r, v, g, kd, gt1, sh2, sc2, *params)


def _cumsum_lanes(x):
    n = x.shape[1]
    w = min(n, 256)
    ri = lax.broadcasted_iota(I32, (w, w), 0)
    ci = lax.broadcasted_iota(I32, (w, w), 1)
    tri = (ri <= ci).astype(BF16)
    carry = jnp.zeros((x.shape[0], 1), F32)
    parts = []
    for j in range(n // w):
        cs = _dot(x[:, j * w:(j + 1) * w].astype(BF16), tri) + carry
        carry = cs[:, w - 1:w]
        parts.append(cs)
    return parts[0] if len(parts) == 1 else jnp.concatenate(parts, axis=1)


def _to_token_major(x):
    n = x.shape[1]
    pad = jnp.zeros((LANES - x.shape[0], LANES), F32)
    blocks = [jnp.concatenate([x[:, j * LANES:(j + 1) * LANES], pad], axis=0).T for j in range(n // LANES)]
    return blocks[0] if len(blocks) == 1 else jnp.concatenate(blocks, axis=0)


def _route_kernel(aff_ref, idx_ref, aff_t_ref, *, cap, slots):
    aff = aff_ref[0]
    n_exp, n = aff.shape
    bits = pltpu.bitcast(aff, I32)

    def search(i, thr):
        cand = thr | (1 << (30 - i))
        cnt = jnp.sum((bits >= cand).astype(F32), axis=1, keepdims=True)
        return jnp.where(cnt >= cap, cand, thr)

    thr = lax.fori_loop(0, 31, search, jnp.zeros((aff.shape[0], 1), I32))
    above = bits > thr
    tied = bits == thr
    need = cap - jnp.sum(above.astype(F32), axis=1, keepdims=True)
    tied_f = tied.astype(F32)
    chosen = jnp.where(above | (tied & (_cumsum_lanes(tied_f) <= need)), 1.0, 0.0)
    rank = _cumsum_lanes(chosen)
    aff_t_ref[0] = _to_token_major(aff)
    rank_t = _to_token_major(rank)
    rows = min(n, 512)
    slot = lax.broadcasted_iota(I32, (rows, slots), 1).astype(F32)
    expert = lax.broadcasted_iota(I32, (n_exp, slots), 0)
    idx = jnp.zeros((n_exp, slots), F32)
    for e in range(n_exp):
        cnt = jnp.zeros((1, slots), F32)
        for r0 in range(0, n, rows):
            cnt = cnt + jnp.sum(jnp.where(rank_t[r0:r0 + rows, e:e + 1] <= slot, 1.0, 0.0), axis=0, keepdims=True)
        idx = jnp.where(expert == e, cnt, idx)
    idx_ref[0] = jnp.minimum(idx, n - 1.0).astype(I32)


def _route(aff, *, cap, slots):
    bsz, e, n = aff.shape
    return pl.pallas_call(
        functools.partial(_route_kernel, cap=cap, slots=slots),
        grid=(bsz,),
        in_specs=[pl.BlockSpec((1, e, n), lambda b: (b, 0, 0))],
        out_specs=[pl.BlockSpec((1, e, slots), lambda b: (b, 0, 0)),
                   pl.BlockSpec((1, n, LANES), lambda b: (b, 0, 0))],
        out_shape=[jax.ShapeDtypeStruct((bsz, e, slots), I32),
                   jax.ShapeDtypeStruct((bsz, n, LANES), F32)],
        compiler_params=_cparams("parallel"),
        name="route",
    )(aff)


ROW_GROUP = 2 * SUBLANES
RELAYOUT_ROWS = 256


def _gather_kernel(idx_ref, h_ref, aff_ref, xs_ref, gs_ref, h_rows, a_rows):
    n, d = h_ref.shape

    @pl.when(pl.program_id(1) == 0)
    def _():
        for r in range(0, n, RELAYOUT_ROWS):
            m = min(RELAYOUT_ROWS, n - r)
            h_rows[r:r + m] = h_ref[r:r + m, :].astype(F32).reshape(m, 1, d)
            a_rows[r:r + m] = aff_ref[r:r + m, :].reshape(m, 1, LANES)

    def rows(c, carry):
        base = pl.multiple_of(c * ROW_GROUP, ROW_GROUP)
        tokens = [idx_ref[0, 0, c * ROW_GROUP + u] for u in range(ROW_GROUP)]
        xs_ref[pl.ds(base, ROW_GROUP), :] = jnp.concatenate([h_rows[i] for i in tokens], axis=0).astype(BF16)
        gs_ref[pl.ds(base, ROW_GROUP), :] = jnp.concatenate([a_rows[i] for i in tokens], axis=0)
        return carry

    lax.fori_loop(0, xs_ref.shape[0] // ROW_GROUP, rows, 0)


def _gather(idx, h2, aff_t):
    bsz, n, d = h2.shape
    be, _, slots = idx.shape
    e = be // bsz
    return pl.pallas_call(
        _gather_kernel,
        grid=(bsz, e),
        in_specs=[pl.BlockSpec((1, 1, slots), lambda b, j: (b * e + j, 0, 0), memory_space=pltpu.SMEM),
                  pl.BlockSpec((None, n, d), lambda b, j: (b, 0, 0), pipeline_mode=pl.Buffered(1)),
                  pl.BlockSpec((None, n, LANES), lambda b, j: (b, 0, 0))],
        out_specs=[pl.BlockSpec((slots, d), lambda b, j: (b * e + j, 0)),
                   pl.BlockSpec((slots, LANES), lambda b, j: (b * e + j, 0))],
        out_shape=[jax.ShapeDtypeStruct((be * slots, d), BF16),
                   jax.ShapeDtypeStruct((be * slots, LANES), F32)],
        scratch_shapes=[pltpu.VMEM((n, 1, d), F32), pltpu.VMEM((n, 1, LANES), F32)],
        compiler_params=_cparams("parallel", "arbitrary"),
        name="gather",
    )(idx, h2, aff_t)


def _experts_kernel(*refs, streams):
    xs_refs, gs_refs = refs[:streams], refs[streams:2 * streams]
    wg_ref, wu_ref, wd_ref = refs[2 * streams:2 * streams + 3]
    y_refs, (wg_s, wu_s, wd_s) = refs[2 * streams + 3:3 * streams + 3], refs[3 * streams + 3:]
    e = pl.program_id(0)

    @pl.when(pl.program_id(1) == 0)
    def _():
        wg_s[...] = wg_ref[0, 0].astype(BF16)
        wu_s[...] = wu_ref[0, 0].astype(BF16)
        wd_s[...] = wd_ref[0, 0].astype(BF16)

    for xs_ref, gs_ref, y_ref in zip(xs_refs, gs_refs, y_refs):
        xs = xs_ref[...]
        hg = _dot(xs, wg_s[...])
        hu = _dot(xs, wu_s[...])
        hid = (hg * _sigmoid(hg)) * hu
        gs = gs_ref[...]
        lane = lax.broadcasted_iota(I32, gs.shape, 1)
        gate = jnp.sum(jnp.where(lane == e, gs, 0.0), axis=1, keepdims=True)
        y_ref[...] = (_dot(hid.astype(BF16), wd_s[...]) * gate).astype(BF16)


def _experts(streams, w_gate, w_up, w_down, layer, *, bsz):
    e, d, f = w_gate.shape[1], w_gate.shape[2], w_gate.shape[3]
    slot_counts = [xs.shape[0] // (bsz * e) for xs, _ in streams]
    rows = lambda s, w: pl.BlockSpec((s, w), lambda j, b: (b * e + j, 0))
    return pl.pallas_call(
        functools.partial(_experts_kernel, streams=len(streams)),
        grid=(e, bsz),
        in_specs=[rows(s, d) for s in slot_counts] + [rows(s, LANES) for s in slot_counts] + [
            pl.BlockSpec((1, 1, d, f), lambda j, b: (layer, j, 0, 0)),
            pl.BlockSpec((1, 1, d, f), lambda j, b: (layer, j, 0, 0)),
            pl.BlockSpec((1, 1, f, d), lambda j, b: (layer, j, 0, 0))],
        out_specs=[rows(s, d) for s in slot_counts],
        out_shape=[jax.ShapeDtypeStruct(xs.shape, BF16) for xs, _ in streams],
        scratch_shapes=[pltpu.VMEM((d, f), BF16), pltpu.VMEM((d, f), BF16), pltpu.VMEM((f, d), BF16)],
        compiler_params=_cparams("parallel", "arbitrary"),
        name="experts",
    )(*[xs for xs, _ in streams], *[gs for _, gs in streams], w_gate, w_up, w_down)


def _combine_kernel(idx_ref, y_ref, x_ref, gt_ref, o_ref, acc, *, cap):
    e = pl.program_id(1)
    n, d = x_ref.shape

    @pl.when(e == 0)
    def _():
        acc[...] = jnp.zeros_like(acc)

    def rows(c, carry):
        base = pl.multiple_of(c * ROW_GROUP, ROW_GROUP)
        tile = y_ref[pl.ds(base, ROW_GROUP), :].astype(F32).reshape(ROW_GROUP, 1, d)
        tokens = [idx_ref[0, 0, c * ROW_GROUP + u] for u in range(ROW_GROUP)]
        sums = [acc[i] + tile[u] for u, i in enumerate(tokens)]
        for i, s in zip(tokens, sums):
            acc[i] = s
        return carry

    lax.fori_loop(0, cap // ROW_GROUP, rows, 0)

    @pl.when(e == pl.num_programs(1) - 1)
    def _():
        for r in range(0, n, RELAYOUT_ROWS):
            m = min(RELAYOUT_ROWS, n - r)
            o_ref[r:r + m, :] = x_ref[r:r + m, :] + gt_ref[0] * acc[r:r + m].reshape(m, d)


def _combine(idx, y, x1, gt2, *, cap):
    bsz, n, d = x1.shape
    be, _, slots = idx.shape
    e = be // bsz
    mod_map = (lambda b, j: (b, 0, 0)) if gt2.shape[0] == bsz else (lambda b, j: (0, 0, 0))
    return pl.pallas_call(
        functools.partial(_combine_kernel, cap=cap),
        grid=(bsz, e),
        in_specs=[pl.BlockSpec((1, 1, slots), lambda b, j: (b * e + j, 0, 0), memory_space=pltpu.SMEM),
                  pl.BlockSpec((slots, d), lambda b, j: (b * e + j, 0)),
                  pl.BlockSpec((None, n, d), lambda b, j: (b, 0, 0), pipeline_mode=pl.Buffered(1)),
                  pl.BlockSpec((1, 1, d), mod_map)],
        out_specs=pl.BlockSpec((None, n, d), lambda b, j: (b, 0, 0), pipeline_mode=pl.Buffered(1)),
        out_shape=jax.ShapeDtypeStruct(x1.shape, F32),
        scratch_shapes=[pltpu.VMEM((n, 1, d), F32)],
        compiler_params=_cparams("parallel", "arbitrary"),
        name="combine",
    )(idx, y, x1, gt2)


def _rope_tables(n):
    rows = n // GRID_W
    row = jnp.repeat(jnp.arange(rows, dtype=F32), GRID_W)
    col = jnp.tile(jnp.arange(GRID_W, dtype=F32), rows)
    half = HEAD // 2
    inv = ROPE_THETA ** (-jnp.arange(0, half, 2, dtype=F32) / half)
    ang = jnp.concatenate([row[:, None] * inv, col[:, None] * inv], axis=-1)
    cos, sin = jnp.cos(ang), jnp.sin(ang)
    cos = jnp.concatenate([cos, cos], axis=-1)
    sin = jnp.concatenate([-sin, sin], axis=-1)
    return jnp.tile(cos, (1, LANES // HEAD)), jnp.tile(sin, (1, LANES // HEAD))


def _block_diag2(m):
    z = jnp.zeros_like(m[0])
    return jnp.concatenate([jnp.concatenate([m[0], z], axis=1), jnp.concatenate([z, m[1]], axis=1)], axis=0)


def _permute_q_heads(w, axis):
    idx = jnp.concatenate([jnp.arange(h * HEAD, (h + 1) * HEAD) for h in Q_HEAD_ORDER])
    return jnp.take(w, idx, axis=axis)


def _layer_params(l, w_in, w_out, conv_w, q_norm_g, k_norm_g, rw_mu, rw_w0, rw_w_b, rw_a0, rw_a_b, rw_g_b,
                  rw_k_k, rw_k_a, rw_r_k, rw_ln_w, rw_ln_b, router_w, norm1_g, norm2_g):
    wi = w_in[l]
    q_cols = _permute_q_heads(wi[:, CONV_COLS:CONV_COLS + Q_COLS], 1)
    wi = jnp.concatenate([wi[:, :CONV_COLS], q_cols, wi[:, CONV_COLS + Q_COLS:],
                          jnp.zeros((wi.shape[0], RW_COLS_PAD - RW_COLS), F32)], axis=1).astype(BF16)
    wo = w_out[l]
    wo = jnp.concatenate([wo[:CONV_W], _permute_q_heads(wo[CONV_W:CONV_W + Q_COLS], 0),
                          wo[CONV_W + Q_COLS:]], axis=0).astype(BF16)
    row = lambda a: a.reshape(1, -1)
    return dict(
        w_in=wi, w_out=wo, conv_w=conv_w[l],
        qg=row(jnp.tile(q_norm_g[l], N_Q_HEADS)), kg=row(jnp.tile(k_norm_g[l], N_KV_HEADS)),
        mu=row(jnp.pad(rw_mu[l], (0, RW_COLS_PAD - RW_COLS))),
        w0=row(rw_w0[l]), wb=_block_diag2(rw_w_b[l]), a0=row(rw_a0[l]), ab=_block_diag2(rw_a_b[l]),
        gb=jnp.pad(rw_g_b[l], ((0, GATE_LORA_PAD - GATE_LORA), (0, 0))),
        kk=row(rw_k_k[l]), ka=row(rw_k_a[l]), rk=row(rw_r_k[l]), lnw=row(rw_ln_w[l]), lnb=row(rw_ln_b[l]),
        router_t=router_w[l].T, g1=row(norm1_g[l]), g2=row(norm2_g[l]),
    )


def _moe(streams, w_gate, w_up, w_down, layer):
    bsz, _, d = streams[0][0].shape
    routed = []
    for x1, h2, aff, _ in streams:
        n = x1.shape[1]
        cap = CAPACITY_FACTOR * n // N_EXPERTS
        slots = -(-cap // LANES) * LANES
        idx, aff_t = _route(aff, cap=cap, slots=slots)
        idx = idx.reshape(bsz * N_EXPERTS, 1, slots)
        routed.append((idx, cap) + tuple(_gather(idx, h2, aff_t)))
    ys = _experts([(r[2], r[3]) for r in routed], w_gate, w_up, w_down, layer, bsz=bsz)
    return [_combine(idx, y, x1, gt2, cap=cap) for (x1, _, _, gt2), (idx, cap, _, _), y in zip(streams, routed, ys)]


def kernel(x, c, ctx, c_ctx, ada_w, ada_b, norm1_g, norm2_g, w_in, w_out, conv_w, q_norm_g, k_norm_g, rw_mu,
           rw_w0, rw_w_b, rw_a0, rw_a_b, rw_g_b, rw_k_k, rw_k_a, rw_r_k, rw_ln_w, rw_ln_b, router_w,
           exp_w_gate, exp_w_up, exp_w_down):
    bsz, n, d = x.shape
    n_ctx = ctx.shape[1]
    depth = ada_w.shape[0]
    tm = min(512, n)
    tm_c = min(512, n_ctx)
    tq = min(256, n)
    tq_c = min(256, n_ctx)

    rows = -(-(bsz + 1) // SUBLANES) * SUBLANES
    cvecs = jnp.concatenate([c, c_ctx[None], jnp.zeros((rows - bsz - 1, d), F32)], axis=0)
    mod = _adaln(cvecs, ada_w, ada_b).reshape(depth, rows, N_MOD, d)

    cos, sin = _rope_tables(n)
    cos_c = jnp.ones((n_ctx, LANES), F32)
    sin_c = jnp.zeros((n_ctx, LANES), F32)
    ones_q = _block_ones(Q_COLS)
    ones_rw = _block_ones(RW)
    zero_state = jnp.zeros((2, bsz, RW, RW), F32)

    xc = ctx
    for l in range(depth):
        update_ctx = l < depth - 1
        p = _layer_params(l, w_in, w_out, conv_w, q_norm_g, k_norm_g, rw_mu, rw_w0, rw_w_b, rw_a0, rw_a_b, rw_g_b,
                          rw_k_k, rw_k_a, rw_r_k, rw_ln_w, rw_ln_b, router_w, norm1_g, norm2_g)
        lat = [mod[l, :bsz, i][:, None, :] for i in range(N_MOD)]
        cmod = [mod[l, bsz:bsz + 1, i][:, None, :] for i in range(N_MOD)]

        pc_l, q_l, k_l, v_l, rw_l = _inproj(x, lat[0], lat[1], p["g1"], p["w_in"], ones_q, p["qg"], p["kg"],
                                            cos, sin, rope=True, tm=tm)
        pc_c, q_c, k_c, v_c, rw_c = _inproj(xc, cmod[0], cmod[1], p["g1"], p["w_in"], ones_q, p["qg"], p["kg"],
                                            cos_c, sin_c, rope=False, tm=tm_c)
        feat_args = (p["conv_w"], p["mu"], p["w0"], p["wb"], p["a0"], p["ab"], p["gb"], p["kk"], p["ka"], ones_rw)
        conv_l, r_l, vv_l, kk_l, g_l, lw_l, kd_l, bd_l = _features(pc_l, rw_l, *feat_args, tm=min(256, n))
        conv_c, r_c, vv_c, kk_c, g_c, lw_c, kd_c, bd_c = _features(pc_c, rw_c, *feat_args, tm=min(256, n_ctx))

        att_l = _attend(q_l, jnp.concatenate([k_l, k_c], axis=1), jnp.concatenate([v_l, v_c], axis=1), tq=tq)

        yf_c, yb_c, s_c = _wkv(r_c, vv_c, kk_c, lw_c, kd_c, bd_c, zero_state)
        yf_l, yb_l, _ = _wkv(r_l, vv_l, kk_l, lw_l, kd_l, bd_l, s_c)
        ys_l, ys_c = (yf_l, yb_l), (yf_c, yb_c)

        out_args = (p["g2"], p["w_out"], p["rk"], p["lnw"], p["lnb"], ones_rw, p["router_t"])
        x1, h2, aff = _outproj(x, conv_l, att_l, ys_l[0], ys_l[1], r_l, vv_l, g_l, kd_l,
                               lat[2], lat[3], lat[4], *out_args, tm=tm)
        streams = [(x1, h2, aff, lat[5])]
        if update_ctx:
            att_c = _attend(q_c, k_c, v_c, tq=tq_c)
            xc1, hc2, aff_c = _outproj(xc, conv_c, att_c, ys_c[0], ys_c[1], r_c, vv_c, g_c, kd_c,
                                       cmod[2], cmod[3], cmod[4], *out_args, tm=tm_c)
            streams.append((xc1, hc2, aff_c, cmod[5]))
        outs = _moe(streams, exp_w_gate, exp_w_up, exp_w_down, l)
        x = outs[0]
        if update_ctx:
            xc = outs[1]
    return x
```

```python
import functools

import jax
import jax.numpy as jnp
from jax import lax
from jax.experimental import pallas as pl
from jax.experimental.pallas import tpu as pltpu

F32 = jnp.float32
BF16 = jnp.bfloat16
I32 = jnp.int32
HIGHEST = lax.Precision.HIGHEST

HEAD = 64
N_Q_HEADS = 8
N_KV_HEADS = 2
CONV_W = 256
RW = 256
RW_HEADS = RW // HEAD
DECAY_LORA = 64
AAA_LORA = 64
GATE_LORA = 160
N_MOD = 6
N_EXPERTS = 16
CAPACITY_FACTOR = 2
GRID_W = 64
ROPE_THETA = 10000.0
NORM_EPS = 1e-6
GN_EPS = 64e-5
Q_SCALE = 1.4426950408889634 * HEAD ** -0.5

CONV_COLS = 3 * CONV_W
Q_COLS = N_Q_HEADS * HEAD
KV_COLS = N_KV_HEADS * HEAD
ATT_COLS = Q_COLS + 2 * KV_COLS
RW_COLS = 3 * RW + 2 * (DECAY_LORA + AAA_LORA) + GATE_LORA
LANES = 128
SUBLANES = 8
RW_COLS_PAD = -(-RW_COLS // LANES) * LANES
GATE_LORA_PAD = RW_COLS_PAD - (3 * RW + 2 * (DECAY_LORA + AAA_LORA))
PROJ_PAD = CONV_COLS + ATT_COLS + RW_COLS_PAD

WKV_CHUNK = 64
WKV_GROUP_BATCHES = 8
VMEM_LIMIT = 56 * 1024 * 1024
WKV_PRECISION = dict(score="bf16", inverse="bf16", apply="bf16", state="bf16")

Q_HEAD_ORDER = (0, 4, 1, 5, 2, 6, 3, 7)


def _cparams(*sem):
    return pltpu.CompilerParams(dimension_semantics=sem, vmem_limit_bytes=VMEM_LIMIT)


def _dot(a, b, precision=None):
    return jnp.dot(a, b, preferred_element_type=F32, precision=precision)


def _dot_nt(a, b, precision=None):
    return lax.dot_general(a, b, (((1,), (1,)), ((), ())), preferred_element_type=F32, precision=precision)


def _dot_tn(a, b, precision=None):
    return lax.dot_general(a, b, (((0,), (0,)), ((), ())), preferred_element_type=F32, precision=precision)


def _split_bf16(x):
    hi = x.astype(BF16)
    return hi, (x - hi.astype(F32)).astype(BF16)


_NN = ((1,), (0,))
_NT = ((1,), (1,))
_TN = ((0,), (0,))


def _contract(a, b, dims, mode):
    dn = (dims, ((), ()))
    if mode == "f32":
        return lax.dot_general(a, b, dn, preferred_element_type=F32, precision=HIGHEST)
    if mode == "bf16":
        return lax.dot_general(a.astype(BF16), b.astype(BF16), dn, preferred_element_type=F32)
    ah, al = _split_bf16(a)
    bh, bl = _split_bf16(b)
    dot = lambda x, y: lax.dot_general(x, y, dn, preferred_element_type=F32)
    return dot(ah, bh) + (dot(ah, bl) + dot(al, bh))


def _sigmoid(x):
    return 1.0 / (1.0 + jnp.exp(-x))


def _group_sum(t, ones_bd):
    hi = t.astype(BF16)
    lo = (t - hi.astype(F32)).astype(BF16)
    return _dot(hi, ones_bd) + _dot(lo, ones_bd)


def _block_ones(width):
    g = jnp.arange(width) // HEAD
    return (g[:, None] == g[None, :]).astype(BF16)


def _adaln_kernel(c_ref, w_ref, b_ref, o_ref):
    c = c_ref[...]
    s = c * _sigmoid(c)
    o_ref[0] = _dot(s, w_ref[0], HIGHEST) + b_ref[0]


def _adaln(cvecs, ada_w, ada_b):
    depth, d, n = ada_w.shape
    r = cvecs.shape[0]
    tn = 1024
    return pl.pallas_call(
        _adaln_kernel,
        grid=(depth, n // tn),
        in_specs=[
            pl.BlockSpec((r, d), lambda l, j: (0, 0)),
            pl.BlockSpec((1, d, tn), lambda l, j: (l, 0, j)),
            pl.BlockSpec((1, 1, tn), lambda l, j: (l, 0, j)),
        ],
        out_specs=pl.BlockSpec((1, r, tn), lambda l, j: (l, 0, j)),
        out_shape=jax.ShapeDtypeStruct((depth, r, n), F32),
        compiler_params=_cparams("parallel", "parallel"),
        name="adaln",
    )(cvecs, ada_w, ada_b.reshape(depth, 1, n))


def _modulated_norm(x, g, shift, scale):
    ms = jnp.mean(x * x, axis=-1, keepdims=True)
    return (x * lax.rsqrt(ms + NORM_EPS) * g) * (1.0 + scale) + shift


def _head_norm(t, ones_bd, gain):
    ss = _group_sum(t * t, ones_bd)
    return t * lax.rsqrt(ss * (1.0 / HEAD) + NORM_EPS) * gain


def _rope(t, cos, sin_signed):
    width = t.shape[1]
    lane = lax.broadcasted_iota(I32, t.shape, 1)
    first_half = (lane & (HEAD - 1)) < HEAD // 2
    swapped = jnp.where(first_half, pltpu.roll(t, width - HEAD // 2, 1), pltpu.roll(t, HEAD // 2, 1))
    reps = width // LANES
    if reps > 1:
        cos = jnp.concatenate([cos] * reps, axis=1)
        sin_signed = jnp.concatenate([sin_signed] * reps, axis=1)
    return t * cos + swapped * sin_signed


def _inproj_kernel(x_ref, sh_ref, sc_ref, g_ref, w_ref, bd_ref, qg_ref, kg_ref, cos_ref, sin_ref,
                   pc_ref, q_ref, k_ref, v_ref, rw_ref, *, rope):
    tm = x_ref.shape[1]
    half = tm // 2 if tm % 16 == 0 else tm
    for r in range(0, tm, half):
        rows = slice(r, r + half)
        h = _modulated_norm(x_ref[0, rows], g_ref[...], sh_ref[0], sc_ref[0])
        p = _dot(h.astype(BF16), w_ref[...])
        pc_ref[0, rows] = p[:, :CONV_COLS]
        q = p[:, CONV_COLS:CONV_COLS + Q_COLS]
        k = p[:, CONV_COLS + Q_COLS:CONV_COLS + Q_COLS + KV_COLS]
        v = p[:, CONV_COLS + Q_COLS + KV_COLS:CONV_COLS + ATT_COLS]
        q = _head_norm(q, bd_ref[...], qg_ref[...])
        k = _head_norm(k, bd_ref[:KV_COLS, :KV_COLS], kg_ref[...])
        if rope:
            q = _rope(q, cos_ref[rows], sin_ref[rows])
            k = _rope(k, cos_ref[rows], sin_ref[rows])
        q_ref[0, rows] = (q * Q_SCALE).astype(BF16)
        k_ref[0, rows] = k.astype(BF16)
        v_ref[0, rows] = v.astype(BF16)
        rw_ref[0, rows] = p[:, CONV_COLS + ATT_COLS:]


def _inproj(x, shift, scale, g, w_in, ones_q, qg, kg, cos, sin, *, rope, tm):
    bsz, n, d = x.shape
    mod_map = (lambda b, i: (b, 0, 0)) if shift.shape[0] == bsz else (lambda b, i: (0, 0, 0))
    const2 = lambda b, i: (0, 0)
    tile = lambda w: pl.BlockSpec((1, tm, w), lambda b, i: (b, i, 0))
    return pl.pallas_call(
        functools.partial(_inproj_kernel, rope=rope),
        grid=(bsz, n // tm),
        in_specs=[
            tile(d),
            pl.BlockSpec((1, 1, d), mod_map),
            pl.BlockSpec((1, 1, d), mod_map),
            pl.BlockSpec((1, d), const2),
            pl.BlockSpec((d, PROJ_PAD), const2),
            pl.BlockSpec((Q_COLS, Q_COLS), const2),
            pl.BlockSpec((1, Q_COLS), const2),
            pl.BlockSpec((1, KV_COLS), const2),
            pl.BlockSpec((tm, LANES), lambda b, i: (i, 0)),
            pl.BlockSpec((tm, LANES), lambda b, i: (i, 0)),
        ],
        out_specs=[tile(CONV_COLS), tile(Q_COLS), tile(KV_COLS), tile(KV_COLS), tile(RW_COLS_PAD)],
        out_shape=[
            jax.ShapeDtypeStruct((bsz, n, CONV_COLS), F32),
            jax.ShapeDtypeStruct((bsz, n, Q_COLS), BF16),
            jax.ShapeDtypeStruct((bsz, n, KV_COLS), BF16),
            jax.ShapeDtypeStruct((bsz, n, KV_COLS), BF16),
            jax.ShapeDtypeStruct((bsz, n, RW_COLS_PAD), F32),
        ],
        compiler_params=_cparams("parallel", "parallel"),
        name="inproj",
    )(x, shift, scale, g, w_in, ones_q, qg, kg, cos, sin)


def _shifted(x, prev_row, next_row):
    rows = lax.broadcasted_iota(I32, x.shape, 0)
    tm = x.shape[0]
    before = jnp.where(rows == 0, prev_row, pltpu.roll(x, 1, 0))
    after = jnp.where(rows == tm - 1, next_row, pltpu.roll(x, tm - 1, 0))
    return before, after


def _features_kernel(pc_ref, pc_prev_ref, pc_next_ref, rw_ref, rw_prev_ref, rw_next_ref,
                     conv_w_ref, mu_ref, w0_ref, wb_ref, a0_ref, ab_ref, gb_ref, kk_ref, ka_ref, bd_ref,
                     conv_ref, r_ref, v_ref, kkn_ref, g_ref, lw_ref, kd_ref, bdir_ref):
    i = pl.program_id(1)
    has_prev = (i > 0).astype(F32)
    has_next = (i < pl.num_programs(1) - 1).astype(F32)

    pc = pc_ref[0]
    pcp = pc_prev_ref[0][SUBLANES - 1:SUBLANES, :] * has_prev
    pcn = pc_next_ref[0][0:1, :] * has_next
    z = pc[:, CONV_W:2 * CONV_W] * pc[:, 2 * CONV_W:]
    zp = pcp[:, CONV_W:2 * CONV_W] * pcp[:, 2 * CONV_W:]
    zn = pcn[:, CONV_W:2 * CONV_W] * pcn[:, 2 * CONV_W:]
    zb, za = _shifted(z, zp, zn)
    cw = conv_w_ref[...]
    conv_ref[0] = pc[:, :CONV_W] * (cw[0:1] * zb + cw[1:2] * z + cw[2:3] * za)

    p = rw_ref[0]
    pb, pa = _shifted(p, rw_prev_ref[0][SUBLANES - 1:SUBLANES, :] * has_prev, rw_next_ref[0][0:1, :] * has_next)
    p = p + mu_ref[...] * (0.5 * (pb + pa) - p)
    r = p[:, :RW]
    k = p[:, RW:2 * RW]
    v = p[:, 2 * RW:3 * RW]
    o = 3 * RW
    wl = p[:, o:o + 2 * DECAY_LORA]
    al = p[:, o + 2 * DECAY_LORA:o + 2 * (DECAY_LORA + AAA_LORA)]
    gl = p[:, o + 2 * (DECAY_LORA + AAA_LORA):]
    wz = -(w0_ref[...] + _contract(jnp.tanh(wl), wb_ref[...], _NN, "bf16"))
    softplus = jnp.maximum(wz, 0.0) + jnp.log(1.0 + jnp.exp(-jnp.abs(wz)))
    lw_ref[0] = -jnp.exp(-softplus - 0.5)
    a = _sigmoid(a0_ref[...] + _contract(al, ab_ref[...], _NN, "bf16"))
    kk = k * kk_ref[...]
    kk = kk * lax.rsqrt(jnp.maximum(_group_sum(kk * kk, bd_ref[...]), 1e-24))
    k2 = jnp.concatenate([k, k], axis=1)
    kk2 = jnp.concatenate([kk, kk], axis=1)
    ka2 = jnp.concatenate([ka_ref[...], ka_ref[...]], axis=1)
    kd_ref[0] = k2 * (1.0 + (a - 1.0) * ka2)
    bdir_ref[0] = kk2 * a
    r_ref[0] = r
    v_ref[0] = v
    kkn_ref[0] = kk
    g_ref[0] = _contract(_sigmoid(gl), gb_ref[...], _NN, "bf16")


def _features(pc, prw, conv_w, mu, w0, wb, a0, ab, gb, kk, ka, ones_rw, *, tm):
    bsz, n, _ = pc.shape
    nt = n // tm
    r8 = tm // SUBLANES
    tile = lambda w: pl.BlockSpec((1, tm, w), lambda b, i: (b, i, 0))
    prev = lambda w: pl.BlockSpec((1, SUBLANES, w), lambda b, i: (b, jnp.maximum(i * r8 - 1, 0), 0))
    nxt = lambda w: pl.BlockSpec((1, SUBLANES, w), lambda b, i: (b, jnp.minimum((i + 1) * r8, nt * r8 - 1), 0))
    full = lambda a: pl.BlockSpec(a.shape, lambda b, i: (0,) * a.ndim)
    params = (conv_w, mu, w0, wb, a0, ab, gb, kk, ka, ones_rw)
    widths = (CONV_W, RW, RW, RW, RW, 2 * RW, 2 * RW, 2 * RW)
    return pl.pallas_call(
        _features_kernel,
        grid=(bsz, nt),
        in_specs=[tile(CONV_COLS), prev(CONV_COLS), nxt(CONV_COLS),
                  tile(RW_COLS_PAD), prev(RW_COLS_PAD), nxt(RW_COLS_PAD)] + [full(a) for a in params],
        out_specs=[tile(w) for w in widths],
        out_shape=[jax.ShapeDtypeStruct((bsz, n, w), F32) for w in widths],
        compiler_params=_cparams("parallel", "parallel"),
        name="features",
    )(pc, pc, pc, prw, prw, prw, *params)


def _attend_kernel(q_ref, k_ref, v_ref, o_ref):
    k = k_ref[0]
    v = v_ref[0]
    lane = lax.broadcasted_iota(I32, (q_ref.shape[1], LANES), 1)
    low = lane < HEAD
    key_low = lax.broadcasted_iota(I32, v.shape, 1) < HEAD
    ones = jnp.ones_like(v)
    v_sides = {True: jnp.where(key_low, v, ones), False: jnp.where(key_low, ones, v)}
    for c in range(Q_COLS // LANES):
        qc = q_ref[0, :, c * LANES:(c + 1) * LANES]
        outs = []
        for use_low in (True, False):
            qm = jnp.where(low == use_low, qc, jnp.zeros_like(qc))
            s = _dot_nt(qm, k)
            p = jnp.exp2(s - jnp.max(s, axis=-1, keepdims=True))
            o = _dot(p.astype(BF16), v_sides[use_low])
            outs.append(o / pltpu.roll(o, HEAD, 1))
        o_ref[0, :, c * LANES:(c + 1) * LANES] = jnp.where(low, outs[0], outs[1]).astype(BF16)


def _attend(q, k, v, *, tq):
    bsz, n, _ = q.shape
    nk = k.shape[1]
    return pl.pallas_call(
        _attend_kernel,
        grid=(bsz, n // tq),
        in_specs=[
            pl.BlockSpec((1, tq, Q_COLS), lambda b, i: (b, i, 0)),
            pl.BlockSpec((1, nk, KV_COLS), lambda b, i: (b, 0, 0)),
            pl.BlockSpec((1, nk, KV_COLS), lambda b, i: (b, 0, 0)),
        ],
        out_specs=pl.BlockSpec((1, tq, Q_COLS), lambda b, i: (b, i, 0)),
        out_shape=jax.ShapeDtypeStruct((bsz, n, Q_COLS), BF16),
        compiler_params=_cparams("parallel", "parallel"),
        name="attend",
    )(q, k, v)


def _head_blocks(x, mode):
    if mode == "bf16":
        x = x.astype(BF16)
    head = lax.broadcasted_iota(I32, x.shape, 1) >> 6
    return jnp.concatenate([jnp.where(head == h, x, jnp.zeros_like(x)) for h in range(RW_HEADS)], axis=0)


def _wkv_chunks(chains, prec):
    c = WKV_CHUNK
    ps, pi, pa, pst = prec["score"], prec["inverse"], prec["apply"], prec["state"]
    ri = lax.broadcasted_iota(I32, (c, c), 0)
    ci = lax.broadcasted_iota(I32, (c, c), 1)
    row = lax.broadcasted_iota(I32, (c, RW_HEADS * c), 0)
    col = lax.broadcasted_iota(I32, (c, RW_HEADS * c), 1) & (c - 1)
    eye = (col == row).astype(F32)
    masks = {}
    for reverse in sorted({ch[7] for ch in chains}):
        masks[reverse] = (((ci >= ri) if reverse else (ci <= ri)).astype(BF16),
                          (col > row) if reverse else (col < row),
                          (col >= row) if reverse else (col <= row))

    pre = []
    for r, v, kk, lw, kd, bd, s, reverse in chains:
        tri = masks[reverse][0]
        lw_hi, lw_lo = _split_bf16(lw)
        lw_mid, lw_lo = _split_bf16(lw - lw_hi.astype(F32))
        cum = _dot(tri, lw_hi) + (_dot(tri, lw_mid) + _dot(tri, lw_lo))
        total = jnp.sum(lw, axis=0, keepdims=True)
        inv = jnp.exp(-cum)
        rem = jnp.exp(total - cum)
        pre.append(dict(a_t=-kk * jnp.exp(cum - lw), r_t=r * jnp.exp(cum), b_t=bd * inv, k_t=kd * inv,
                        b_end=bd * rem, k_end=kd * rem, decay=jnp.exp(total)))

    scores = []
    for ch, q in zip(chains, pre):
        _, strict, incl = masks[ch[7]]
        ar = jnp.concatenate([q["a_t"], q["r_t"]], axis=0)
        sb = _contract(ar, _head_blocks(q["b_t"], ps), _NT, ps)
        sk = _contract(ar, _head_blocks(q["k_t"], ps), _NT, ps)
        scores.append(dict(a_ab=jnp.where(strict, sb[:c], 0.0), a_rb=jnp.where(incl, sb[c:], 0.0),
                           a_ak=jnp.where(strict, sk[:c], 0.0), a_rk=jnp.where(incl, sk[c:], 0.0)))

    t_inv = [eye + sc["a_ab"] for sc in scores]
    m = [_contract(sc["a_ab"], _head_blocks(sc["a_ab"], pi), _NN, pi) for sc in scores]
    for _ in range((c - 1).bit_length() - 2):
        both = [_contract(jnp.concatenate([x, t], axis=0), _head_blocks(x, pi), _NN, pi) for x, t in zip(m, t_inv)]
        m = [x[:c] for x in both]
        t_inv = [t + x[c:] for t, x in zip(t_inv, both)]
    t_inv = [t + _contract(t, _head_blocks(x, pi), _NN, pi) for t, x in zip(t_inv, m)]

    av = [_contract(jnp.concatenate([sc["a_ak"], sc["a_rk"]], axis=0), _head_blocks(ch[1], pa), _NN, pa)
          for ch, sc in zip(chains, scores)]
    w1 = [_contract(t, _head_blocks(q["a_t"], pa), _NN, pa) for t, q in zip(t_inv, pre)]
    w2 = [_contract(t, _head_blocks(x[:c], pa), _NN, pa) for t, x in zip(t_inv, av)]
    g = [_contract(jnp.concatenate([x, q["r_t"]], axis=0), ch[6], _NT, pst) for x, q, ch in zip(w1, pre, chains)]
    u = [x[:c] + y for x, y in zip(g, w2)]
    ys = [x[c:] + _contract(sc["a_rb"], _head_blocks(uu, pa), _NN, pa) + y[c:]
          for x, sc, uu, y in zip(g, scores, u, av)]
    upd = [_contract(jnp.concatenate([uu, ch[1]], axis=0), jnp.concatenate([q["b_end"], q["k_end"]], axis=0), _TN, pst)
           for uu, ch, q in zip(u, chains, pre)]
    hr = lax.broadcasted_iota(I32, (RW, RW), 0) >> 6
    hc = lax.broadcasted_iota(I32, (RW, RW), 1) >> 6
    s_new = [ch[6] * q["decay"] + jnp.where(hr == hc, x, 0.0) for ch, q, x in zip(chains, pre, upd)]
    return ys, s_new


def _wkv_kernel(rf_ref, vf_ref, kkf_ref, lwf_ref, kdf_ref, bdf_ref, rb_ref, vb_ref, kkb_ref, lwb_ref, kdb_ref,
                bdb_ref, s0_ref, yf_ref, yb_ref, s_out_ref, s_scr, *, prec):
    step = pl.program_id(0)

    @pl.when(step == 0)
    def _():
        s_scr[...] = s0_ref[...]

    bsz = rf_ref.shape[0]
    for b0 in range(0, bsz, WKV_GROUP_BATCHES):
        group = range(b0, min(b0 + WKV_GROUP_BATCHES, bsz))
        chains = []
        for b in group:
            chains.append((rf_ref[b], vf_ref[b], kkf_ref[b], lwf_ref[b], kdf_ref[b], bdf_ref[b], s_scr[0, b], False))
            chains.append((rb_ref[b], vb_ref[b], kkb_ref[b], lwb_ref[b], kdb_ref[b], bdb_ref[b], s_scr[1, b], True))
        ys, s_new = _wkv_chunks(chains, prec)
        for i, b in enumerate(group):
            yf_ref[b] = ys[2 * i]
            yb_ref[b] = ys[2 * i + 1]
            s_scr[0, b] = s_new[2 * i]
            s_scr[1, b] = s_new[2 * i + 1]

    @pl.when(step == pl.num_programs(0) - 1)
    def _():
        s_out_ref[...] = s_scr[...]


def _wkv(r, v, kk, lw, kd, bd, s0, prec=None):
    prec = prec or WKV_PRECISION
    bsz, n, _ = r.shape
    steps = n // WKV_CHUNK
    blk = (bsz, WKV_CHUNK, RW)
    fwd = pl.BlockSpec(blk, lambda i: (0, i, 0))
    bwd = pl.BlockSpec(blk, lambda i: (0, steps - 1 - i, 0))
    bwd_dir = pl.BlockSpec(blk, lambda i: (0, steps - 1 - i, 1))
    state = pl.BlockSpec((2, bsz, RW, RW), lambda i: (0, 0, 0, 0))
    y_shape = jax.ShapeDtypeStruct((bsz, n, RW), F32)
    return pl.pallas_call(
        functools.partial(_wkv_kernel, prec=prec),
        grid=(steps,),
        in_specs=[fwd, fwd, fwd, fwd, fwd, fwd, bwd, bwd, bwd, bwd_dir, bwd_dir, bwd_dir, state],
        out_specs=[fwd, bwd, state],
        out_shape=[y_shape, y_shape, jax.ShapeDtypeStruct((2, bsz, RW, RW), F32)],
        scratch_shapes=[pltpu.VMEM((2, bsz, RW, RW), F32)],
        compiler_params=_cparams("arbitrary"),
        name="wkv",
    )(r, v, kk, lw, kd, bd, r, v, kk, lw, kd, bd, s0)


def _outproj_kernel(x_ref, conv_ref, att_ref, yf_ref, yb_ref, r_ref, v_ref, g_ref, kd_ref,
                    gt1_ref, sh2_ref, sc2_ref, g2_ref, w_ref, rk_ref, lnw_ref, lnb_ref, bd_ref, rwt_ref,
                    x1_ref, h2_ref, aff_ref):
    ones_bd = bd_ref[...]
    y = yf_ref[0] + yb_ref[0]
    mean = _group_sum(y, ones_bd) * (1.0 / HEAD)
    yc = y - mean
    var = _group_sum(yc * yc, ones_bd) * (1.0 / HEAD)
    yn = yc * lax.rsqrt(var + GN_EPS)
    kd = kd_ref[0]
    k_mean = 0.5 * (kd[:, :RW] + kd[:, RW:])
    bonus = _group_sum(r_ref[0] * k_mean * rk_ref[...], ones_bd) * v_ref[0]
    rwkv = (yn * lnw_ref[...] + lnb_ref[...] + bonus) * g_ref[0]

    mixed = (_dot(conv_ref[0].astype(BF16), w_ref[:CONV_W])
             + _dot(att_ref[0], w_ref[CONV_W:CONV_W + Q_COLS])
             + _dot(rwkv.astype(BF16), w_ref[CONV_W + Q_COLS:]))
    x1 = x_ref[0] + gt1_ref[0] * mixed
    x1_ref[0] = x1

    h2 = _modulated_norm(x1, g2_ref[...], sh2_ref[0], sc2_ref[0])
    h2_ref[0] = h2.astype(BF16)
    logits = _contract(rwt_ref[...], h2, _NT, "x3")
    e = jnp.exp(logits - jnp.max(logits, axis=0, keepdims=True))
    aff_ref[0] = e / jnp.sum(e, axis=0, keepdims=True)


def _outproj(x, conv, att, yf, yb, r, v, g, kd, gt1, sh2, sc2, g2, w_out, rk, lnw, lnb, ones_rw, router_t, *, tm):
    bsz, n, d = x.shape
    mod_map = (lambda b, i: (b, 0, 0)) if gt1.shape[0] == bsz else (lambda b, i: (0, 0, 0))
    tile = lambda w: pl.BlockSpec((1, tm, w), lambda b, i: (b, i, 0))
    mod = pl.BlockSpec((1, 1, d), mod_map)
    full = lambda a: pl.BlockSpec(a.shape, lambda b, i: (0,) * a.ndim)
    params = (g2, w_out, rk, lnw, lnb, ones_rw, router_t)
    return pl.pallas_call(
        _outproj_kernel,
        grid=(bsz, n // tm),
        in_specs=[tile(d), tile(CONV_W), tile(Q_COLS), tile(RW), tile(RW), tile(RW), tile(RW), tile(RW),
                  tile(2 * RW), mod, mod, mod] + [full(a) for a in params],
        out_specs=[tile(d), tile(d), pl.BlockSpec((1, N_EXPERTS, tm), lambda b, i: (b, 0, i))],
        out_shape=[
            jax.ShapeDtypeStruct((bsz, n, d), F32),
            jax.ShapeDtypeStruct((bsz, n, d), BF16),
            jax.ShapeDtypeStruct((bsz, N_EXPERTS, n), F32),
        ],
        compiler_params=_cparams("parallel", "parallel"),
        name="outproj",
    )(x, conv, att, yf, yb, r, v, g, kd, gt1, sh2, sc2, *params)


def _cumsum_lanes(x):
    n = x.shape[1]
    w = min(n, 256)
    ri = lax.broadcasted_iota(I32, (w, w), 0)
    ci = lax.broadcasted_iota(I32, (w, w), 1)
    tri = (ri <= ci).astype(BF16)
    carry = jnp.zeros((x.shape[0], 1), F32)
    parts = []
    for j in range(n // w):
        cs = _dot(x[:, j * w:(j + 1) * w].astype(BF16), tri) + carry
        carry = cs[:, w - 1:w]
        parts.append(cs)
    return parts[0] if len(parts) == 1 else jnp.concatenate(parts, axis=1)


def _to_token_major(x):
    n = x.shape[1]
    pad = jnp.zeros((LANES - x.shape[0], LANES), F32)
    blocks = [jnp.concatenate([x[:, j * LANES:(j + 1) * LANES], pad], axis=0).T for j in range(n // LANES)]
    return blocks[0] if len(blocks) == 1 else jnp.concatenate(blocks, axis=0)


def _route_kernel(aff_ref, idx_ref, aff_t_ref, *, cap, slots):
    aff = aff_ref[0]
    n_exp, n = aff.shape
    bits = pltpu.bitcast(aff, I32)

    def search(i, thr):
        cand = thr | (1 << (30 - i))
        cnt = jnp.sum((bits >= cand).astype(F32), axis=1, keepdims=True)
        return jnp.where(cnt >= cap, cand, thr)

    thr = lax.fori_loop(0, 31, search, jnp.zeros((aff.shape[0], 1), I32))
    above = bits > thr
    tied = bits == thr
    need = cap - jnp.sum(above.astype(F32), axis=1, keepdims=True)
    tied_f = tied.astype(F32)
    chosen = jnp.where(above | (tied & (_cumsum_lanes(tied_f) <= need)), 1.0, 0.0)
    rank = _cumsum_lanes(chosen)
    aff_t_ref[0] = _to_token_major(aff)
    rank_t = _to_token_major(rank)
    rows = min(n, 512)
    slot = lax.broadcasted_iota(I32, (rows, slots), 1).astype(F32)
    expert = lax.broadcasted_iota(I32, (n_exp, slots), 0)
    idx = jnp.zeros((n_exp, slots), F32)
    for e in range(n_exp):
        cnt = jnp.zeros((1, slots), F32)
        for r0 in range(0, n, rows):
            cnt = cnt + jnp.sum(jnp.where(rank_t[r0:r0 + rows, e:e + 1] <= slot, 1.0, 0.0), axis=0, keepdims=True)
        idx = jnp.where(expert == e, cnt, idx)
    idx_ref[0] = jnp.minimum(idx, n - 1.0).astype(I32)


def _route(aff, *, cap, slots):
    bsz, e, n = aff.shape
    return pl.pallas_call(
        functools.partial(_route_kernel, cap=cap, slots=slots),
        grid=(bsz,),
        in_specs=[pl.BlockSpec((1, e, n), lambda b: (b, 0, 0))],
        out_specs=[pl.BlockSpec((1, e, slots), lambda b: (b, 0, 0)),
                   pl.BlockSpec((1, n, LANES), lambda b: (b, 0, 0))],
        out_shape=[jax.ShapeDtypeStruct((bsz, e, slots), I32),
                   jax.ShapeDtypeStruct((bsz, n, LANES), F32)],
        compiler_params=_cparams("parallel"),
        name="route",
    )(aff)


ROW_GROUP = 2 * SUBLANES
RELAYOUT_ROWS = 256
MOVER_ROWS = 1024


def _gather_kernel(idx_ref, h_ref, aff_ref, xs_ref, gs_ref, h_rows, a_rows):
    n, d = h_ref.shape

    @pl.when(pl.program_id(1) == 0)
    def _():
        for r in range(0, n, RELAYOUT_ROWS):
            m = min(RELAYOUT_ROWS, n - r)
            h_rows[r:r + m] = h_ref[r:r + m, :].astype(F32).reshape(m, 1, d)
            a_rows[r:r + m] = aff_ref[r:r + m, :].reshape(m, 1, LANES)

    def rows(c, carry):
        base = pl.multiple_of(c * ROW_GROUP, ROW_GROUP)
        tokens = [idx_ref[0, 0, c * ROW_GROUP + u] for u in range(ROW_GROUP)]
        xs_ref[pl.ds(base, ROW_GROUP), :] = jnp.concatenate([h_rows[i] for i in tokens], axis=0).astype(BF16)
        gs_ref[pl.ds(base, ROW_GROUP), :] = jnp.concatenate([a_rows[i] for i in tokens], axis=0)
        return carry

    lax.fori_loop(0, xs_ref.shape[0] // ROW_GROUP, rows, 0)


def _gather(idx, h2, aff_t):
    bsz, n, d = h2.shape
    be, _, slots = idx.shape
    e = be // bsz
    return pl.pallas_call(
        _gather_kernel,
        grid=(bsz, e),
        in_specs=[pl.BlockSpec((1, 1, slots), lambda b, j: (b * e + j, 0, 0), memory_space=pltpu.SMEM),
                  pl.BlockSpec((None, n, d), lambda b, j: (b, 0, 0), pipeline_mode=pl.Buffered(1)),
                  pl.BlockSpec((None, n, LANES), lambda b, j: (b, 0, 0))],
        out_specs=[pl.BlockSpec((slots, d), lambda b, j: (b * e + j, 0)),
                   pl.BlockSpec((slots, LANES), lambda b, j: (b * e + j, 0))],
        out_shape=[jax.ShapeDtypeStruct((be * slots, d), BF16),
                   jax.ShapeDtypeStruct((be * slots, LANES), F32)],
        scratch_shapes=[pltpu.VMEM((n, 1, d), F32), pltpu.VMEM((n, 1, LANES), F32)],
        compiler_params=_cparams("parallel", "arbitrary"),
        name="gather",
    )(idx, h2, aff_t)


def _experts_kernel(*refs, streams):
    xs_refs, gs_refs = refs[:streams], refs[streams:2 * streams]
    wg_ref, wu_ref, wd_ref = refs[2 * streams:2 * streams + 3]
    y_refs, (wg_s, wu_s, wd_s) = refs[2 * streams + 3:3 * streams + 3], refs[3 * streams + 3:]
    e = pl.program_id(0)

    @pl.when(pl.program_id(1) == 0)
    def _():
        wg_s[...] = wg_ref[0, 0].astype(BF16)
        wu_s[...] = wu_ref[0, 0].astype(BF16)
        wd_s[...] = wd_ref[0, 0].astype(BF16)

    for xs_ref, gs_ref, y_ref in zip(xs_refs, gs_refs, y_refs):
        xs = xs_ref[...]
        hg = _dot(xs, wg_s[...])
        hu = _dot(xs, wu_s[...])
        hid = (hg * _sigmoid(hg)) * hu
        gs = gs_ref[...]
        lane = lax.broadcasted_iota(I32, gs.shape, 1)
        gate = jnp.sum(jnp.where(lane == e, gs, 0.0), axis=1, keepdims=True)
        y_ref[...] = (_dot(hid.astype(BF16), wd_s[...]) * gate).astype(BF16)


def _experts(streams, w_gate, w_up, w_down, layer, *, bsz):
    e, d, f = w_gate.shape[1], w_gate.shape[2], w_gate.shape[3]
    slot_counts = [xs.shape[0] // (bsz * e) for xs, _ in streams]
    rows = lambda s, w: pl.BlockSpec((s, w), lambda j, b: (b * e + j, 0))
    return pl.pallas_call(
        functools.partial(_experts_kernel, streams=len(streams)),
        grid=(e, bsz),
        in_specs=[rows(s, d) for s in slot_counts] + [rows(s, LANES) for s in slot_counts] + [
            pl.BlockSpec((1, 1, d, f), lambda j, b: (layer, j, 0, 0)),
            pl.BlockSpec((1, 1, d, f), lambda j, b: (layer, j, 0, 0)),
            pl.BlockSpec((1, 1, f, d), lambda j, b: (layer, j, 0, 0))],
        out_specs=[rows(s, d) for s in slot_counts],
        out_shape=[jax.ShapeDtypeStruct(xs.shape, BF16) for xs, _ in streams],
        scratch_shapes=[pltpu.VMEM((d, f), BF16), pltpu.VMEM((d, f), BF16), pltpu.VMEM((f, d), BF16)],
        compiler_params=_cparams("parallel", "arbitrary"),
        name="experts",
    )(*[xs for xs, _ in streams], *[gs for _, gs in streams], w_gate, w_up, w_down)


def _combine_kernel(idx_ref, y_ref, x_ref, gt_ref, o_ref, acc, *, cap, slots):
    e = pl.program_id(1)
    n, d = x_ref.shape

    @pl.when(e == 0)
    def _():
        acc[...] = jnp.zeros_like(acc)

    for first in range(0, y_ref.shape[0], slots):
        def rows(c, carry, first=first):
            base = pl.multiple_of(first + c * ROW_GROUP, ROW_GROUP)
            tile = y_ref[pl.ds(base, ROW_GROUP), :].astype(F32).reshape(ROW_GROUP, 1, d)
            tokens = [idx_ref[0, 0, first + c * ROW_GROUP + u] for u in range(ROW_GROUP)]
            sums = [acc[i] + tile[u] for u, i in enumerate(tokens)]
            for i, s in zip(tokens, sums):
                acc[i] = s
            return carry

        lax.fori_loop(0, cap // ROW_GROUP, rows, 0)

    @pl.when(e == pl.num_programs(1) - 1)
    def _():
        for r in range(0, n, RELAYOUT_ROWS):
            m = min(RELAYOUT_ROWS, n - r)
            o_ref[r:r + m, :] = x_ref[r:r + m, :] + gt_ref[0] * acc[r:r + m].reshape(m, d)


def _combine(idx, y, x1, gt2, *, cap, slots):
    bsz, n, d = x1.shape
    be, _, rows = idx.shape
    e = be // bsz
    mod_map = (lambda b, j: (b, 0, 0)) if gt2.shape[0] == bsz else (lambda b, j: (0, 0, 0))
    return pl.pallas_call(
        functools.partial(_combine_kernel, cap=cap, slots=slots),
        grid=(bsz, e),
        in_specs=[pl.BlockSpec((1, 1, rows), lambda b, j: (b * e + j, 0, 0), memory_space=pltpu.SMEM),
                  pl.BlockSpec((rows, d), lambda b, j: (b * e + j, 0)),
                  pl.BlockSpec((None, n, d), lambda b, j: (b, 0, 0), pipeline_mode=pl.Buffered(1)),
                  pl.BlockSpec((1, 1, d), mod_map)],
        out_specs=pl.BlockSpec((None, n, d), lambda b, j: (b, 0, 0), pipeline_mode=pl.Buffered(1)),
        out_shape=jax.ShapeDtypeStruct(x1.shape, F32),
        scratch_shapes=[pltpu.VMEM((n, 1, d), F32)],
        compiler_params=_cparams("parallel", "arbitrary"),
        name="combine",
    )(idx, y, x1, gt2)


def _rope_tables(n):
    rows = n // GRID_W
    row = jnp.repeat(jnp.arange(rows, dtype=F32), GRID_W)
    col = jnp.tile(jnp.arange(GRID_W, dtype=F32), rows)
    half = HEAD // 2
    inv = ROPE_THETA ** (-jnp.arange(0, half, 2, dtype=F32) / half)
    ang = jnp.concatenate([row[:, None] * inv, col[:, None] * inv], axis=-1)
    cos, sin = jnp.cos(ang), jnp.sin(ang)
    cos = jnp.concatenate([cos, cos], axis=-1)
    sin = jnp.concatenate([-sin, sin], axis=-1)
    return jnp.tile(cos, (1, LANES // HEAD)), jnp.tile(sin, (1, LANES // HEAD))


def _block_diag2(m):
    z = jnp.zeros_like(m[0])
    return jnp.concatenate([jnp.concatenate([m[0], z], axis=1), jnp.concatenate([z, m[1]], axis=1)], axis=0)


def _permute_q_heads(w, axis):
    idx = jnp.concatenate([jnp.arange(h * HEAD, (h + 1) * HEAD) for h in Q_HEAD_ORDER])
    return jnp.take(w, idx, axis=axis)


def _layer_params(l, w_in, w_out, conv_w, q_norm_g, k_norm_g, rw_mu, rw_w0, rw_w_b, rw_a0, rw_a_b, rw_g_b,
                  rw_k_k, rw_k_a, rw_r_k, rw_ln_w, rw_ln_b, router_w, norm1_g, norm2_g):
    wi = w_in[l]
    q_cols = _permute_q_heads(wi[:, CONV_COLS:CONV_COLS + Q_COLS], 1)
    wi = jnp.concatenate([wi[:, :CONV_COLS], q_cols, wi[:, CONV_COLS + Q_COLS:],
                          jnp.zeros((wi.shape[0], RW_COLS_PAD - RW_COLS), F32)], axis=1).astype(BF16)
    wo = w_out[l]
    wo = jnp.concatenate([wo[:CONV_W], _permute_q_heads(wo[CONV_W:CONV_W + Q_COLS], 0),
                          wo[CONV_W + Q_COLS:]], axis=0).astype(BF16)
    row = lambda a: a.reshape(1, -1)
    return dict(
        w_in=wi, w_out=wo, conv_w=conv_w[l],
        qg=row(jnp.tile(q_norm_g[l], N_Q_HEADS)), kg=row(jnp.tile(k_norm_g[l], N_KV_HEADS)),
        mu=row(jnp.pad(rw_mu[l], (0, RW_COLS_PAD - RW_COLS))),
        w0=row(rw_w0[l]), wb=_block_diag2(rw_w_b[l]), a0=row(rw_a0[l]), ab=_block_diag2(rw_a_b[l]),
        gb=jnp.pad(rw_g_b[l], ((0, GATE_LORA_PAD - GATE_LORA), (0, 0))),
        kk=row(rw_k_k[l]), ka=row(rw_k_a[l]), rk=row(rw_r_k[l]), lnw=row(rw_ln_w[l]), lnb=row(rw_ln_b[l]),
        router_t=router_w[l].T, g1=row(norm1_g[l]), g2=row(norm2_g[l]),
    )


def _moe(streams, w_gate, w_up, w_down, layer):
    bsz, _, d = streams[0][0].shape
    routed = []
    for x1, h2, aff, _ in streams:
        n = x1.shape[1]
        cap = CAPACITY_FACTOR * n // N_EXPERTS
        slots = -(-cap // LANES) * LANES
        idx, aff_t = _route(aff, cap=cap, slots=slots)
        group = N_EXPERTS if N_EXPERTS * slots <= 2 * MOVER_ROWS else max(1, MOVER_ROWS // slots)
        idx = idx.reshape(bsz * N_EXPERTS // group, 1, group * slots)
        routed.append((idx, cap, slots) + tuple(_gather(idx, h2, aff_t)))
    ys = _experts([(r[3], r[4]) for r in routed], w_gate, w_up, w_down, layer, bsz=bsz)
    return [_combine(idx, y, x1, gt2, cap=cap, slots=slots)
            for (x1, _, _, gt2), (idx, cap, slots, _, _), y in zip(streams, routed, ys)]


def kernel(x, c, ctx, c_ctx, ada_w, ada_b, norm1_g, norm2_g, w_in, w_out, conv_w, q_norm_g, k_norm_g, rw_mu,
           rw_w0, rw_w_b, rw_a0, rw_a_b, rw_g_b, rw_k_k, rw_k_a, rw_r_k, rw_ln_w, rw_ln_b, router_w,
           exp_w_gate, exp_w_up, exp_w_down):
    bsz, n, d = x.shape
    n_ctx = ctx.shape[1]
    depth = ada_w.shape[0]
    tm = min(512, n)
    tm_c = min(512, n_ctx)
    tq = min(256, n)
    tq_c = min(256, n_ctx)

    rows = -(-(bsz + 1) // SUBLANES) * SUBLANES
    cvecs = jnp.concatenate([c, c_ctx[None], jnp.zeros((rows - bsz - 1, d), F32)], axis=0)
    mod = _adaln(cvecs, ada_w, ada_b).reshape(depth, rows, N_MOD, d)

    cos, sin = _rope_tables(n)
    cos_c = jnp.ones((n_ctx, LANES), F32)
    sin_c = jnp.zeros((n_ctx, LANES), F32)
    ones_q = _block_ones(Q_COLS)
    ones_rw = _block_ones(RW)
    zero_state = jnp.zeros((2, bsz, RW, RW), F32)

    xc = ctx
    for l in range(depth):
        update_ctx = l < depth - 1
        p = _layer_params(l, w_in, w_out, conv_w, q_norm_g, k_norm_g, rw_mu, rw_w0, rw_w_b, rw_a0, rw_a_b, rw_g_b,
                          rw_k_k, rw_k_a, rw_r_k, rw_ln_w, rw_ln_b, router_w, norm1_g, norm2_g)
        lat = [mod[l, :bsz, i][:, None, :] for i in range(N_MOD)]
        cmod = [mod[l, bsz:bsz + 1, i][:, None, :] for i in range(N_MOD)]

        pc_l, q_l, k_l, v_l, rw_l = _inproj(x, lat[0], lat[1], p["g1"], p["w_in"], ones_q, p["qg"], p["kg"],
                                            cos, sin, rope=True, tm=tm)
        pc_c, q_c, k_c, v_c, rw_c = _inproj(xc, cmod[0], cmod[1], p["g1"], p["w_in"], ones_q, p["qg"], p["kg"],
                                            cos_c, sin_c, rope=False, tm=tm_c)
        feat_args = (p["conv_w"], p["mu"], p["w0"], p["wb"], p["a0"], p["ab"], p["gb"], p["kk"], p["ka"], ones_rw)
        conv_l, r_l, vv_l, kk_l, g_l, lw_l, kd_l, bd_l = _features(pc_l, rw_l, *feat_args, tm=min(256, n))
        conv_c, r_c, vv_c, kk_c, g_c, lw_c, kd_c, bd_c = _features(pc_c, rw_c, *feat_args, tm=min(256, n_ctx))

        att_l = _attend(q_l, jnp.concatenate([k_l, k_c], axis=1), jnp.concatenate([v_l, v_c], axis=1), tq=tq)

        yf_c, yb_c, s_c = _wkv(r_c, vv_c, kk_c, lw_c, kd_c, bd_c, zero_state)
        yf_l, yb_l, _ = _wkv(r_l, vv_l, kk_l, lw_l, kd_l, bd_l, s_c)
        ys_l, ys_c = (yf_l, yb_l), (yf_c, yb_c)

        out_args = (p["g2"], p["w_out"], p["rk"], p["lnw"], p["lnb"], ones_rw, p["router_t"])
        x1, h2, aff = _outproj(x, conv_l, att_l, ys_l[0], ys_l[1], r_l, vv_l, g_l, kd_l,
                               lat[2], lat[3], lat[4], *out_args, tm=tm)
        streams = [(x1, h2, aff, lat[5])]
        if update_ctx:
            att_c = _attend(q_c, k_c, v_c, tq=tq_c)
            xc1, hc2, aff_c = _outproj(xc, conv_c, att_c, ys_c[0], ys_c[1], r_c, vv_c, g_c, kd_c,
                                       cmod[2], cmod[3], cmod[4], *out_args, tm=tm_c)
            streams.append((xc1, hc2, aff_c, cmod[5]))
        outs = _moe(streams, exp_w_gate, exp_w_up, exp_w_down, l)
        x = outs[0]
        if update_ctx:
            xc = outs[1]
    return x
```

```python
import functools

import jax
import jax.numpy as jnp
from jax import lax
from jax.experimental import pallas as pl
from jax.experimental.pallas import tpu as pltpu

F32 = jnp.float32
BF16 = jnp.bfloat16
I32 = jnp.int32
HIGHEST = lax.Precision.HIGHEST

HEAD = 64
N_Q_HEADS = 8
N_KV_HEADS = 2
CONV_W = 256
RW = 256
RW_HEADS = RW // HEAD
DECAY_LORA = 64
AAA_LORA = 64
GATE_LORA = 160
N_MOD = 6
N_EXPERTS = 16
CAPACITY_FACTOR = 2
GRID_W = 64
ROPE_THETA = 10000.0
NORM_EPS = 1e-6
GN_EPS = 64e-5
Q_SCALE = 1.4426950408889634 * HEAD ** -0.5

CONV_COLS = 3 * CONV_W
Q_COLS = N_Q_HEADS * HEAD
KV_COLS = N_KV_HEADS * HEAD
ATT_COLS = Q_COLS + 2 * KV_COLS
RW_COLS = 3 * RW + 2 * (DECAY_LORA + AAA_LORA) + GATE_LORA
LANES = 128
SUBLANES = 8
RW_COLS_PAD = -(-RW_COLS // LANES) * LANES
GATE_LORA_PAD = RW_COLS_PAD - (3 * RW + 2 * (DECAY_LORA + AAA_LORA))
PROJ_PAD = CONV_COLS + ATT_COLS + RW_COLS_PAD

WKV_CHUNK = 64
WKV_GROUP_BATCHES = 8
VMEM_LIMIT = 56 * 1024 * 1024
WKV_PRECISION = dict(score="bf16", inverse="bf16", apply="bf16", state="bf16")

Q_HEAD_ORDER = (0, 4, 1, 5, 2, 6, 3, 7)


def _cparams(*sem):
    return pltpu.CompilerParams(dimension_semantics=sem, vmem_limit_bytes=VMEM_LIMIT)


def _dot(a, b, precision=None):
    return jnp.dot(a, b, preferred_element_type=F32, precision=precision)


def _dot_nt(a, b, precision=None):
    return lax.dot_general(a, b, (((1,), (1,)), ((), ())), preferred_element_type=F32, precision=precision)


def _dot_tn(a, b, precision=None):
    return lax.dot_general(a, b, (((0,), (0,)), ((), ())), preferred_element_type=F32, precision=precision)


def _split_bf16(x):
    hi = x.astype(BF16)
    return hi, (x - hi.astype(F32)).astype(BF16)


_NN = ((1,), (0,))
_NT = ((1,), (1,))
_TN = ((0,), (0,))


def _contract(a, b, dims, mode):
    dn = (dims, ((), ()))
    if mode == "f32":
        return lax.dot_general(a, b, dn, preferred_element_type=F32, precision=HIGHEST)
    if mode == "bf16":
        return lax.dot_general(a.astype(BF16), b.astype(BF16), dn, preferred_element_type=F32)
    ah, al = _split_bf16(a)
    bh, bl = _split_bf16(b)
    dot = lambda x, y: lax.dot_general(x, y, dn, preferred_element_type=F32)
    return dot(ah, bh) + (dot(ah, bl) + dot(al, bh))


def _sigmoid(x):
    return 1.0 / (1.0 + jnp.exp(-x))


def _group_sum(t, ones_bd):
    hi = t.astype(BF16)
    lo = (t - hi.astype(F32)).astype(BF16)
    return _dot(hi, ones_bd) + _dot(lo, ones_bd)


def _block_ones(width):
    g = jnp.arange(width) // HEAD
    return (g[:, None] == g[None, :]).astype(BF16)


def _adaln_kernel(c_ref, w_ref, b_ref, o_ref):
    c = c_ref[...]
    s = c * _sigmoid(c)
    o_ref[0] = _dot(s, w_ref[0], HIGHEST) + b_ref[0]


def _adaln(cvecs, ada_w, ada_b):
    depth, d, n = ada_w.shape
    r = cvecs.shape[0]
    tn = 1024
    return pl.pallas_call(
        _adaln_kernel,
        grid=(depth, n // tn),
        in_specs=[
            pl.BlockSpec((r, d), lambda l, j: (0, 0)),
            pl.BlockSpec((1, d, tn), lambda l, j: (l, 0, j)),
            pl.BlockSpec((1, 1, tn), lambda l, j: (l, 0, j)),
        ],
        out_specs=pl.BlockSpec((1, r, tn), lambda l, j: (l, 0, j)),
        out_shape=jax.ShapeDtypeStruct((depth, r, n), F32),
        compiler_params=_cparams("parallel", "parallel"),
        name="adaln",
    )(cvecs, ada_w, ada_b.reshape(depth, 1, n))


def _modulated_norm(x, g, shift, scale):
    ms = jnp.mean(x * x, axis=-1, keepdims=True)
    return (x * lax.rsqrt(ms + NORM_EPS) * g) * (1.0 + scale) + shift


def _head_norm(t, ones_bd, gain):
    ss = _group_sum(t * t, ones_bd)
    return t * lax.rsqrt(ss * (1.0 / HEAD) + NORM_EPS) * gain


def _rope(t, cos, sin_signed):
    width = t.shape[1]
    lane = lax.broadcasted_iota(I32, t.shape, 1)
    first_half = (lane & (HEAD - 1)) < HEAD // 2
    swapped = jnp.where(first_half, pltpu.roll(t, width - HEAD // 2, 1), pltpu.roll(t, HEAD // 2, 1))
    reps = width // LANES
    if reps > 1:
        cos = jnp.concatenate([cos] * reps, axis=1)
        sin_signed = jnp.concatenate([sin_signed] * reps, axis=1)
    return t * cos + swapped * sin_signed


def _inproj_kernel(x_ref, sh_ref, sc_ref, g_ref, w_ref, bd_ref, qg_ref, kg_ref, cos_ref, sin_ref,
                   pc_ref, q_ref, k_ref, v_ref, rw_ref, *, rope):
    tm = x_ref.shape[1]
    half = tm // 2 if tm % 16 == 0 else tm
    for r in range(0, tm, half):
        rows = slice(r, r + half)
        h = _modulated_norm(x_ref[0, rows], g_ref[...], sh_ref[0], sc_ref[0])
        p = _dot(h.astype(BF16), w_ref[...])
        pc_ref[0, rows] = p[:, :CONV_COLS]
        q = p[:, CONV_COLS:CONV_COLS + Q_COLS]
        k = p[:, CONV_COLS + Q_COLS:CONV_COLS + Q_COLS + KV_COLS]
        v = p[:, CONV_COLS + Q_COLS + KV_COLS:CONV_COLS + ATT_COLS]
        q = _head_norm(q, bd_ref[...], qg_ref[...])
        k = _head_norm(k, bd_ref[:KV_COLS, :KV_COLS], kg_ref[...])
        if rope:
            q = _rope(q, cos_ref[rows], sin_ref[rows])
            k = _rope(k, cos_ref[rows], sin_ref[rows])
        q_ref[0, rows] = (q * Q_SCALE).astype(BF16)
        k_ref[0, rows] = k.astype(BF16)
        v_ref[0, rows] = v.astype(BF16)
        rw_ref[0, rows] = p[:, CONV_COLS + ATT_COLS:]


def _inproj(x, shift, scale, g, w_in, ones_q, qg, kg, cos, sin, *, rope, tm):
    bsz, n, d = x.shape
    mod_map = (lambda b, i: (b, 0, 0)) if shift.shape[0] == bsz else (lambda b, i: (0, 0, 0))
    const2 = lambda b, i: (0, 0)
    tile = lambda w: pl.BlockSpec((1, tm, w), lambda b, i: (b, i, 0))
    return pl.pallas_call(
        functools.partial(_inproj_kernel, rope=rope),
        grid=(bsz, n // tm),
        in_specs=[
            tile(d),
            pl.BlockSpec((1, 1, d), mod_map),
            pl.BlockSpec((1, 1, d), mod_map),
            pl.BlockSpec((1, d), const2),
            pl.BlockSpec((d, PROJ_PAD), const2),
            pl.BlockSpec((Q_COLS, Q_COLS), const2),
            pl.BlockSpec((1, Q_COLS), const2),
            pl.BlockSpec((1, KV_COLS), const2),
            pl.BlockSpec((tm, LANES), lambda b, i: (i, 0)),
            pl.BlockSpec((tm, LANES), lambda b, i: (i, 0)),
        ],
        out_specs=[tile(CONV_COLS), tile(Q_COLS), tile(KV_COLS), tile(KV_COLS), tile(RW_COLS_PAD)],
        out_shape=[
            jax.ShapeDtypeStruct((bsz, n, CONV_COLS), F32),
            jax.ShapeDtypeStruct((bsz, n, Q_COLS), BF16),
            jax.ShapeDtypeStruct((bsz, n, KV_COLS), BF16),
            jax.ShapeDtypeStruct((bsz, n, KV_COLS), BF16),
            jax.ShapeDtypeStruct((bsz, n, RW_COLS_PAD), F32),
        ],
        compiler_params=_cparams("parallel", "parallel"),
        name="inproj",
    )(x, shift, scale, g, w_in, ones_q, qg, kg, cos, sin)


def _shifted(x, prev_row, next_row):
    rows = lax.broadcasted_iota(I32, x.shape, 0)
    tm = x.shape[0]
    before = jnp.where(rows == 0, prev_row, pltpu.roll(x, 1, 0))
    after = jnp.where(rows == tm - 1, next_row, pltpu.roll(x, tm - 1, 0))
    return before, after


def _features_kernel(pc_ref, pc_prev_ref, pc_next_ref, rw_ref, rw_prev_ref, rw_next_ref,
                     conv_w_ref, mu_ref, w0_ref, wb_ref, a0_ref, ab_ref, gb_ref, kk_ref, ka_ref, bd_ref,
                     conv_ref, r_ref, v_ref, kkn_ref, g_ref, lw_ref, kd_ref, bdir_ref):
    i = pl.program_id(1)
    has_prev = (i > 0).astype(F32)
    has_next = (i < pl.num_programs(1) - 1).astype(F32)

    pc = pc_ref[0]
    pcp = pc_prev_ref[0][SUBLANES - 1:SUBLANES, :] * has_prev
    pcn = pc_next_ref[0][0:1, :] * has_next
    z = pc[:, CONV_W:2 * CONV_W] * pc[:, 2 * CONV_W:]
    zp = pcp[:, CONV_W:2 * CONV_W] * pcp[:, 2 * CONV_W:]
    zn = pcn[:, CONV_W:2 * CONV_W] * pcn[:, 2 * CONV_W:]
    zb, za = _shifted(z, zp, zn)
    cw = conv_w_ref[...]
    conv_ref[0] = pc[:, :CONV_W] * (cw[0:1] * zb + cw[1:2] * z + cw[2:3] * za)

    p = rw_ref[0]
    pb, pa = _shifted(p, rw_prev_ref[0][SUBLANES - 1:SUBLANES, :] * has_prev, rw_next_ref[0][0:1, :] * has_next)
    p = p + mu_ref[...] * (0.5 * (pb + pa) - p)
    r = p[:, :RW]
    k = p[:, RW:2 * RW]
    v = p[:, 2 * RW:3 * RW]
    o = 3 * RW
    wl = p[:, o:o + 2 * DECAY_LORA]
    al = p[:, o + 2 * DECAY_LORA:o + 2 * (DECAY_LORA + AAA_LORA)]
    gl = p[:, o + 2 * (DECAY_LORA + AAA_LORA):]
    wz = -(w0_ref[...] + _contract(jnp.tanh(wl), wb_ref[...], _NN, "bf16"))
    softplus = jnp.maximum(wz, 0.0) + jnp.log(1.0 + jnp.exp(-jnp.abs(wz)))
    lw_ref[0] = -jnp.exp(-softplus - 0.5)
    a = _sigmoid(a0_ref[...] + _contract(al, ab_ref[...], _NN, "bf16"))
    kk = k * kk_ref[...]
    kk = kk * lax.rsqrt(jnp.maximum(_group_sum(kk * kk, bd_ref[...]), 1e-24))
    k2 = jnp.concatenate([k, k], axis=1)
    kk2 = jnp.concatenate([kk, kk], axis=1)
    ka2 = jnp.concatenate([ka_ref[...], ka_ref[...]], axis=1)
    kd_ref[0] = k2 * (1.0 + (a - 1.0) * ka2)
    bdir_ref[0] = kk2 * a
    r_ref[0] = r
    v_ref[0] = v
    kkn_ref[0] = kk
    g_ref[0] = _contract(_sigmoid(gl), gb_ref[...], _NN, "bf16")


def _features(pc, prw, conv_w, mu, w0, wb, a0, ab, gb, kk, ka, ones_rw, *, tm):
    bsz, n, _ = pc.shape
    nt = n // tm
    r8 = tm // SUBLANES
    tile = lambda w: pl.BlockSpec((1, tm, w), lambda b, i: (b, i, 0))
    prev = lambda w: pl.BlockSpec((1, SUBLANES, w), lambda b, i: (b, jnp.maximum(i * r8 - 1, 0), 0))
    nxt = lambda w: pl.BlockSpec((1, SUBLANES, w), lambda b, i: (b, jnp.minimum((i + 1) * r8, nt * r8 - 1), 0))
    full = lambda a: pl.BlockSpec(a.shape, lambda b, i: (0,) * a.ndim)
    params = (conv_w, mu, w0, wb, a0, ab, gb, kk, ka, ones_rw)
    widths = (CONV_W, RW, RW, RW, RW, 2 * RW, 2 * RW, 2 * RW)
    return pl.pallas_call(
        _features_kernel,
        grid=(bsz, nt),
        in_specs=[tile(CONV_COLS), prev(CONV_COLS), nxt(CONV_COLS),
                  tile(RW_COLS_PAD), prev(RW_COLS_PAD), nxt(RW_COLS_PAD)] + [full(a) for a in params],
        out_specs=[tile(w) for w in widths],
        out_shape=[jax.ShapeDtypeStruct((bsz, n, w), F32) for w in widths],
        compiler_params=_cparams("parallel", "parallel"),
        name="features",
    )(pc, pc, pc, prw, prw, prw, *params)


def _attend_kernel(q_ref, k_ref, v_low_ref, v_high_ref, o_ref):
    k = k_ref[0]
    lane = lax.broadcasted_iota(I32, (q_ref.shape[1], LANES), 1)
    low = lane < HEAD
    v_sides = {True: v_low_ref[0], False: v_high_ref[0]}
    for c in range(Q_COLS // LANES):
        qc = q_ref[0, :, c * LANES:(c + 1) * LANES]
        outs = []
        for use_low in (True, False):
            qm = jnp.where(low == use_low, qc, jnp.zeros_like(qc))
            s = _dot_nt(qm, k)
            p = jnp.exp2(s - jnp.max(s, axis=-1, keepdims=True))
            o = _dot(p.astype(BF16), v_sides[use_low])
            outs.append(o / pltpu.roll(o, HEAD, 1))
        o_ref[0, :, c * LANES:(c + 1) * LANES] = jnp.where(low, outs[0], outs[1]).astype(BF16)


def _attend(q, k, v, *, tq):
    bsz, n, _ = q.shape
    nk = k.shape[1]
    ones = jnp.ones((bsz, nk, HEAD), v.dtype)
    v_low = jnp.concatenate([v[:, :, :HEAD], ones], axis=2)
    v_high = jnp.concatenate([ones, v[:, :, HEAD:]], axis=2)
    keys = pl.BlockSpec((1, nk, KV_COLS), lambda b, i: (b, 0, 0))
    return pl.pallas_call(
        _attend_kernel,
        grid=(bsz, n // tq),
        in_specs=[pl.BlockSpec((1, tq, Q_COLS), lambda b, i: (b, i, 0)), keys, keys, keys],
        out_specs=pl.BlockSpec((1, tq, Q_COLS), lambda b, i: (b, i, 0)),
        out_shape=jax.ShapeDtypeStruct((bsz, n, Q_COLS), BF16),
        compiler_params=_cparams("parallel", "parallel"),
        name="attend",
    )(q, k, v_low, v_high)


def _head_blocks(x, mode):
    if mode == "bf16":
        x = x.astype(BF16)
    head = lax.broadcasted_iota(I32, x.shape, 1) >> 6
    return jnp.concatenate([jnp.where(head == h, x, jnp.zeros_like(x)) for h in range(RW_HEADS)], axis=0)


def _wkv_chunks(chains, prec):
    c = WKV_CHUNK
    ps, pi, pa, pst = prec["score"], prec["inverse"], prec["apply"], prec["state"]
    ri = lax.broadcasted_iota(I32, (c, c), 0)
    ci = lax.broadcasted_iota(I32, (c, c), 1)
    row = lax.broadcasted_iota(I32, (c, RW_HEADS * c), 0)
    col = lax.broadcasted_iota(I32, (c, RW_HEADS * c), 1) & (c - 1)
    eye = (col == row).astype(F32)
    masks = {}
    for reverse in sorted({ch[7] for ch in chains}):
        masks[reverse] = (((ci >= ri) if reverse else (ci <= ri)).astype(BF16),
                          (col > row) if reverse else (col < row),
                          (col >= row) if reverse else (col <= row))

    pre = []
    for r, v, kk, lw, kd, bd, s, reverse in chains:
        tri = masks[reverse][0]
        lw_hi, lw_lo = _split_bf16(lw)
        lw_mid, lw_lo = _split_bf16(lw - lw_hi.astype(F32))
        cum = _dot(tri, lw_hi) + (_dot(tri, lw_mid) + _dot(tri, lw_lo))
        total = jnp.sum(lw, axis=0, keepdims=True)
        inv = jnp.exp(-cum)
        rem = jnp.exp(total - cum)
        pre.append(dict(a_t=-kk * jnp.exp(cum - lw), r_t=r * jnp.exp(cum), b_t=bd * inv, k_t=kd * inv,
                        b_end=bd * rem, k_end=kd * rem, decay=jnp.exp(total)))

    scores = []
    for ch, q in zip(chains, pre):
        _, strict, incl = masks[ch[7]]
        ar = jnp.concatenate([q["a_t"], q["r_t"]], axis=0)
        sb = _contract(ar, _head_blocks(q["b_t"], ps), _NT, ps)
        sk = _contract(ar, _head_blocks(q["k_t"], ps), _NT, ps)
        scores.append(dict(a_ab=jnp.where(strict, sb[:c], 0.0), a_rb=jnp.where(incl, sb[c:], 0.0),
                           a_ak=jnp.where(strict, sk[:c], 0.0), a_rk=jnp.where(incl, sk[c:], 0.0)))

    t_inv = [eye + sc["a_ab"] for sc in scores]
    m = [_contract(sc["a_ab"], _head_blocks(sc["a_ab"], pi), _NN, pi) for sc in scores]
    for _ in range((c - 1).bit_length() - 2):
        both = [_contract(jnp.concatenate([x, t], axis=0), _head_blocks(x, pi), _NN, pi) for x, t in zip(m, t_inv)]
        m = [x[:c] for x in both]
        t_inv = [t + x[c:] for t, x in zip(t_inv, both)]
    t_inv = [t + _contract(t, _head_blocks(x, pi), _NN, pi) for t, x in zip(t_inv, m)]

    av = [_contract(jnp.concatenate([sc["a_ak"], sc["a_rk"]], axis=0), _head_blocks(ch[1], pa), _NN, pa)
          for ch, sc in zip(chains, scores)]
    w1 = [_contract(t, _head_blocks(q["a_t"], pa), _NN, pa) for t, q in zip(t_inv, pre)]
    w2 = [_contract(t, _head_blocks(x[:c], pa), _NN, pa) for t, x in zip(t_inv, av)]
    g = [_contract(jnp.concatenate([x, q["r_t"]], axis=0), ch[6], _NT, pst) for x, q, ch in zip(w1, pre, chains)]
    u = [x[:c] + y for x, y in zip(g, w2)]
    ys = [x[c:] + _contract(sc["a_rb"], _head_blocks(uu, pa), _NN, pa) + y[c:]
          for x, sc, uu, y in zip(g, scores, u, av)]
    upd = [_contract(jnp.concatenate([uu, ch[1]], axis=0), jnp.concatenate([q["b_end"], q["k_end"]], axis=0), _TN, pst)
           for uu, ch, q in zip(u, chains, pre)]
    hr = lax.broadcasted_iota(I32, (RW, RW), 0) >> 6
    hc = lax.broadcasted_iota(I32, (RW, RW), 1) >> 6
    s_new = [ch[6] * q["decay"] + jnp.where(hr == hc, x, 0.0) for ch, q, x in zip(chains, pre, upd)]
    return ys, s_new


def _wkv_kernel(rf_ref, vf_ref, kkf_ref, lwf_ref, kdf_ref, bdf_ref, rb_ref, vb_ref, kkb_ref, lwb_ref, kdb_ref,
                bdb_ref, s0_ref, yf_ref, yb_ref, s_out_ref, s_scr, *, prec):
    step = pl.program_id(0)

    @pl.when(step == 0)
    def _():
        s_scr[...] = s0_ref[...]

    bsz = rf_ref.shape[0]
    for b0 in range(0, bsz, WKV_GROUP_BATCHES):
        group = range(b0, min(b0 + WKV_GROUP_BATCHES, bsz))
        chains = []
        for b in group:
            chains.append((rf_ref[b], vf_ref[b], kkf_ref[b], lwf_ref[b], kdf_ref[b], bdf_ref[b], s_scr[0, b], False))
            chains.append((rb_ref[b], vb_ref[b], kkb_ref[b], lwb_ref[b], kdb_ref[b], bdb_ref[b], s_scr[1, b], True))
        ys, s_new = _wkv_chunks(chains, prec)
        for i, b in enumerate(group):
            yf_ref[b] = ys[2 * i]
            yb_ref[b] = ys[2 * i + 1]
            s_scr[0, b] = s_new[2 * i]
            s_scr[1, b] = s_new[2 * i + 1]

    @pl.when(step == pl.num_programs(0) - 1)
    def _():
        s_out_ref[...] = s_scr[...]


def _wkv(r, v, kk, lw, kd, bd, s0, prec=None):
    prec = prec or WKV_PRECISION
    bsz, n, _ = r.shape
    steps = n // WKV_CHUNK
    blk = (bsz, WKV_CHUNK, RW)
    fwd = pl.BlockSpec(blk, lambda i: (0, i, 0))
    bwd = pl.BlockSpec(blk, lambda i: (0, steps - 1 - i, 0))
    bwd_dir = pl.BlockSpec(blk, lambda i: (0, steps - 1 - i, 1))
    state = pl.BlockSpec((2, bsz, RW, RW), lambda i: (0, 0, 0, 0))
    y_shape = jax.ShapeDtypeStruct((bsz, n, RW), F32)
    return pl.pallas_call(
        functools.partial(_wkv_kernel, prec=prec),
        grid=(steps,),
        in_specs=[fwd, fwd, fwd, fwd, fwd, fwd, bwd, bwd, bwd, bwd_dir, bwd_dir, bwd_dir, state],
        out_specs=[fwd, bwd, state],
        out_shape=[y_shape, y_shape, jax.ShapeDtypeStruct((2, bsz, RW, RW), F32)],
        scratch_shapes=[pltpu.VMEM((2, bsz, RW, RW), F32)],
        compiler_params=_cparams("arbitrary"),
        name="wkv",
    )(r, v, kk, lw, kd, bd, r, v, kk, lw, kd, bd, s0)


def _outproj_kernel(x_ref, conv_ref, att_ref, yf_ref, yb_ref, r_ref, v_ref, g_ref, kd_ref,
                    gt1_ref, sh2_ref, sc2_ref, g2_ref, w_ref, rk_ref, lnw_ref, lnb_ref, bd_ref, rwt_ref,
                    x1_ref, h2_ref, aff_ref):
    ones_bd = bd_ref[...]
    y = yf_ref[0] + yb_ref[0]
    mean = _group_sum(y, ones_bd) * (1.0 / HEAD)
    yc = y - mean
    var = _group_sum(yc * yc, ones_bd) * (1.0 / HEAD)
    yn = yc * lax.rsqrt(var + GN_EPS)
    kd = kd_ref[0]
    k_mean = 0.5 * (kd[:, :RW] + kd[:, RW:])
    bonus = _group_sum(r_ref[0] * k_mean * rk_ref[...], ones_bd) * v_ref[0]
    rwkv = (yn * lnw_ref[...] + lnb_ref[...] + bonus) * g_ref[0]

    mixed = (_dot(conv_ref[0].astype(BF16), w_ref[:CONV_W])
             + _dot(att_ref[0], w_ref[CONV_W:CONV_W + Q_COLS])
             + _dot(rwkv.astype(BF16), w_ref[CONV_W + Q_COLS:]))
    x1 = x_ref[0] + gt1_ref[0] * mixed
    x1_ref[0] = x1

    h2 = _modulated_norm(x1, g2_ref[...], sh2_ref[0], sc2_ref[0])
    h2_ref[0] = h2.astype(BF16)
    logits = _contract(rwt_ref[...], h2, _NT, "x3")
    e = jnp.exp(logits - jnp.max(logits, axis=0, keepdims=True))
    aff_ref[0] = e / jnp.sum(e, axis=0, keepdims=True)


def _outproj(x, conv, att, yf, yb, r, v, g, kd, gt1, sh2, sc2, g2, w_out, rk, lnw, lnb, ones_rw, router_t, *, tm):
    bsz, n, d = x.shape
    mod_map = (lambda b, i: (b, 0, 0)) if gt1.shape[0] == bsz else (lambda b, i: (0, 0, 0))
    tile = lambda w: pl.BlockSpec((1, tm, w), lambda b, i: (b, i, 0))
    mod = pl.BlockSpec((1, 1, d), mod_map)
    full = lambda a: pl.BlockSpec(a.shape, lambda b, i: (0,) * a.ndim)
    params = (g2, w_out, rk, lnw, lnb, ones_rw, router_t)
    return pl.pallas_call(
        _outproj_kernel,
        grid=(bsz, n // tm),
        in_specs=[tile(d), tile(CONV_W), tile(Q_COLS), tile(RW), tile(RW), tile(RW), tile(RW), tile(RW),
                  tile(2 * RW), mod, mod, mod] + [full(a) for a in params],
        out_specs=[tile(d), tile(d), pl.BlockSpec((1, N_EXPERTS, tm), lambda b, i: (b, 0, i))],
        out_shape=[
            jax.ShapeDtypeStruct((bsz, n, d), F32),
            jax.ShapeDtypeStruct((bsz, n, d), BF16),
            jax.ShapeDtypeStruct((bsz, N_EXPERTS, n), F32),
        ],
        compiler_params=_cparams("parallel", "parallel"),
        name="outproj",
    )(x, conv, att, yf, yb, r, v, g, kd, gt1, sh2, sc2, *params)


def _cumsum_lanes(x):
    n = x.shape[1]
    w = min(n, 256)
    ri = lax.broadcasted_iota(I32, (w, w), 0)
    ci = lax.broadcasted_iota(I32, (w, w), 1)
    tri = (ri <= ci).astype(BF16)
    carry = jnp.zeros((x.shape[0], 1), F32)
    parts = []
    for j in range(n // w):
        cs = _dot(x[:, j * w:(j + 1) * w].astype(BF16), tri) + carry
        carry = cs[:, w - 1:w]
        parts.append(cs)
    return parts[0] if len(parts) == 1 else jnp.concatenate(parts, axis=1)


def _to_token_major(x):
    n = x.shape[1]
    pad = jnp.zeros((LANES - x.shape[0], LANES), F32)
    blocks = [jnp.concatenate([x[:, j * LANES:(j + 1) * LANES], pad], axis=0).T for j in range(n // LANES)]
    return blocks[0] if len(blocks) == 1 else jnp.concatenate(blocks, axis=0)


def _route_kernel(aff_ref, idx_ref, aff_t_ref, *, cap, slots):
    aff = aff_ref[0]
    n_exp, n = aff.shape
    bits = pltpu.bitcast(aff, I32)

    def search(i, thr):
        cand = thr | (1 << (30 - i))
        cnt = jnp.sum((bits >= cand).astype(F32), axis=1, keepdims=True)
        return jnp.where(cnt >= cap, cand, thr)

    thr = lax.fori_loop(0, 31, search, jnp.zeros((aff.shape[0], 1), I32))
    above = bits > thr
    tied = bits == thr
    need = cap - jnp.sum(above.astype(F32), axis=1, keepdims=True)
    tied_f = tied.astype(F32)
    chosen = jnp.where(above | (tied & (_cumsum_lanes(tied_f) <= need)), 1.0, 0.0)
    rank = _cumsum_lanes(chosen)
    aff_t_ref[0] = _to_token_major(aff)
    rank_t = _to_token_major(rank)
    rows = min(n, 512)
    slot = lax.broadcasted_iota(I32, (rows, slots), 1).astype(F32)
    expert = lax.broadcasted_iota(I32, (n_exp, slots), 0)
    idx = jnp.zeros((n_exp, slots), F32)
    for e in range(n_exp):
        cnt = jnp.zeros((1, slots), F32)
        for r0 in range(0, n, rows):
            cnt = cnt + jnp.sum(jnp.where(rank_t[r0:r0 + rows, e:e + 1] <= slot, 1.0, 0.0), axis=0, keepdims=True)
        idx = jnp.where(expert == e, cnt, idx)
    idx_ref[0] = jnp.minimum(idx, n - 1.0).astype(I32)


def _route(aff, *, cap, slots):
    bsz, e, n = aff.shape
    return pl.pallas_call(
        functools.partial(_route_kernel, cap=cap, slots=slots),
        grid=(bsz,),
        in_specs=[pl.BlockSpec((1, e, n), lambda b: (b, 0, 0))],
        out_specs=[pl.BlockSpec((1, e, slots), lambda b: (b, 0, 0)),
                   pl.BlockSpec((1, n, LANES), lambda b: (b, 0, 0))],
        out_shape=[jax.ShapeDtypeStruct((bsz, e, slots), I32),
                   jax.ShapeDtypeStruct((bsz, n, LANES), F32)],
        compiler_params=_cparams("parallel"),
        name="route",
    )(aff)


ROW_GROUP = 2 * SUBLANES
RELAYOUT_ROWS = 256
MOVER_ROWS = 1024


def _gather_kernel(idx_ref, h_ref, aff_ref, xs_ref, gs_ref, h_rows, a_rows):
    n, d = h_ref.shape

    @pl.when(pl.program_id(1) == 0)
    def _():
        for r in range(0, n, RELAYOUT_ROWS):
            m = min(RELAYOUT_ROWS, n - r)
            h_rows[r:r + m] = h_ref[r:r + m, :].astype(F32).reshape(m, 1, d)
            a_rows[r:r + m] = aff_ref[r:r + m, :].reshape(m, 1, LANES)

    def rows(c, carry):
        base = pl.multiple_of(c * ROW_GROUP, ROW_GROUP)
        tokens = [idx_ref[0, 0, c * ROW_GROUP + u] for u in range(ROW_GROUP)]
        xs_ref[pl.ds(base, ROW_GROUP), :] = jnp.concatenate([h_rows[i] for i in tokens], axis=0).astype(BF16)
        gs_ref[pl.ds(base, ROW_GROUP), :] = jnp.concatenate([a_rows[i] for i in tokens], axis=0)
        return carry

    lax.fori_loop(0, xs_ref.shape[0] // ROW_GROUP, rows, 0)


def _gather(idx, h2, aff_t):
    bsz, n, d = h2.shape
    be, _, slots = idx.shape
    e = be // bsz
    return pl.pallas_call(
        _gather_kernel,
        grid=(bsz, e),
        in_specs=[pl.BlockSpec((1, 1, slots), lambda b, j: (b * e + j, 0, 0), memory_space=pltpu.SMEM),
                  pl.BlockSpec((None, n, d), lambda b, j: (b, 0, 0), pipeline_mode=pl.Buffered(1)),
                  pl.BlockSpec((None, n, LANES), lambda b, j: (b, 0, 0))],
        out_specs=[pl.BlockSpec((slots, d), lambda b, j: (b * e + j, 0)),
                   pl.BlockSpec((slots, LANES), lambda b, j: (b * e + j, 0))],
        out_shape=[jax.ShapeDtypeStruct((be * slots, d), BF16),
                   jax.ShapeDtypeStruct((be * slots, LANES), F32)],
        scratch_shapes=[pltpu.VMEM((n, 1, d), F32), pltpu.VMEM((n, 1, LANES), F32)],
        compiler_params=_cparams("parallel", "arbitrary"),
        name="gather",
    )(idx, h2, aff_t)


def _experts_kernel(*refs, streams):
    xs_refs, gs_refs = refs[:streams], refs[streams:2 * streams]
    wg_ref, wu_ref, wd_ref = refs[2 * streams:2 * streams + 3]
    y_refs, (wg_s, wu_s, wd_s) = refs[2 * streams + 3:3 * streams + 3], refs[3 * streams + 3:]
    e = pl.program_id(0)

    @pl.when(pl.program_id(1) == 0)
    def _():
        wg_s[...] = wg_ref[0, 0].astype(BF16)
        wu_s[...] = wu_ref[0, 0].astype(BF16)
        wd_s[...] = wd_ref[0, 0].astype(BF16)

    for xs_ref, gs_ref, y_ref in zip(xs_refs, gs_refs, y_refs):
        xs = xs_ref[...]
        hg = _dot(xs, wg_s[...])
        hu = _dot(xs, wu_s[...])
        hid = (hg * _sigmoid(hg)) * hu
        gs = gs_ref[...]
        lane = lax.broadcasted_iota(I32, gs.shape, 1)
        gate = jnp.sum(jnp.where(lane == e, gs, 0.0), axis=1, keepdims=True)
        y_ref[...] = (_dot(hid.astype(BF16), wd_s[...]) * gate).astype(BF16)


def _experts(streams, w_gate, w_up, w_down, layer, *, bsz):
    e, d, f = w_gate.shape[1], w_gate.shape[2], w_gate.shape[3]
    slot_counts = [xs.shape[0] // (bsz * e) for xs, _ in streams]
    rows = lambda s, w: pl.BlockSpec((s, w), lambda j, b: (b * e + j, 0))
    return pl.pallas_call(
        functools.partial(_experts_kernel, streams=len(streams)),
        grid=(e, bsz),
        in_specs=[rows(s, d) for s in slot_counts] + [rows(s, LANES) for s in slot_counts] + [
            pl.BlockSpec((1, 1, d, f), lambda j, b: (layer, j, 0, 0)),
            pl.BlockSpec((1, 1, d, f), lambda j, b: (layer, j, 0, 0)),
            pl.BlockSpec((1, 1, f, d), lambda j, b: (layer, j, 0, 0))],
        out_specs=[rows(s, d) for s in slot_counts],
        out_shape=[jax.ShapeDtypeStruct(xs.shape, BF16) for xs, _ in streams],
        scratch_shapes=[pltpu.VMEM((d, f), BF16), pltpu.VMEM((d, f), BF16), pltpu.VMEM((f, d), BF16)],
        compiler_params=_cparams("parallel", "arbitrary"),
        name="experts",
    )(*[xs for xs, _ in streams], *[gs for _, gs in streams], w_gate, w_up, w_down)


def _combine_kernel(idx_ref, y_ref, x_ref, gt_ref, o_ref, acc, *, cap, slots):
    e = pl.program_id(1)
    n, d = x_ref.shape

    @pl.when(e == 0)
    def _():
        acc[...] = jnp.zeros_like(acc)

    for first in range(0, y_ref.shape[0], slots):
        def rows(c, carry, first=first):
            base = pl.multiple_of(first + c * ROW_GROUP, ROW_GROUP)
            tile = y_ref[pl.ds(base, ROW_GROUP), :].astype(F32).reshape(ROW_GROUP, 1, d)
            tokens = [idx_ref[0, 0, first + c * ROW_GROUP + u] for u in range(ROW_GROUP)]
            sums = [acc[i] + tile[u] for u, i in enumerate(tokens)]
            for i, s in zip(tokens, sums):
                acc[i] = s
            return carry

        lax.fori_loop(0, cap // ROW_GROUP, rows, 0)

    @pl.when(e == pl.num_programs(1) - 1)
    def _():
        for r in range(0, n, RELAYOUT_ROWS):
            m = min(RELAYOUT_ROWS, n - r)
            o_ref[r:r + m, :] = x_ref[r:r + m, :] + gt_ref[0] * acc[r:r + m].reshape(m, d)


def _combine(idx, y, x1, gt2, *, cap, slots):
    bsz, n, d = x1.shape
    be, _, rows = idx.shape
    e = be // bsz
    mod_map = (lambda b, j: (b, 0, 0)) if gt2.shape[0] == bsz else (lambda b, j: (0, 0, 0))
    return pl.pallas_call(
        functools.partial(_combine_kernel, cap=cap, slots=slots),
        grid=(bsz, e),
        in_specs=[pl.BlockSpec((1, 1, rows), lambda b, j: (b * e + j, 0, 0), memory_space=pltpu.SMEM),
                  pl.BlockSpec((rows, d), lambda b, j: (b * e + j, 0)),
                  pl.BlockSpec((None, n, d), lambda b, j: (b, 0, 0), pipeline_mode=pl.Buffered(1)),
                  pl.BlockSpec((1, 1, d), mod_map)],
        out_specs=pl.BlockSpec((None, n, d), lambda b, j: (b, 0, 0), pipeline_mode=pl.Buffered(1)),
        out_shape=jax.ShapeDtypeStruct(x1.shape, F32),
        scratch_shapes=[pltpu.VMEM((n, 1, d), F32)],
        compiler_params=_cparams("parallel", "arbitrary"),
        name="combine",
    )(idx, y, x1, gt2)


def _rope_tables(n):
    rows = n // GRID_W
    row = jnp.repeat(jnp.arange(rows, dtype=F32), GRID_W)
    col = jnp.tile(jnp.arange(GRID_W, dtype=F32), rows)
    half = HEAD // 2
    inv = ROPE_THETA ** (-jnp.arange(0, half, 2, dtype=F32) / half)
    ang = jnp.concatenate([row[:, None] * inv, col[:, None] * inv], axis=-1)
    cos, sin = jnp.cos(ang), jnp.sin(ang)
    cos = jnp.concatenate([cos, cos], axis=-1)
    sin = jnp.concatenate([-sin, sin], axis=-1)
    return jnp.tile(cos, (1, LANES // HEAD)), jnp.tile(sin, (1, LANES // HEAD))


def _block_diag2(m):
    z = jnp.zeros_like(m[0])
    return jnp.concatenate([jnp.concatenate([m[0], z], axis=1), jnp.concatenate([z, m[1]], axis=1)], axis=0)


def _permute_q_heads(w, axis):
    idx = jnp.concatenate([jnp.arange(h * HEAD, (h + 1) * HEAD) for h in Q_HEAD_ORDER])
    return jnp.take(w, idx, axis=axis)


def _layer_params(l, w_in, w_out, conv_w, q_norm_g, k_norm_g, rw_mu, rw_w0, rw_w_b, rw_a0, rw_a_b, rw_g_b,
                  rw_k_k, rw_k_a, rw_r_k, rw_ln_w, rw_ln_b, router_w, norm1_g, norm2_g):
    wi = w_in[l]
    q_cols = _permute_q_heads(wi[:, CONV_COLS:CONV_COLS + Q_COLS], 1)
    wi = jnp.concatenate([wi[:, :CONV_COLS], q_cols, wi[:, CONV_COLS + Q_COLS:],
                          jnp.zeros((wi.shape[0], RW_COLS_PAD - RW_COLS), F32)], axis=1).astype(BF16)
    wo = w_out[l]
    wo = jnp.concatenate([wo[:CONV_W], _permute_q_heads(wo[CONV_W:CONV_W + Q_COLS], 0),
                          wo[CONV_W + Q_COLS:]], axis=0).astype(BF16)
    row = lambda a: a.reshape(1, -1)
    return dict(
        w_in=wi, w_out=wo, conv_w=conv_w[l],
        qg=row(jnp.tile(q_norm_g[l], N_Q_HEADS)), kg=row(jnp.tile(k_norm_g[l], N_KV_HEADS)),
        mu=row(jnp.pad(rw_mu[l], (0, RW_COLS_PAD - RW_COLS))),
        w0=row(rw_w0[l]), wb=_block_diag2(rw_w_b[l]), a0=row(rw_a0[l]), ab=_block_diag2(rw_a_b[l]),
        gb=jnp.pad(rw_g_b[l], ((0, GATE_LORA_PAD - GATE_LORA), (0, 0))),
        kk=row(rw_k_k[l]), ka=row(rw_k_a[l]), rk=row(rw_r_k[l]), lnw=row(rw_ln_w[l]), lnb=row(rw_ln_b[l]),
        router_t=router_w[l].T, g1=row(norm1_g[l]), g2=row(norm2_g[l]),
    )


def _moe(streams, w_gate, w_up, w_down, layer):
    bsz, _, d = streams[0][0].shape
    routed = []
    for x1, h2, aff, _ in streams:
        n = x1.shape[1]
        cap = CAPACITY_FACTOR * n // N_EXPERTS
        slots = -(-cap // LANES) * LANES
        idx, aff_t = _route(aff, cap=cap, slots=slots)
        group = N_EXPERTS if N_EXPERTS * slots <= 2 * MOVER_ROWS else max(1, MOVER_ROWS // slots)
        idx = idx.reshape(bsz * N_EXPERTS // group, 1, group * slots)
        routed.append((idx, cap, slots) + tuple(_gather(idx, h2, aff_t)))
    ys = _experts([(r[3], r[4]) for r in routed], w_gate, w_up, w_down, layer, bsz=bsz)
    return [_combine(idx, y, x1, gt2, cap=cap, slots=slots)
            for (x1, _, _, gt2), (idx, cap, slots, _, _), y in zip(streams, routed, ys)]


def kernel(x, c, ctx, c_ctx, ada_w, ada_b, norm1_g, norm2_g, w_in, w_out, conv_w, q_norm_g, k_norm_g, rw_mu,
           rw_w0, rw_w_b, rw_a0, rw_a_b, rw_g_b, rw_k_k, rw_k_a, rw_r_k, rw_ln_w, rw_ln_b, router_w,
           exp_w_gate, exp_w_up, exp_w_down):
    bsz, n, d = x.shape
    n_ctx = ctx.shape[1]
    depth = ada_w.shape[0]
    tm = min(512, n)
    tm_c = min(512, n_ctx)
    tq = min(256, n)
    tq_c = min(256, n_ctx)

    rows = -(-(bsz + 1) // SUBLANES) * SUBLANES
    cvecs = jnp.concatenate([c, c_ctx[None], jnp.zeros((rows - bsz - 1, d), F32)], axis=0)
    mod = _adaln(cvecs, ada_w, ada_b).reshape(depth, rows, N_MOD, d)

    cos, sin = _rope_tables(n)
    cos_c = jnp.ones((n_ctx, LANES), F32)
    sin_c = jnp.zeros((n_ctx, LANES), F32)
    ones_q = _block_ones(Q_COLS)
    ones_rw = _block_ones(RW)
    zero_state = jnp.zeros((2, bsz, RW, RW), F32)

    xc = ctx
    for l in range(depth):
        update_ctx = l < depth - 1
        p = _layer_params(l, w_in, w_out, conv_w, q_norm_g, k_norm_g, rw_mu, rw_w0, rw_w_b, rw_a0, rw_a_b, rw_g_b,
                          rw_k_k, rw_k_a, rw_r_k, rw_ln_w, rw_ln_b, router_w, norm1_g, norm2_g)
        lat = [mod[l, :bsz, i][:, None, :] for i in range(N_MOD)]
        cmod = [mod[l, bsz:bsz + 1, i][:, None, :] for i in range(N_MOD)]

        pc_l, q_l, k_l, v_l, rw_l = _inproj(x, lat[0], lat[1], p["g1"], p["w_in"], ones_q, p["qg"], p["kg"],
                                            cos, sin, rope=True, tm=tm)
        pc_c, q_c, k_c, v_c, rw_c = _inproj(xc, cmod[0], cmod[1], p["g1"], p["w_in"], ones_q, p["qg"], p["kg"],
                                            cos_c, sin_c, rope=False, tm=tm_c)
        feat_args = (p["conv_w"], p["mu"], p["w0"], p["wb"], p["a0"], p["ab"], p["gb"], p["kk"], p["ka"], ones_rw)
        conv_l, r_l, vv_l, kk_l, g_l, lw_l, kd_l, bd_l = _features(pc_l, rw_l, *feat_args, tm=tm)
        conv_c, r_c, vv_c, kk_c, g_c, lw_c, kd_c, bd_c = _features(pc_c, rw_c, *feat_args, tm=min(256, n_ctx))

        att_l = _attend(q_l, jnp.concatenate([k_l, k_c], axis=1), jnp.concatenate([v_l, v_c], axis=1), tq=tq)

        yf_c, yb_c, s_c = _wkv(r_c, vv_c, kk_c, lw_c, kd_c, bd_c, zero_state)
        yf_l, yb_l, _ = _wkv(r_l, vv_l, kk_l, lw_l, kd_l, bd_l, s_c)
        ys_l, ys_c = (yf_l, yb_l), (yf_c, yb_c)

        out_args = (p["g2"], p["w_out"], p["rk"], p["lnw"], p["lnb"], ones_rw, p["router_t"])
        x1, h2, aff = _outproj(x, conv_l, att_l, ys_l[0], ys_l[1], r_l, vv_l, g_l, kd_l,
                               lat[2], lat[3], lat[4], *out_args, tm=tm)
        streams = [(x1, h2, aff, lat[5])]
        if update_ctx:
            att_c = _attend(q_c, k_c, v_c, tq=tq_c)
            xc1, hc2, aff_c = _outproj(xc, conv_c, att_c, ys_c[0], ys_c[1], r_c, vv_c, g_c, kd_c,
                                       cmod[2], cmod[3], cmod[4], *out_args, tm=tm_c)
            streams.append((xc1, hc2, aff_c, cmod[5]))
        outs = _moe(streams, exp_w_gate, exp_w_up, exp_w_down, l)
        x = outs[0]
        if update_ctx:
            xc = outs[1]
    return x
```
